```python
import math, functools
import jax, jax.numpy as jnp
from jax import lax
import numpy as np

D_MODEL = 1024
BATCH = 4
SEQ = 4096
DEPTH = 2

GRID_W = 64
CTX_LEN = 256
N_MIXERS = 2
N_A_LAYERS = (DEPTH + 1) // 2
N_B_LAYERS = DEPTH // 2
D_FF = 2816
N_MOD = 9
EPS = 1e-6
ROPE_BASE = 10000.0
Q_BLOCK = 128
NEG_INF = -1e30

A_HEADS = 8
A_HEAD_DIM = 64
A_V_DIM = 2 * A_HEAD_DIM
A_QKV_COLS = 3 * A_HEADS * A_V_DIM
A_WIDTH = A_HEADS * A_V_DIM

B_Q_HEADS = 16
B_KV_HEADS = 4
B_HEAD_DIM = 64
B_GROUP = B_Q_HEADS // B_KV_HEADS
B_QKV_COLS = (B_Q_HEADS + 2 * B_KV_HEADS) * B_HEAD_DIM
B_WIDTH = B_Q_HEADS * B_HEAD_DIM
WINDOW = 128
BAND_BLOCK = 128

kernel_name = "hybrid_diffattn_swagqa_macaron_dit"


def rms(x):
    xf = x.astype(jnp.float32)
    return (xf * lax.rsqrt(jnp.mean(xf * xf, axis=-1, keepdims=True) + EPS)).astype(x.dtype)


def rms_gain(x, g):
    return rms(x) * g


def modulate(h, shift, scale):
    return h * (1 + scale) + shift


def swiglu(h, wi, wo):
    g, u = jnp.split(h @ wi, 2, axis=-1)
    return (jax.nn.silu(g) * u) @ wo


def axial_rope_tables(rows, head_dim):
    nf = head_dim // 4
    inv = ROPE_BASE ** (-jnp.arange(nf, dtype=jnp.float32) / nf)
    row = jnp.broadcast_to(jnp.arange(rows, dtype=jnp.float32)[:, None], (rows, GRID_W)).reshape(-1)
    col = jnp.broadcast_to(jnp.arange(GRID_W, dtype=jnp.float32)[None, :], (rows, GRID_W)).reshape(-1)
    ang = jnp.stack([row[:, None] * inv, col[:, None] * inv], axis=1)
    ang = jnp.stack([ang, ang], axis=2).reshape(rows * GRID_W, head_dim)
    return jnp.cos(ang), jnp.sin(ang)


def apply_rope(x, cos, sin):
    dh = x.shape[-1]
    xr = x.reshape(x.shape[:-1] + (2, 2, dh // 4))
    rot = jnp.stack([-xr[..., 1, :], xr[..., 0, :]], axis=-2).reshape(x.shape)
    return (x * cos + rot * sin).astype(x.dtype)


def diff_attention(h_lat, h_ctx, w_qkv, w_o, q_gain, k_gain, lam_vec, subln_gain, lam_init, cos, sin, ctx_out):
    B, L, _ = h_lat.shape

    def project(h):
        q, k, v = jnp.split(h @ w_qkv, [A_WIDTH, 2 * A_WIDTH], axis=-1)
        q = rms_gain(q.reshape(B, -1, A_HEADS, 2, A_HEAD_DIM), q_gain)
        k = rms_gain(k.reshape(B, -1, A_HEADS, 2, A_HEAD_DIM), k_gain)
        return q, k, v.reshape(B, -1, A_HEADS, A_V_DIM)

    q_l, k_l, v_l = project(h_lat)
    q_c, k_c, v_c = project(h_ctx)
    rc, rs = cos[:, None, None, :], sin[:, None, None, :]
    q_l, k_l = apply_rope(q_l, rc, rs), apply_rope(k_l, rc, rs)

    lv = lam_vec.astype(jnp.float32)
    lam = jnp.exp(jnp.sum(lv[0] * lv[1])) - jnp.exp(jnp.sum(lv[2] * lv[3])) + lam_init
    scale = A_HEAD_DIM ** -0.5

    def attend(q, k, v):
        s = jnp.einsum('bqhcd,bkhcd->bhcqk', q, k).astype(jnp.float32) * scale
        p = jax.nn.softmax(s, axis=-1)
        a = (p[:, :, 0] - lam * p[:, :, 1]).astype(v.dtype)
        return jnp.einsum('bhqk,bkhe->bqhe', a, v)

    def finish(o):
        o = rms_gain(o, subln_gain) * (1.0 - lam_init)
        return o.reshape(o.shape[0], o.shape[1], A_WIDTH) @ w_o

    k_all = jnp.concatenate([k_l, k_c], axis=1)
    v_all = jnp.concatenate([v_l, v_c], axis=1)
    nb = L // Q_BLOCK
    qb = q_l.reshape(B, nb, Q_BLOCK, A_HEADS, 2, A_HEAD_DIM).transpose(1, 0, 2, 3, 4, 5)
    o_l = lax.map(lambda qblk: attend(qblk, k_all, v_all), qb)
    o_l = o_l.transpose(1, 0, 2, 3, 4).reshape(B, L, A_HEADS, A_V_DIM)
    out_l = finish(o_l)
    out_c = finish(attend(q_c, k_c, v_c)) if ctx_out else None
    return out_l, out_c


def window_gqa(h_lat, h_ctx, w_qkv, w_o, q_gain, k_gain, sink, cos, sin, ctx_out):
    B, L, _ = h_lat.shape
    nb = L // BAND_BLOCK
    BB = BAND_BLOCK

    def project(h):
        q, k, v = jnp.split(h @ w_qkv, [B_WIDTH, B_WIDTH + B_KV_HEADS * B_HEAD_DIM], axis=-1)
        q = rms_gain(q.reshape(B, -1, B_KV_HEADS, B_GROUP, B_HEAD_DIM), q_gain)
        k = rms_gain(k.reshape(B, -1, B_KV_HEADS, B_HEAD_DIM), k_gain)
        return q, k, v.reshape(B, -1, B_KV_HEADS, B_HEAD_DIM)

    q_l, k_l, v_l = project(h_lat)
    q_c, k_c, v_c = project(h_ctx)
    q_l = apply_rope(q_l, cos[:, None, None, :], sin[:, None, None, :])
    k_l = apply_rope(k_l, cos[:, None, :], sin[:, None, :])
    scale = B_HEAD_DIM ** -0.5
    sink_f = sink.astype(jnp.float32).reshape(B_KV_HEADS, B_GROUP)

    def band(t):
        tb = jnp.pad(t, ((0, 0), (BB, BB), (0, 0), (0, 0))).reshape(B, nb + 2, BB, B_KV_HEADS, B_HEAD_DIM)
        return jnp.concatenate([tb[:, :-2], tb[:, 1:-1], tb[:, 2:]], axis=2)

    k_w, v_w = band(k_l), band(v_l)
    qb = q_l.reshape(B, nb, BB, B_KV_HEADS, B_GROUP, B_HEAD_DIM)
    s_w = jnp.einsum('bnqhgd,bnkhd->bnhgqk', qb, k_w).astype(jnp.float32) * scale
    s_c = jnp.einsum('bnqhgd,bkhd->bnhgqk', qb, k_c).astype(jnp.float32) * scale
    qi = jnp.arange(BB)
    kj = jnp.arange(3 * BB)
    rel = kj[None, :] - BB - qi[:, None]
    kpos = jnp.arange(nb)[:, None] * BB - BB + kj[None, :]
    valid = (jnp.abs(rel) <= WINDOW)[None] & ((kpos >= 0) & (kpos < L))[:, None, :]
    s_w = jnp.where(valid[None, :, None, None], s_w, NEG_INF)
    sink_b = jnp.broadcast_to(sink_f[None, None, :, :, None, None], s_w.shape[:-1] + (1,))
    n_ctx = k_c.shape[1]
    p = jax.nn.softmax(jnp.concatenate([s_w, s_c, sink_b], axis=-1), axis=-1)
    p_w = p[..., :3 * BB].astype(v_w.dtype)
    p_c = p[..., 3 * BB:3 * BB + n_ctx].astype(v_c.dtype)
    o = jnp.einsum('bnhgqk,bnkhd->bnqhgd', p_w, v_w) + jnp.einsum('bnhgqk,bkhd->bnqhgd', p_c, v_c)
    out_l = o.reshape(B, L, B_WIDTH) @ w_o
    out_c = None
    if ctx_out:
        s = jnp.einsum('bqhgd,bkhd->bhgqk', q_c, k_c).astype(jnp.float32) * scale
        sb = jnp.broadcast_to(sink_f[None, :, :, None, None], s.shape[:-1] + (1,))
        pc = jax.nn.softmax(jnp.concatenate([s, sb], axis=-1), axis=-1)[..., :n_ctx].astype(v_c.dtype)
        oc = jnp.einsum('bhgqk,bkhd->bqhgd', pc, v_c)
        out_c = oc.reshape(B, n_ctx, B_WIDTH) @ w_o
    return out_l, out_c


def layer(x, xc, mod_l, mod_c, pre_wi, pre_wo, post_wi, post_wo, mixer, ctx_out):
    sh1, sc1, g1, sh2, sc2, g2, sh3, sc3, g3 = jnp.split(mod_l, N_MOD, axis=-1)
    ch1, cc1, cg1, ch2, cc2, cg2, ch3, cc3, cg3 = jnp.split(mod_c, N_MOD, axis=-1)

    def ffn_step(t, sh, sc, g, wi, wo):
        return t + 0.5 * g * swiglu(modulate(rms(t), sh, sc), wi, wo)

    x = ffn_step(x, sh1, sc1, g1, pre_wi, pre_wo)
    xc = ffn_step(xc, ch1, cc1, cg1, pre_wi, pre_wo)
    o_l, o_c = mixer(h_lat=modulate(rms(x), sh2, sc2), h_ctx=modulate(rms(xc), ch2, cc2), ctx_out=ctx_out)
    x = x + g2 * o_l
    x = ffn_step(x, sh3, sc3, g3, post_wi, post_wo)
    if ctx_out:
        xc = xc + cg2 * o_c
        xc = ffn_step(xc, ch3, cc3, cg3, post_wi, post_wo)
    return x, xc


def setup_inputs(seed: int = 0) -> dict:
    key = jax.random.key(seed)
    ks = jax.random.split(key, 24)
    f32 = jnp.float32
    D = D_MODEL

    def nrm(k, shape, s):
        return jax.random.normal(k, shape, f32) * s

    return {
        "x": nrm(ks[0], (BATCH, SEQ, D), 1.0),
        "c": nrm(ks[1], (BATCH, D), 1.0),
        "ctx": nrm(ks[2], (BATCH, CTX_LEN, D), 1.0),
        "c_ctx": nrm(ks[3], (D,), 1.0),
        "ada_w": nrm(ks[4], (DEPTH, D, N_MOD * D), 0.5 * D ** -0.5),
        "ada_b": nrm(ks[5], (DEPTH, N_MOD * D), 0.02),
        "ffn_pre_wi": nrm(ks[6], (DEPTH, D, 2 * D_FF), D ** -0.5),
        "ffn_pre_wo": nrm(ks[7], (DEPTH, D_FF, D), D_FF ** -0.5),
        "ffn_post_wi": nrm(ks[8], (DEPTH, D, 2 * D_FF), D ** -0.5),
        "ffn_post_wo": nrm(ks[9], (DEPTH, D_FF, D), D_FF ** -0.5),
        "a_w_qkv": nrm(ks[10], (N_A_LAYERS, D, A_QKV_COLS), D ** -0.5),
        "a_w_o": nrm(ks[11], (N_A_LAYERS, A_WIDTH, D), A_WIDTH ** -0.5),
        "a_q_gain": 1.0 + nrm(ks[12], (N_A_LAYERS, A_HEAD_DIM), 0.1),
        "a_k_gain": 1.0 + nrm(ks[13], (N_A_LAYERS, A_HEAD_DIM), 0.1),
        "a_lambda": nrm(ks[14], (N_A_LAYERS, 4, A_HEAD_DIM), 0.1),
        "a_subln_gain": 1.0 + nrm(ks[15], (N_A_LAYERS, A_V_DIM), 0.1),
        "b_w_qkv": nrm(ks[16], (N_B_LAYERS, D, B_QKV_COLS), D ** -0.5),
        "b_w_o": nrm(ks[17], (N_B_LAYERS, B_WIDTH, D), B_WIDTH ** -0.5),
        "b_q_gain": 1.0 + nrm(ks[18], (N_B_LAYERS, B_HEAD_DIM), 0.1),
        "b_k_gain": 1.0 + nrm(ks[19], (N_B_LAYERS, B_HEAD_DIM), 0.1),
        "b_sink": nrm(ks[20], (N_B_LAYERS, B_Q_HEADS), 0.5),
    }


def reference(x, c, ctx, c_ctx, ada_w, ada_b, ffn_pre_wi, ffn_pre_wo, ffn_post_wi, ffn_post_wo,
              a_w_qkv, a_w_o, a_q_gain, a_k_gain, a_lambda, a_subln_gain,
              b_w_qkv, b_w_o, b_q_gain, b_k_gain, b_sink):
    n_tok = x.shape[1]
    rows = n_tok // GRID_W
    cos, sin = axial_rope_tables(rows, A_HEAD_DIM)
    xc = ctx
    for i in range(DEPTH):
        mod_l = (jax.nn.silu(c) @ ada_w[i] + ada_b[i])[:, None, :]
        mod_c = jax.nn.silu(c_ctx) @ ada_w[i] + ada_b[i]
        ctx_out = i < DEPTH - 1
        j = i // N_MIXERS
        if i % N_MIXERS == 0:
            mixer = functools.partial(diff_attention, w_qkv=a_w_qkv[j], w_o=a_w_o[j], q_gain=a_q_gain[j],
                                      k_gain=a_k_gain[j], lam_vec=a_lambda[j], subln_gain=a_subln_gain[j],
                                      lam_init=0.8 - 0.6 * math.exp(-0.3 * i), cos=cos, sin=sin)
        else:
            mixer = functools.partial(window_gqa, w_qkv=b_w_qkv[j], w_o=b_w_o[j], q_gain=b_q_gain[j],
                                      k_gain=b_k_gain[j], sink=b_sink[j], cos=cos, sin=sin)
        x, xc = layer(x, xc, mod_l, mod_c, ffn_pre_wi[i], ffn_pre_wo[i], ffn_post_wi[i], ffn_post_wo[i],
                      mixer, ctx_out)
    return x
```

```python
import functools
import math

import jax
import jax.numpy as jnp
from jax import lax
from jax.experimental import pallas as pl
from jax.experimental.pallas import tpu as pltpu

F32 = jnp.float32
BF16 = jnp.bfloat16

LANES = 128
MXU_COLS = 256
HEAD_DIM = 64
N_MOD = 9
EPS = 1e-6
ROPE_BASE = 10000.0
GRID_W = 64
WINDOW_BLOCK = 128
NEG_INF = -1e30
LOG2E = math.log2(math.e)
VMEM_LIMIT = 56 * 1024 * 1024

A_HEADS = 8
B_Q_HEADS = 16
B_KV_HEADS = 4
B_GROUP = B_Q_HEADS // B_KV_HEADS


def _resident(shape):
    nd = len(shape)
    return pl.BlockSpec(shape, lambda *_: (0,) * nd, pipeline_mode=pl.Buffered(1))


def _ada_kernel(cs_ref, w_ref, b_ref, out_ref):
    s = cs_ref[...]
    s = s * jax.nn.sigmoid(s)
    out_ref[0] = jnp.dot(s.astype(BF16), w_ref[0].astype(BF16),
                         preferred_element_type=F32) + b_ref[0]


def _ada(cs, ada_w, ada_b):
    depth, d, n = ada_w.shape
    rows = cs.shape[0]
    tn = n // 8
    return pl.pallas_call(
        _ada_kernel,
        grid=(depth, n // tn),
        in_specs=[
            pl.BlockSpec((rows, d), lambda i, j: (0, 0)),
            pl.BlockSpec((1, d, tn), lambda i, j: (i, 0, j)),
            pl.BlockSpec((1, 1, tn), lambda i, j: (i, 0, j)),
        ],
        out_specs=pl.BlockSpec((1, rows, tn), lambda i, j: (i, 0, j)),
        out_shape=jax.ShapeDtypeStruct((depth, rows, n), F32),
        compiler_params=pltpu.CompilerParams(
            dimension_semantics=("arbitrary", "arbitrary"),
            vmem_limit_bytes=VMEM_LIMIT),
        name="ada_mod",
    )(cs, ada_w, ada_b.reshape(depth, 1, n))


def _rms(x):
    return x * lax.rsqrt(jnp.mean(x * x, axis=-1, keepdims=True) + EPS)


def _modulated(x, shift, scale):
    return (_rms(x) * (1.0 + scale) + shift).astype(BF16)


def _ffn_half_step(x, shift, scale, gate, wi_ref, wo_ref):
    d_ff = wo_ref.shape[0]
    h = _modulated(x, shift, scale)
    gu = jnp.dot(h, wi_ref[...], preferred_element_type=F32)
    g = gu[:, :d_ff]
    u = gu[:, d_ff:]
    act = (g * jax.nn.sigmoid(g) * u).astype(BF16)
    ff = jnp.dot(act, wo_ref[...], preferred_element_type=F32)
    return x + (0.5 * gate) * ff


def _pre_kernel(*refs, n_q, n_k, v_cols, rope):
    if rope:
        (x_ref, mod_ref, wi_ref, wo_ref, wqkv_ref, bd_ref, gq_ref, gk_ref,
         cos_ref, sa_ref, sb_ref, x_out, q_out, k_out, vt_out) = refs
    else:
        (x_ref, mod_ref, wi_ref, wo_ref, wqkv_ref, bd_ref, gq_ref, gk_ref,
         x_out, q_out, k_out, vt_out) = refs
    mod = mod_ref[0]
    x1 = _ffn_half_step(x_ref[0], mod[0:1], mod[1:2], mod[2:3], wi_ref, wo_ref)
    x_out[0] = x1
    h = _modulated(x1, mod[3:4], mod[4:5])
    qkv = jnp.dot(h, wqkv_ref[...], preferred_element_type=F32)

    bd = bd_ref[...]
    if rope:
        cos, sa, sb = cos_ref[...], sa_ref[...], sb_ref[...]

    def norm_rope_store(col0, n_slabs, gain, out):
        for s in range(0, n_slabs, 2):
            y = qkv[:, col0 + s * LANES: col0 + (s + 2) * LANES]
            ss = jnp.dot((y * y).astype(BF16), bd, preferred_element_type=F32)
            y = y * lax.rsqrt(ss * (1.0 / HEAD_DIM) + EPS)
            for t in range(2):
                z = y[:, t * LANES:(t + 1) * LANES] * gain
                if rope:
                    z = (z * cos + pltpu.roll(z, LANES - 16, 1) * sa
                         + pltpu.roll(z, 16, 1) * sb)
                out[0, s + t] = z.astype(BF16)

    norm_rope_store(0, n_q, gq_ref[...], q_out)
    norm_rope_store(n_q * LANES, n_k, gk_ref[...], k_out)

    v0 = (n_q + n_k) * LANES
    vt = qkv[:, v0:v0 + v_cols].T.astype(BF16)
    chunk = vt_out.shape[3]
    for c in range(vt_out.shape[1]):
        vt_out[0, c] = vt[:, c * chunk:(c + 1) * chunk]


def _post_kernel(x_ref, o_ref, mod_ref, wout_ref, wi_ref, wo_ref, x_out):
    mod = mod_ref[0]
    attn = jnp.dot(o_ref[0], wout_ref[...], preferred_element_type=F32)
    x2 = x_ref[0] + mod[5:6] * attn
    x_out[0] = _ffn_half_step(x2, mod[6:7], mod[7:8], mod[8:9], wi_ref, wo_ref)


def _pre_call(x, mod, wi, wo, wqkv, bd, gq, gk, rope_tabs, *, n_q, n_k, v_cols,
              tm, vt_chunk, name):
    bx, lx, d = x.shape
    per_batch_mod = mod.shape[0] > 1
    rope = rope_tabs is not None
    grid = (bx, lx // tm)
    mod_map = (lambda b, i: (b, 0, 0)) if per_batch_mod else (lambda b, i: (0, 0, 0))
    in_specs = [
        pl.BlockSpec((1, tm, d), lambda b, i: (b, i, 0)),
        pl.BlockSpec((1, N_MOD, d), mod_map),
        _resident(wi.shape), _resident(wo.shape), _resident(wqkv.shape),
        _resident(bd.shape), _resident(gq.shape), _resident(gk.shape),
    ]
    args = [x, mod, wi, wo, wqkv, bd, gq, gk]
    if rope:
        in_specs += [pl.BlockSpec((tm, LANES), lambda b, i: (i, 0))] * 3
        args += list(rope_tabs)
    if tm >= vt_chunk:
        vt_block = (1, tm // vt_chunk, v_cols, vt_chunk)
        vt_map = lambda b, i: (b, i, 0, 0)
    else:
        per = vt_chunk // tm
        vt_block = (1, 1, v_cols, tm)
        vt_map = lambda b, i: (b, i // per, 0, i % per)
    out_specs = [
        pl.BlockSpec((1, tm, d), lambda b, i: (b, i, 0)),
        pl.BlockSpec((1, n_q, tm, LANES), lambda b, i: (b, 0, i, 0)),
        pl.BlockSpec((1, n_k, tm, LANES), lambda b, i: (b, 0, i, 0)),
        pl.BlockSpec(vt_block, vt_map),
    ]
    out_shape = [
        jax.ShapeDtypeStruct((bx, lx, d), F32),
        jax.ShapeDtypeStruct((bx, n_q, lx, LANES), BF16),
        jax.ShapeDtypeStruct((bx, n_k, lx, LANES), BF16),
        jax.ShapeDtypeStruct((bx, lx // vt_chunk, v_cols, vt_chunk), BF16),
    ]
    return pl.pallas_call(
        functools.partial(_pre_kernel, n_q=n_q, n_k=n_k, v_cols=v_cols, rope=rope),
        grid=grid, in_specs=in_specs, out_specs=out_specs, out_shape=out_shape,
        compiler_params=pltpu.CompilerParams(
            dimension_semantics=("arbitrary", "arbitrary"),
            vmem_limit_bytes=VMEM_LIMIT),
        name=name,
    )(*args)


def _post_call(x, o, mod, wout, wi, wo, *, tm, name):
    bx, lx, d = x.shape
    per_batch_mod = mod.shape[0] > 1
    mod_map = (lambda b, i: (b, 0, 0)) if per_batch_mod else (lambda b, i: (0, 0, 0))
    return pl.pallas_call(
        _post_kernel,
        grid=(bx, lx // tm),
        in_specs=[
            pl.BlockSpec((1, tm, d), lambda b, i: (b, i, 0)),
            pl.BlockSpec((1, tm, o.shape[2]), lambda b, i: (b, i, 0)),
            pl.BlockSpec((1, N_MOD, d), mod_map),
            _resident(wout.shape), _resident(wi.shape), _resident(wo.shape),
        ],
        out_specs=pl.BlockSpec((1, tm, d), lambda b, i: (b, i, 0)),
        out_shape=jax.ShapeDtypeStruct((bx, lx, d), F32),
        compiler_params=pltpu.CompilerParams(
            dimension_semantics=("arbitrary", "arbitrary"),
            vmem_limit_bytes=VMEM_LIMIT),
        name=name,
    )(x, o, mod, wout, wi, wo)


def _split_components(q_tile):
    qf = q_tile.astype(F32)
    lane = lax.broadcasted_iota(jnp.int32, qf.shape, 1)
    lo = jnp.where(lane < HEAD_DIM, qf, 0.0).astype(BF16)
    hi = jnp.where(lane >= HEAD_DIM, qf, 0.0).astype(BF16)
    return jnp.concatenate([lo, hi], axis=0)


def _attn_a_kernel(*refs, n_lat, lam_init):
    if n_lat:
        (lam_ref, gain_ref, q_ref, k_ref, vt_ref, kc_ref, vtc_ref,
         o_ref, m_sc, l_sc, acc_sc) = refs
    else:
        lam_ref, gain_ref, q_ref, kc_ref, vtc_ref, o_ref, m_sc, l_sc, acc_sc = refs
    tq = q_ref.shape[2]
    qq = _split_components(q_ref[0, 0])
    m_sc[...] = jnp.full(m_sc.shape, -jnp.inf, F32)
    l_sc[...] = jnp.zeros(l_sc.shape, F32)
    acc_sc[...] = jnp.zeros(acc_sc.shape, F32)

    def update(kb, vtb):
        st = lax.dot_general(kb, qq, (((1,), (1,)), ((), ())), preferred_element_type=F32)
        m_old = m_sc[...]
        m_new = jnp.maximum(m_old, jnp.max(st, axis=0, keepdims=True))
        alpha = jnp.exp2(m_old - m_new)
        p = jnp.exp2(st - m_new)
        l_sc[...] = alpha * l_sc[...] + jnp.sum(p, axis=0, keepdims=True)
        acc_sc[...] = alpha * acc_sc[...] + jnp.dot(vtb, p.astype(BF16),
                                                    preferred_element_type=F32)
        m_sc[...] = m_new

    if n_lat:
        tkv = vt_ref.shape[3]

        def body(j, carry):
            kb = k_ref[0, 0, pl.ds(pl.multiple_of(j * tkv, tkv), tkv), :]
            update(kb, vt_ref[0, j])
            return carry

        lax.fori_loop(0, n_lat, body, 0)
    ckv = vtc_ref.shape[3]
    for c in range(vtc_ref.shape[1]):
        update(kc_ref[0, 0, c * ckv:(c + 1) * ckv, :], vtc_ref[0, c])

    inv = 1.0 / l_sc[...]
    acc = acc_sc[...]
    lv = lam_ref[...]
    lam = (jnp.exp(jnp.sum(lv[0:1] * lv[1:2], axis=-1, keepdims=True))
           - jnp.exp(jnp.sum(lv[2:3] * lv[3:4], axis=-1, keepdims=True)) + lam_init)
    ot = acc[:, :tq] * inv[:, :tq] - lam * (acc[:, tq:] * inv[:, tq:])
    on = ot * lax.rsqrt(jnp.mean(ot * ot, axis=0, keepdims=True) + EPS)
    o_ref[0] = (on.T * (gain_ref[...] * (1.0 - lam_init))).astype(BF16)


def _attn_a_call(lam_vec, gain, q, k, vt, kc, vtc, *, tq, lam_init, name):
    b, h, lq, _ = q.shape
    n_lat = 0 if k is None else vt.shape[1]
    in_specs = [
        pl.BlockSpec(lam_vec.shape, lambda b_, h_, i: (0, 0)),
        pl.BlockSpec(gain.shape, lambda b_, h_, i: (0, 0)),
        pl.BlockSpec((1, 1, tq, LANES), lambda b_, h_, i: (b_, h_, i, 0)),
    ]
    args = [lam_vec, gain, q]
    if n_lat:
        in_specs += [
            pl.BlockSpec((1, 1, k.shape[2], LANES), lambda b_, h_, i: (b_, h_, 0, 0)),
            pl.BlockSpec((1, vt.shape[1], LANES, vt.shape[3]), lambda b_, h_, i: (b_, 0, h_, 0)),
        ]
        args += [k, vt]
    in_specs += [
        pl.BlockSpec((1, 1, kc.shape[2], LANES), lambda b_, h_, i: (b_, h_, 0, 0)),
        pl.BlockSpec((1, vtc.shape[1], LANES, vtc.shape[3]), lambda b_, h_, i: (b_, 0, h_, 0)),
    ]
    args += [kc, vtc]
    return pl.pallas_call(
        functools.partial(_attn_a_kernel, n_lat=n_lat, lam_init=lam_init),
        grid=(b, h, lq // tq),
        in_specs=in_specs,
        out_specs=pl.BlockSpec((1, tq, LANES), lambda b_, h_, i: (b_, i, h_)),
        out_shape=jax.ShapeDtypeStruct((b, lq, h * LANES), BF16),
        scratch_shapes=[pltpu.VMEM((1, 2 * tq), F32), pltpu.VMEM((1, 2 * tq), F32),
                        pltpu.VMEM((LANES, 2 * tq), F32)],
        compiler_params=pltpu.CompilerParams(
            dimension_semantics=("arbitrary", "arbitrary", "arbitrary"),
            vmem_limit_bytes=VMEM_LIMIT),
        name=name,
    )(*args)


def _attn_b_kernel(sink_ref, q_ref, kp_ref, ko_ref, kn_ref, kc_ref,
                   vp_ref, vo_ref, vn_ref, vc_ref, o_ref):
    n = pl.program_id(1)
    nb = pl.num_programs(1)
    bb = WINDOW_BLOCK
    slabs = [q_ref[0, g].astype(F32) for g in range(B_GROUP)]
    lane = lax.broadcasted_iota(jnp.int32, slabs[0].shape, 1)
    lo = [jnp.where(lane < HEAD_DIM, s, 0.0).astype(BF16) for s in slabs]
    hi = [jnp.where(lane >= HEAD_DIM, s, 0.0).astype(BF16) for s in slabs]
    qq = jnp.concatenate(lo + hi, axis=0)
    kcat = jnp.concatenate([kp_ref[0, 0], ko_ref[0, 0], kn_ref[0, 0], kc_ref[0, 0]], axis=0)
    st = lax.dot_general(kcat, qq, (((1,), (1,)), ((), ())), preferred_element_type=F32)
    ncol = st.shape[1]

    key_j = lax.broadcasted_iota(jnp.int32, (bb, ncol), 0)
    qry_i = lax.broadcasted_iota(jnp.int32, (bb, ncol), 1) & (bb - 1)
    off_prev = jnp.where(n > 0, 0, bb)
    off_next = jnp.where(n < nb - 1, 0, bb)
    s_prev = jnp.where(key_j >= qry_i + off_prev, st[0:bb], NEG_INF)
    s_own = st[bb:2 * bb]
    s_next = jnp.where(key_j <= qry_i - off_next, st[2 * bb:3 * bb], NEG_INF)
    s_ctx = st[3 * bb:]
    sink = sink_ref[0]

    m = jnp.maximum(jnp.maximum(jnp.max(s_prev, axis=0, keepdims=True),
                                jnp.max(s_own, axis=0, keepdims=True)),
                    jnp.maximum(jnp.max(s_next, axis=0, keepdims=True),
                                jnp.max(s_ctx, axis=0, keepdims=True)))
    m = jnp.maximum(m, sink)
    parts = [jnp.exp2(s - m) for s in (s_prev, s_own, s_next, s_ctx)]
    l = jnp.exp2(sink - m)
    for p in parts:
        l = l + jnp.sum(p, axis=0, keepdims=True)
    pt = jnp.concatenate([p.astype(BF16) for p in parts], axis=0)
    vcat = jnp.concatenate([vp_ref[0, 0], vo_ref[0, 0], vn_ref[0, 0]]
                           + [vc_ref[0, c] for c in range(vc_ref.shape[1])], axis=1)
    inv = 1.0 / l
    half = ncol // 2
    for e in range(2):
        ot = jnp.dot(vcat[e * HEAD_DIM:(e + 1) * HEAD_DIM], pt[:, e * half:(e + 1) * half],
                     preferred_element_type=F32) * inv[:, e * half:(e + 1) * half]
        for gp in range(B_GROUP // 2):
            blk = jnp.concatenate([ot[:, (2 * gp) * bb:(2 * gp + 1) * bb],
                                   ot[:, (2 * gp + 1) * bb:(2 * gp + 2) * bb]], axis=0)
            c0 = (e * (B_GROUP // 2) + gp) * LANES
            o_ref[0, :, c0:c0 + LANES] = blk.T.astype(BF16)


def _attn_b_call(sink_rows, q, k, vt, kc, vtc, *, name):
    b, n_slab, l, _ = q.shape
    n_pair = k.shape[1]
    nb = l // WINDOW_BLOCK
    bb = WINDOW_BLOCK
    prev = lambda n: jnp.maximum(n - 1, 0)
    nxt = lambda n: jnp.minimum(n + 1, nb - 1)
    kspec = lambda f: pl.BlockSpec((1, 1, bb, LANES), lambda b_, n, p: (b_, p, f(n), 0))
    vspec = lambda f: pl.BlockSpec((1, 1, LANES, bb), lambda b_, n, p: (b_, f(n), p, 0))
    ident = lambda n: n
    return pl.pallas_call(
        _attn_b_kernel,
        grid=(b, nb, n_pair),
        in_specs=[
            pl.BlockSpec((1, 1, sink_rows.shape[2]), lambda b_, n, p: (p, 0, 0)),
            pl.BlockSpec((1, B_GROUP, bb, LANES), lambda b_, n, p: (b_, p, n, 0)),
            kspec(prev), kspec(ident), kspec(nxt),
            pl.BlockSpec((1, 1, kc.shape[2], LANES), lambda b_, n, p: (b_, p, 0, 0)),
            vspec(prev), vspec(ident), vspec(nxt),
            pl.BlockSpec((1, vtc.shape[1], LANES, vtc.shape[3]), lambda b_, n, p: (b_, 0, p, 0)),
        ],
        out_specs=pl.BlockSpec((1, bb, B_GROUP * LANES), lambda b_, n, p: (b_, n, p)),
        out_shape=jax.ShapeDtypeStruct((b, l, n_slab * LANES), BF16),
        compiler_params=pltpu.CompilerParams(
            dimension_semantics=("arbitrary", "arbitrary", "arbitrary"),
            vmem_limit_bytes=VMEM_LIMIT),
        name=name,
    )(sink_rows, q, k, k, k, kc, vt, vt, vt, vtc)


def _rope_tables(n_tok):
    rows = n_tok // GRID_W
    nf = HEAD_DIM // 4
    inv = ROPE_BASE ** (-jnp.arange(nf, dtype=F32) / nf)
    row = jnp.broadcast_to(jnp.arange(rows, dtype=F32)[:, None], (rows, GRID_W)).reshape(-1)
    col = jnp.broadcast_to(jnp.arange(GRID_W, dtype=F32)[None, :], (rows, GRID_W)).reshape(-1)
    ang = jnp.stack([row[:, None] * inv, col[:, None] * inv], axis=1)
    ang = jnp.stack([ang, ang], axis=2).reshape(n_tok, HEAD_DIM)
    cos = jnp.tile(jnp.cos(ang), (1, LANES // HEAD_DIM))
    sin = jnp.tile(jnp.sin(ang), (1, LANES // HEAD_DIM))
    first_half = (jnp.arange(LANES) % (2 * nf)) < nf
    return cos, jnp.where(first_half, -sin, 0.0), jnp.where(first_half, 0.0, sin)


def _group_sum_matrix():
    idx = jnp.arange(MXU_COLS) // HEAD_DIM
    return (idx[:, None] == idx[None, :]).astype(BF16)


def _pair_q_heads(w_qkv):
    order = []
    for p in range(B_KV_HEADS // 2):
        for g in range(B_GROUP):
            order += [(2 * p) * B_GROUP + g, (2 * p + 1) * B_GROUP + g]
    d = w_qkv.shape[0]
    wq = w_qkv[:, :B_Q_HEADS * HEAD_DIM].reshape(d, B_Q_HEADS, HEAD_DIM)
    wq = wq[:, jnp.array(order)].reshape(d, B_Q_HEADS * HEAD_DIM)
    return jnp.concatenate([wq, w_qkv[:, B_Q_HEADS * HEAD_DIM:]], axis=1)


def kernel(x, c, ctx, c_ctx, ada_w, ada_b, ffn_pre_wi, ffn_pre_wo, ffn_post_wi, ffn_post_wo,
           a_w_qkv, a_w_o, a_q_gain, a_k_gain, a_lambda, a_subln_gain,
           b_w_qkv, b_w_o, b_q_gain, b_k_gain, b_sink):
    bsz, seq, d = x.shape
    n_ctx = ctx.shape[1]
    depth = ada_w.shape[0]
    assert depth == 2 and seq % WINDOW_BLOCK == 0

    rows = 8
    cs = jnp.zeros((rows, d), F32).at[:bsz].set(c).at[bsz].set(c_ctx)
    mods = _ada(cs, ada_w, ada_b)
    mod_l = [mods[i, :bsz].reshape(bsz, N_MOD, d) for i in range(depth)]
    mod_c = [mods[i, bsz:bsz + 1].reshape(1, N_MOD, d) for i in range(depth)]

    rope_tabs = _rope_tables(seq)
    bd = _group_sum_matrix()
    qk_scale = HEAD_DIM ** -0.5 * LOG2E
    tile2 = lambda g: jnp.tile(g.astype(F32), LANES // HEAD_DIM).reshape(1, LANES)

    pre_wi = [w.astype(BF16) for w in ffn_pre_wi]
    pre_wo = [w.astype(BF16) for w in ffn_pre_wo]
    post_wi = [w.astype(BF16) for w in ffn_post_wi]
    post_wo = [w.astype(BF16) for w in ffn_post_wo]

    tm = 256
    tkv = 256
    tq = 256

    lam_init = 0.8 - 0.6 * math.exp(-0.3 * 0)
    a_cfg = dict(n_q=A_HEADS, n_k=A_HEADS, v_cols=A_HEADS * LANES)
    wqkv_a = a_w_qkv[0].astype(BF16)
    gq_a, gk_a = tile2(a_q_gain[0]) * qk_scale, tile2(a_k_gain[0])
    x1, q, k, vt = _pre_call(x, mod_l[0], pre_wi[0], pre_wo[0], wqkv_a, bd, gq_a, gk_a,
                             rope_tabs, tm=tm, vt_chunk=tkv, name="pre0_lat", **a_cfg)
    xc1, qc, kc, vtc = _pre_call(ctx, mod_c[0], pre_wi[0], pre_wo[0], wqkv_a, bd, gq_a, gk_a,
                                 None, tm=n_ctx, vt_chunk=n_ctx, name="pre0_ctx", **a_cfg)
    sub_gain = a_subln_gain[0].astype(F32).reshape(1, LANES)
    lam_vec = a_lambda[0].astype(F32)
    o = _attn_a_call(lam_vec, sub_gain, q, k, vt, kc, vtc, tq=tq, lam_init=lam_init,
                     name="attn_a_lat")
    oc = _attn_a_call(lam_vec, sub_gain, qc, None, None, kc, vtc, tq=n_ctx, lam_init=lam_init,
                      name="attn_a_ctx")
    wout_a = a_w_o[0].astype(BF16)
    x2 = _post_call(x1, o, mod_l[0], wout_a, post_wi[0], post_wo[0], tm=tm, name="post0_lat")
    xc2 = _post_call(xc1, oc, mod_c[0], wout_a, post_wi[0], post_wo[0], tm=n_ctx,
                     name="post0_ctx")

    b_cfg = dict(n_q=B_Q_HEADS // 2, n_k=B_KV_HEADS // 2, v_cols=B_KV_HEADS * HEAD_DIM)
    wqkv_b = _pair_q_heads(b_w_qkv[0]).astype(BF16)
    gq_b, gk_b = tile2(b_q_gain[0]) * qk_scale, tile2(b_k_gain[0])
    x3, q, k, vt = _pre_call(x2, mod_l[1], pre_wi[1], pre_wo[1], wqkv_b, bd, gq_b, gk_b,
                             rope_tabs, tm=tm, vt_chunk=WINDOW_BLOCK, name="pre1_lat", **b_cfg)
    _, _, kc, vtc = _pre_call(xc2, mod_c[1], pre_wi[1], pre_wo[1], wqkv_b, bd, gq_b, gk_b,
                              None, tm=n_ctx, vt_chunk=WINDOW_BLOCK, name="pre1_ctx", **b_cfg)
    sink = (b_sink[0].astype(F32) * LOG2E).reshape(B_KV_HEADS // 2, 2 * B_GROUP, 1)
    sink_rows = jnp.broadcast_to(sink, (B_KV_HEADS // 2, 2 * B_GROUP, WINDOW_BLOCK))
    sink_rows = sink_rows.reshape(B_KV_HEADS // 2, 1, 2 * B_GROUP * WINDOW_BLOCK)
    o = _attn_b_call(sink_rows, q, k, vt, kc, vtc, name="attn_b")
    x4 = _post_call(x3, o, mod_l[1], b_w_o[0].astype(BF16), post_wi[1], post_wo[1], tm=tm,
                    name="post1_lat")
    return x4
```

```python
import functools
import math

import jax
import jax.numpy as jnp
from jax import lax
from jax.experimental import pallas as pl
from jax.experimental.pallas import tpu as pltpu

F32 = jnp.float32
BF16 = jnp.bfloat16

LANES = 128
MXU_COLS = 256
HEAD_DIM = 64
N_MOD = 9
EPS = 1e-6
ROPE_BASE = 10000.0
GRID_W = 64
WINDOW_BLOCK = 128
NEG_INF = -1e30
LOG2E = math.log2(math.e)
VMEM_LIMIT = 56 * 1024 * 1024

A_HEADS = 8
B_Q_HEADS = 16
B_KV_HEADS = 4
B_GROUP = B_Q_HEADS // B_KV_HEADS


def _resident(shape):
    nd = len(shape)
    return pl.BlockSpec(shape, lambda *_: (0,) * nd, pipeline_mode=pl.Buffered(1))


def _ada_kernel(cs_ref, w_ref, b_ref, out_ref):
    s = cs_ref[...]
    s = s * jax.nn.sigmoid(s)
    out_ref[0] = jnp.dot(s.astype(BF16), w_ref[0].astype(BF16),
                         preferred_element_type=F32) + b_ref[0]


def _ada(cs, ada_w, ada_b):
    depth, d, n = ada_w.shape
    rows = cs.shape[0]
    tn = n // 8
    return pl.pallas_call(
        _ada_kernel,
        grid=(depth, n // tn),
        in_specs=[
            pl.BlockSpec((rows, d), lambda i, j: (0, 0)),
            pl.BlockSpec((1, d, tn), lambda i, j: (i, 0, j)),
            pl.BlockSpec((1, 1, tn), lambda i, j: (i, 0, j)),
        ],
        out_specs=pl.BlockSpec((1, rows, tn), lambda i, j: (i, 0, j)),
        out_shape=jax.ShapeDtypeStruct((depth, rows, n), F32),
        compiler_params=pltpu.CompilerParams(
            dimension_semantics=("arbitrary", "arbitrary"),
            vmem_limit_bytes=VMEM_LIMIT),
        name="ada_mod",
    )(cs, ada_w, ada_b.reshape(depth, 1, n))


def _rms(x):
    return x * lax.rsqrt(jnp.mean(x * x, axis=-1, keepdims=True) + EPS)


def _modulated(x, shift, scale):
    return (_rms(x) * (1.0 + scale) + shift).astype(BF16)


def _ffn_half_step(x, shift, scale, gate, wi_ref, wo_ref):
    d_ff = wo_ref.shape[0]
    h = _modulated(x, shift, scale)
    gu = jnp.dot(h, wi_ref[...], preferred_element_type=F32)
    g = gu[:, :d_ff]
    u = gu[:, d_ff:]
    act = (g * jax.nn.sigmoid(g) * u).astype(BF16)
    ff = jnp.dot(act, wo_ref[...], preferred_element_type=F32)
    return x + (0.5 * gate) * ff


def _pre_kernel(*refs, n_q, n_k, v_cols, rope):
    if rope:
        (x_ref, mod_ref, wi_ref, wo_ref, wqkv_ref, bd_ref, gq_ref, gk_ref,
         cos_ref, sa_ref, sb_ref, x_out, q_out, k_out, vt_out) = refs
    else:
        (x_ref, mod_ref, wi_ref, wo_ref, wqkv_ref, bd_ref, gq_ref, gk_ref,
         x_out, q_out, k_out, vt_out) = refs
    mod = mod_ref[0]
    x1 = _ffn_half_step(x_ref[0], mod[0:1], mod[1:2], mod[2:3], wi_ref, wo_ref)
    x_out[0] = x1
    h = _modulated(x1, mod[3:4], mod[4:5])
    qkv = jnp.dot(h, wqkv_ref[...], preferred_element_type=F32)

    bd = bd_ref[...]
    if rope:
        cos, sa, sb = cos_ref[...], sa_ref[...], sb_ref[...]

    def norm_rope_store(col0, n_slabs, gain, out):
        for s in range(0, n_slabs, 2):
            y = qkv[:, col0 + s * LANES: col0 + (s + 2) * LANES]
            ss = jnp.dot((y * y).astype(BF16), bd, preferred_element_type=F32)
            y = y * lax.rsqrt(ss * (1.0 / HEAD_DIM) + EPS)
            for t in range(2):
                z = y[:, t * LANES:(t + 1) * LANES] * gain
                if rope:
                    z = (z * cos + pltpu.roll(z, LANES - 16, 1) * sa
                         + pltpu.roll(z, 16, 1) * sb)
                out[0, s + t] = z.astype(BF16)

    norm_rope_store(0, n_q, gq_ref[...], q_out)
    norm_rope_store(n_q * LANES, n_k, gk_ref[...], k_out)

    v0 = (n_q + n_k) * LANES
    vt = qkv[:, v0:v0 + v_cols].T.astype(BF16)
    chunk = vt_out.shape[3]
    for c in range(vt_out.shape[1]):
        vt_out[0, c] = vt[:, c * chunk:(c + 1) * chunk]


def _post_kernel(x_ref, o_ref, mod_ref, wout_ref, wi_ref, wo_ref, x_out):
    mod = mod_ref[0]
    attn = jnp.dot(o_ref[0], wout_ref[...], preferred_element_type=F32)
    x2 = x_ref[0] + mod[5:6] * attn
    x_out[0] = _ffn_half_step(x2, mod[6:7], mod[7:8], mod[8:9], wi_ref, wo_ref)


def _pre_call(x, mod, wi, wo, wqkv, bd, gq, gk, rope_tabs, *, n_q, n_k, v_cols,
              tm, vt_chunk, name):
    bx, lx, d = x.shape
    per_batch_mod = mod.shape[0] > 1
    rope = rope_tabs is not None
    grid = (bx, lx // tm)
    mod_map = (lambda b, i: (b, 0, 0)) if per_batch_mod else (lambda b, i: (0, 0, 0))
    in_specs = [
        pl.BlockSpec((1, tm, d), lambda b, i: (b, i, 0)),
        pl.BlockSpec((1, N_MOD, d), mod_map),
        _resident(wi.shape), _resident(wo.shape), _resident(wqkv.shape),
        _resident(bd.shape), _resident(gq.shape), _resident(gk.shape),
    ]
    args = [x, mod, wi, wo, wqkv, bd, gq, gk]
    if rope:
        in_specs += [pl.BlockSpec((tm, LANES), lambda b, i: (i, 0))] * 3
        args += list(rope_tabs)
    if tm >= vt_chunk:
        vt_block = (1, tm // vt_chunk, v_cols, vt_chunk)
        vt_map = lambda b, i: (b, i, 0, 0)
    else:
        per = vt_chunk // tm
        vt_block = (1, 1, v_cols, tm)
        vt_map = lambda b, i: (b, i // per, 0, i % per)
    out_specs = [
        pl.BlockSpec((1, tm, d), lambda b, i: (b, i, 0)),
        pl.BlockSpec((1, n_q, tm, LANES), lambda b, i: (b, 0, i, 0)),
        pl.BlockSpec((1, n_k, tm, LANES), lambda b, i: (b, 0, i, 0)),
        pl.BlockSpec(vt_block, vt_map),
    ]
    out_shape = [
        jax.ShapeDtypeStruct((bx, lx, d), F32),
        jax.ShapeDtypeStruct((bx, n_q, lx, LANES), BF16),
        jax.ShapeDtypeStruct((bx, n_k, lx, LANES), BF16),
        jax.ShapeDtypeStruct((bx, lx // vt_chunk, v_cols, vt_chunk), BF16),
    ]
    return pl.pallas_call(
        functools.partial(_pre_kernel, n_q=n_q, n_k=n_k, v_cols=v_cols, rope=rope),
        grid=grid, in_specs=in_specs, out_specs=out_specs, out_shape=out_shape,
        compiler_params=pltpu.CompilerParams(
            dimension_semantics=("arbitrary", "arbitrary"),
            vmem_limit_bytes=VMEM_LIMIT),
        name=name,
    )(*args)


def _post_call(x, o, mod, wout, wi, wo, *, tm, name):
    bx, lx, d = x.shape
    per_batch_mod = mod.shape[0] > 1
    mod_map = (lambda b, i: (b, 0, 0)) if per_batch_mod else (lambda b, i: (0, 0, 0))
    return pl.pallas_call(
        _post_kernel,
        grid=(bx, lx // tm),
        in_specs=[
            pl.BlockSpec((1, tm, d), lambda b, i: (b, i, 0)),
            pl.BlockSpec((1, tm, o.shape[2]), lambda b, i: (b, i, 0)),
            pl.BlockSpec((1, N_MOD, d), mod_map),
            _resident(wout.shape), _resident(wi.shape), _resident(wo.shape),
        ],
        out_specs=pl.BlockSpec((1, tm, d), lambda b, i: (b, i, 0)),
        out_shape=jax.ShapeDtypeStruct((bx, lx, d), F32),
        compiler_params=pltpu.CompilerParams(
            dimension_semantics=("arbitrary", "arbitrary"),
            vmem_limit_bytes=VMEM_LIMIT),
        name=name,
    )(x, o, mod, wout, wi, wo)


def _split_components(q_tile):
    qf = q_tile.astype(F32)
    lane = lax.broadcasted_iota(jnp.int32, qf.shape, 1)
    lo = jnp.where(lane < HEAD_DIM, qf, 0.0).astype(BF16)
    hi = jnp.where(lane >= HEAD_DIM, qf, 0.0).astype(BF16)
    return jnp.concatenate([lo, hi], axis=0)


def _attn_a_kernel(*refs, n_lat, lam_init):
    if n_lat:
        lam_ref, gain_ref, q_ref, k_ref, vt_ref, kc_ref, vtc_ref, o_ref = refs
    else:
        lam_ref, gain_ref, q_ref, kc_ref, vtc_ref, o_ref = refs
    tq = q_ref.shape[2]
    qq = _split_components(q_ref[0, 0])

    def scores(kb):
        return lax.dot_general(kb, qq, (((1,), (1,)), ((), ())), preferred_element_type=F32)

    def update(state, st, vtb):
        m_blk = jnp.max(st, axis=0, keepdims=True)
        if state is None:
            m_new = m_blk
        else:
            m_old, l_old, acc_old = state
            m_new = jnp.maximum(m_old, m_blk)
            alpha = jnp.exp2(m_old - m_new)
        p = jnp.exp2(st - m_new)
        l_new = jnp.sum(p, axis=0, keepdims=True)
        acc_new = jnp.dot(vtb, p.astype(BF16), preferred_element_type=F32)
        if state is not None:
            l_new = alpha * l_old + l_new
            acc_new = alpha * acc_old + acc_new
        return m_new, l_new, acc_new

    blocks = []
    if n_lat:
        tkv = vt_ref.shape[3]
        for j in range(n_lat):
            blocks.append((lambda j=j: k_ref[0, 0, j * tkv:(j + 1) * tkv, :],
                           lambda j=j: vt_ref[0, j]))
    ckv = vtc_ref.shape[3]
    for c in range(vtc_ref.shape[1]):
        blocks.append((lambda c=c: kc_ref[0, 0, c * ckv:(c + 1) * ckv, :],
                       lambda c=c: vtc_ref[0, c]))
    state = None
    st = scores(blocks[0][0]())
    for j, (_, load_vt) in enumerate(blocks):
        st_next = scores(blocks[j + 1][0]()) if j + 1 < len(blocks) else None
        state = update(state, st, load_vt())
        st = st_next
    _, l_fin, acc = state

    inv = 1.0 / l_fin
    lv = lam_ref[...]
    lam = (jnp.exp(jnp.sum(lv[0:1] * lv[1:2], axis=-1, keepdims=True))
           - jnp.exp(jnp.sum(lv[2:3] * lv[3:4], axis=-1, keepdims=True)) + lam_init)
    ot = acc[:, :tq] * inv[:, :tq] - lam * (acc[:, tq:] * inv[:, tq:])
    on = ot * lax.rsqrt(jnp.mean(ot * ot, axis=0, keepdims=True) + EPS)
    o_ref[0] = (on.T * (gain_ref[...] * (1.0 - lam_init))).astype(BF16)


def _attn_a_call(lam_vec, gain, q, k, vt, kc, vtc, *, tq, lam_init, name):
    b, h, lq, _ = q.shape
    n_lat = 0 if k is None else vt.shape[1]
    in_specs = [
        pl.BlockSpec(lam_vec.shape, lambda b_, h_, i: (0, 0)),
        pl.BlockSpec(gain.shape, lambda b_, h_, i: (0, 0)),
        pl.BlockSpec((1, 1, tq, LANES), lambda b_, h_, i: (b_, h_, i, 0)),
    ]
    args = [lam_vec, gain, q]
    if n_lat:
        in_specs += [
            pl.BlockSpec((1, 1, k.shape[2], LANES), lambda b_, h_, i: (b_, h_, 0, 0)),
            pl.BlockSpec((1, vt.shape[1], LANES, vt.shape[3]), lambda b_, h_, i: (b_, 0, h_, 0)),
        ]
        args += [k, vt]
    in_specs += [
        pl.BlockSpec((1, 1, kc.shape[2], LANES), lambda b_, h_, i: (b_, h_, 0, 0)),
        pl.BlockSpec((1, vtc.shape[1], LANES, vtc.shape[3]), lambda b_, h_, i: (b_, 0, h_, 0)),
    ]
    args += [kc, vtc]
    return pl.pallas_call(
        functools.partial(_attn_a_kernel, n_lat=n_lat, lam_init=lam_init),
        grid=(b, h, lq // tq),
        in_specs=in_specs,
        out_specs=pl.BlockSpec((1, tq, LANES), lambda b_, h_, i: (b_, i, h_)),
        out_shape=jax.ShapeDtypeStruct((b, lq, h * LANES), BF16),
        compiler_params=pltpu.CompilerParams(
            dimension_semantics=("arbitrary", "arbitrary", "arbitrary"),
            vmem_limit_bytes=VMEM_LIMIT),
        name=name,
    )(*args)


def _attn_b_kernel(sink_ref, q_ref, kp_ref, ko_ref, kn_ref, kc_ref,
                   vp_ref, vo_ref, vn_ref, vc_ref, o_ref):
    n = pl.program_id(1)
    nb = pl.num_programs(1)
    bb = WINDOW_BLOCK
    slabs = [q_ref[0, g].astype(F32) for g in range(B_GROUP)]
    lane = lax.broadcasted_iota(jnp.int32, slabs[0].shape, 1)
    lo = [jnp.where(lane < HEAD_DIM, s, 0.0).astype(BF16) for s in slabs]
    hi = [jnp.where(lane >= HEAD_DIM, s, 0.0).astype(BF16) for s in slabs]
    qq = jnp.concatenate(lo + hi, axis=0)
    kcat = jnp.concatenate([kp_ref[0, 0], ko_ref[0, 0], kn_ref[0, 0], kc_ref[0, 0]], axis=0)
    st = lax.dot_general(kcat, qq, (((1,), (1,)), ((), ())), preferred_element_type=F32)
    ncol = st.shape[1]

    key_j = lax.broadcasted_iota(jnp.int32, (bb, ncol), 0)
    qry_i = lax.broadcasted_iota(jnp.int32, (bb, ncol), 1) & (bb - 1)
    off_prev = jnp.where(n > 0, 0, bb)
    off_next = jnp.where(n < nb - 1, 0, bb)
    s_prev = jnp.where(key_j >= qry_i + off_prev, st[0:bb], NEG_INF)
    s_own = st[bb:2 * bb]
    s_next = jnp.where(key_j <= qry_i - off_next, st[2 * bb:3 * bb], NEG_INF)
    s_ctx = st[3 * bb:]
    sink = sink_ref[0]

    m = jnp.maximum(jnp.maximum(jnp.max(s_prev, axis=0, keepdims=True),
                                jnp.max(s_own, axis=0, keepdims=True)),
                    jnp.maximum(jnp.max(s_next, axis=0, keepdims=True),
                                jnp.max(s_ctx, axis=0, keepdims=True)))
    m = jnp.maximum(m, sink)
    parts = [jnp.exp2(s - m) for s in (s_prev, s_own, s_next, s_ctx)]
    l = jnp.exp2(sink - m)
    for p in parts:
        l = l + jnp.sum(p, axis=0, keepdims=True)
    pt = jnp.concatenate([p.astype(BF16) for p in parts], axis=0)
    vcat = jnp.concatenate([vp_ref[0, 0], vo_ref[0, 0], vn_ref[0, 0]]
                           + [vc_ref[0, c] for c in range(vc_ref.shape[1])], axis=1)
    inv = 1.0 / l
    half = ncol // 2
    for e in range(2):
        ot = jnp.dot(vcat[e * HEAD_DIM:(e + 1) * HEAD_DIM], pt[:, e * half:(e + 1) * half],
                     preferred_element_type=F32) * inv[:, e * half:(e + 1) * half]
        for gp in range(B_GROUP // 2):
            blk = jnp.concatenate([ot[:, (2 * gp) * bb:(2 * gp + 1) * bb],
                                   ot[:, (2 * gp + 1) * bb:(2 * gp + 2) * bb]], axis=0)
            c0 = (e * (B_GROUP // 2) + gp) * LANES
            o_ref[0, :, c0:c0 + LANES] = blk.T.astype(BF16)


def _attn_b_call(sink_rows, q, k, vt, kc, vtc, *, name):
    b, n_slab, l, _ = q.shape
    n_pair = k.shape[1]
    nb = l // WINDOW_BLOCK
    bb = WINDOW_BLOCK
    prev = lambda n: jnp.maximum(n - 1, 0)
    nxt = lambda n: jnp.minimum(n + 1, nb - 1)
    kspec = lambda f: pl.BlockSpec((1, 1, bb, LANES), lambda b_, n, p: (b_, p, f(n), 0))
    vspec = lambda f: pl.BlockSpec((1, 1, LANES, bb), lambda b_, n, p: (b_, f(n), p, 0))
    ident = lambda n: n
    return pl.pallas_call(
        _attn_b_kernel,
        grid=(b, nb, n_pair),
        in_specs=[
            pl.BlockSpec((1, 1, sink_rows.shape[2]), lambda b_, n, p: (p, 0, 0)),
            pl.BlockSpec((1, B_GROUP, bb, LANES), lambda b_, n, p: (b_, p, n, 0)),
            kspec(prev), kspec(ident), kspec(nxt),
            pl.BlockSpec((1, 1, kc.shape[2], LANES), lambda b_, n, p: (b_, p, 0, 0)),
            vspec(prev), vspec(ident), vspec(nxt),
            pl.BlockSpec((1, vtc.shape[1], LANES, vtc.shape[3]), lambda b_, n, p: (b_, 0, p, 0)),
        ],
        out_specs=pl.BlockSpec((1, bb, B_GROUP * LANES), lambda b_, n, p: (b_, n, p)),
        out_shape=jax.ShapeDtypeStruct((b, l, n_slab * LANES), BF16),
        compiler_params=pltpu.CompilerParams(
            dimension_semantics=("arbitrary", "arbitrary", "arbitrary"),
            vmem_limit_bytes=VMEM_LIMIT),
        name=name,
    )(sink_rows, q, k, k, k, kc, vt, vt, vt, vtc)


def _rope_tables(n_tok):
    rows = n_tok // GRID_W
    nf = HEAD_DIM // 4
    inv = ROPE_BASE ** (-jnp.arange(nf, dtype=F32) / nf)
    row = jnp.broadcast_to(jnp.arange(rows, dtype=F32)[:, None], (rows, GRID_W)).reshape(-1)
    col = jnp.broadcast_to(jnp.arange(GRID_W, dtype=F32)[None, :], (rows, GRID_W)).reshape(-1)
    ang = jnp.stack([row[:, None] * inv, col[:, None] * inv], axis=1)
    ang = jnp.stack([ang, ang], axis=2).reshape(n_tok, HEAD_DIM)
    cos = jnp.tile(jnp.cos(ang), (1, LANES // HEAD_DIM))
    sin = jnp.tile(jnp.sin(ang), (1, LANES // HEAD_DIM))
    first_half = (jnp.arange(LANES) % (2 * nf)) < nf
    return cos, jnp.where(first_half, -sin, 0.0), jnp.where(first_half, 0.0, sin)


def _group_sum_matrix():
    idx = jnp.arange(MXU_COLS) // HEAD_DIM
    return (idx[:, None] == idx[None, :]).astype(BF16)


def _pair_q_heads(w_qkv):
    order = []
    for p in range(B_KV_HEADS // 2):
        for g in range(B_GROUP):
            order += [(2 * p) * B_GROUP + g, (2 * p + 1) * B_GROUP + g]
    d = w_qkv.shape[0]
    wq = w_qkv[:, :B_Q_HEADS * HEAD_DIM].reshape(d, B_Q_HEADS, HEAD_DIM)
    wq = wq[:, jnp.array(order)].reshape(d, B_Q_HEADS * HEAD_DIM)
    return jnp.concatenate([wq, w_qkv[:, B_Q_HEADS * HEAD_DIM:]], axis=1)


def kernel(x, c, ctx, c_ctx, ada_w, ada_b, ffn_pre_wi, ffn_pre_wo, ffn_post_wi, ffn_post_wo,
           a_w_qkv, a_w_o, a_q_gain, a_k_gain, a_lambda, a_subln_gain,
           b_w_qkv, b_w_o, b_q_gain, b_k_gain, b_sink):
    bsz, seq, d = x.shape
    n_ctx = ctx.shape[1]
    depth = ada_w.shape[0]
    assert depth == 2 and seq % WINDOW_BLOCK == 0

    rows = 8
    cs = jnp.zeros((rows, d), F32).at[:bsz].set(c).at[bsz].set(c_ctx)
    mods = _ada(cs, ada_w, ada_b)
    mod_l = [mods[i, :bsz].reshape(bsz, N_MOD, d) for i in range(depth)]
    mod_c = [mods[i, bsz:bsz + 1].reshape(1, N_MOD, d) for i in range(depth)]

    rope_tabs = _rope_tables(seq)
    bd = _group_sum_matrix()
    qk_scale = HEAD_DIM ** -0.5 * LOG2E
    tile2 = lambda g: jnp.tile(g.astype(F32), LANES // HEAD_DIM).reshape(1, LANES)

    pre_wi = [w.astype(BF16) for w in ffn_pre_wi]
    pre_wo = [w.astype(BF16) for w in ffn_pre_wo]
    post_wi = [w.astype(BF16) for w in ffn_post_wi]
    post_wo = [w.astype(BF16) for w in ffn_post_wo]

    tm = 256
    tkv = 1024
    tq = 256

    lam_init = 0.8 - 0.6 * math.exp(-0.3 * 0)
    a_cfg = dict(n_q=A_HEADS, n_k=A_HEADS, v_cols=A_HEADS * LANES)
    wqkv_a = a_w_qkv[0].astype(BF16)
    gq_a, gk_a = tile2(a_q_gain[0]) * qk_scale, tile2(a_k_gain[0])
    x1, q, k, vt = _pre_call(x, mod_l[0], pre_wi[0], pre_wo[0], wqkv_a, bd, gq_a, gk_a,
                             rope_tabs, tm=tm, vt_chunk=tkv, name="pre0_lat", **a_cfg)
    xc1, qc, kc, vtc = _pre_call(ctx, mod_c[0], pre_wi[0], pre_wo[0], wqkv_a, bd, gq_a, gk_a,
                                 None, tm=n_ctx, vt_chunk=n_ctx, name="pre0_ctx", **a_cfg)
    sub_gain = a_subln_gain[0].astype(F32).reshape(1, LANES)
    lam_vec = a_lambda[0].astype(F32)
    o = _attn_a_call(lam_vec, sub_gain, q, k, vt, kc, vtc, tq=tq, lam_init=lam_init,
                     name="attn_a_lat")
    oc = _attn_a_call(lam_vec, sub_gain, qc, None, None, kc, vtc, tq=n_ctx, lam_init=lam_init,
                      name="attn_a_ctx")
    wout_a = a_w_o[0].astype(BF16)
    x2 = _post_call(x1, o, mod_l[0], wout_a, post_wi[0], post_wo[0], tm=tm, name="post0_lat")
    xc2 = _post_call(xc1, oc, mod_c[0], wout_a, post_wi[0], post_wo[0], tm=n_ctx,
                     name="post0_ctx")

    b_cfg = dict(n_q=B_Q_HEADS // 2, n_k=B_KV_HEADS // 2, v_cols=B_KV_HEADS * HEAD_DIM)
    wqkv_b = _pair_q_heads(b_w_qkv[0]).astype(BF16)
    gq_b, gk_b = tile2(b_q_gain[0]) * qk_scale, tile2(b_k_gain[0])
    x3, q, k, vt = _pre_call(x2, mod_l[1], pre_wi[1], pre_wo[1], wqkv_b, bd, gq_b, gk_b,
                             rope_tabs, tm=tm, vt_chunk=WINDOW_BLOCK, name="pre1_lat", **b_cfg)
    _, _, kc, vtc = _pre_call(xc2, mod_c[1], pre_wi[1], pre_wo[1], wqkv_b, bd, gq_b, gk_b,
                              None, tm=n_ctx, vt_chunk=WINDOW_BLOCK, name="pre1_ctx", **b_cfg)
    sink = (b_sink[0].astype(F32) * LOG2E).reshape(B_KV_HEADS // 2, 2 * B_GROUP, 1)
    sink_rows = jnp.broadcast_to(sink, (B_KV_HEADS // 2, 2 * B_GROUP, WINDOW_BLOCK))
    sink_rows = sink_rows.reshape(B_KV_HEADS // 2, 1, 2 * B_GROUP * WINDOW_BLOCK)
    o = _attn_b_call(sink_rows, q, k, vt, kc, vtc, name="attn_b")
    x4 = _post_call(x3, o, mod_l[1], b_w_o[0].astype(BF16), post_wi[1], post_wo[1], tm=tm,
                    name="post1_lat")
    return x4
```

```python
import functools
import math

import jax
import jax.numpy as jnp
from jax import lax
from jax.experimental import pallas as pl
from jax.experimental.pallas import tpu as pltpu

F32 = jnp.float32
BF16 = jnp.bfloat16

LANES = 128
MXU_COLS = 256
HEAD_DIM = 64
N_MOD = 9
EPS = 1e-6
ROPE_BASE = 10000.0
GRID_W = 64
WINDOW_BLOCK = 128
NEG_INF = -1e30
LOG2E = math.log2(math.e)
UNSHIFTED_SCORE_LIMIT = 96.0
VMEM_LIMIT = 56 * 1024 * 1024

A_HEADS = 8
B_Q_HEADS = 16
B_KV_HEADS = 4
B_GROUP = B_Q_HEADS // B_KV_HEADS


def _resident(shape):
    nd = len(shape)
    return pl.BlockSpec(shape, lambda *_: (0,) * nd, pipeline_mode=pl.Buffered(1))


def _ada_kernel(cs_ref, w_ref, b_ref, out_ref):
    s = cs_ref[...]
    s = s * jax.nn.sigmoid(s)
    out_ref[0] = jnp.dot(s.astype(BF16), w_ref[0].astype(BF16),
                         preferred_element_type=F32) + b_ref[0]


def _ada(cs, ada_w, ada_b):
    depth, d, n = ada_w.shape
    rows = cs.shape[0]
    tn = n // 8
    return pl.pallas_call(
        _ada_kernel,
        grid=(depth, n // tn),
        in_specs=[
            pl.BlockSpec((rows, d), lambda i, j: (0, 0)),
            pl.BlockSpec((1, d, tn), lambda i, j: (i, 0, j)),
            pl.BlockSpec((1, 1, tn), lambda i, j: (i, 0, j)),
        ],
        out_specs=pl.BlockSpec((1, rows, tn), lambda i, j: (i, 0, j)),
        out_shape=jax.ShapeDtypeStruct((depth, rows, n), F32),
        compiler_params=pltpu.CompilerParams(
            dimension_semantics=("arbitrary", "arbitrary"),
            vmem_limit_bytes=VMEM_LIMIT),
        name="ada_mod",
    )(cs, ada_w, ada_b.reshape(depth, 1, n))


def _rms(x):
    return x * lax.rsqrt(jnp.mean(x * x, axis=-1, keepdims=True) + EPS)


def _modulated(x, shift, scale):
    return (_rms(x) * (1.0 + scale) + shift).astype(BF16)


def _ffn_half_step(x, shift, scale, gate, wi_ref, wo_ref):
    d_ff = wo_ref.shape[0]
    h = _modulated(x, shift, scale)
    gu = jnp.dot(h, wi_ref[...], preferred_element_type=F32)
    g = gu[:, :d_ff]
    u = gu[:, d_ff:]
    act = (g * jax.nn.sigmoid(g) * u).astype(BF16)
    ff = jnp.dot(act, wo_ref[...], preferred_element_type=F32)
    return x + (0.5 * gate) * ff


def _pre_kernel(*refs, n_q, n_k, v_cols, rope):
    if rope:
        (x_ref, mod_ref, wi_ref, wo_ref, wqkv_ref, bd_ref, gq_ref, gk_ref,
         cos_ref, sa_ref, sb_ref, x_out, q_out, k_out, vt_out) = refs
    else:
        (x_ref, mod_ref, wi_ref, wo_ref, wqkv_ref, bd_ref, gq_ref, gk_ref,
         x_out, q_out, k_out, vt_out) = refs
    mod = mod_ref[0]
    x1 = _ffn_half_step(x_ref[0], mod[0:1], mod[1:2], mod[2:3], wi_ref, wo_ref)
    x_out[0] = x1
    h = _modulated(x1, mod[3:4], mod[4:5])
    qkv = jnp.dot(h, wqkv_ref[...], preferred_element_type=F32)

    bd = bd_ref[...]
    if rope:
        cos, sa, sb = cos_ref[...], sa_ref[...], sb_ref[...]

    def norm_rope_store(col0, n_slabs, gain, out):
        for s in range(0, n_slabs, 2):
            y = qkv[:, col0 + s * LANES: col0 + (s + 2) * LANES]
            ss = jnp.dot((y * y).astype(BF16), bd, preferred_element_type=F32)
            y = y * lax.rsqrt(ss * (1.0 / HEAD_DIM) + EPS)
            for t in range(2):
                z = y[:, t * LANES:(t + 1) * LANES] * gain
                if rope:
                    z = (z * cos + pltpu.roll(z, LANES - 16, 1) * sa
                         + pltpu.roll(z, 16, 1) * sb)
                out[0, s + t] = z.astype(BF16)

    norm_rope_store(0, n_q, gq_ref[...], q_out)
    norm_rope_store(n_q * LANES, n_k, gk_ref[...], k_out)

    v0 = (n_q + n_k) * LANES
    vt = qkv[:, v0:v0 + v_cols].T.astype(BF16)
    chunk = vt_out.shape[3]
    for c in range(vt_out.shape[1]):
        vt_out[0, c] = vt[:, c * chunk:(c + 1) * chunk]


def _post_kernel(x_ref, o_ref, mod_ref, wout_ref, wi_ref, wo_ref, x_out):
    mod = mod_ref[0]
    attn = jnp.dot(o_ref[0], wout_ref[...], preferred_element_type=F32)
    x2 = x_ref[0] + mod[5:6] * attn
    x_out[0] = _ffn_half_step(x2, mod[6:7], mod[7:8], mod[8:9], wi_ref, wo_ref)


def _pre_call(x, mod, wi, wo, wqkv, bd, gq, gk, rope_tabs, *, n_q, n_k, v_cols,
              tm, vt_chunk, name):
    bx, lx, d = x.shape
    per_batch_mod = mod.shape[0] > 1
    rope = rope_tabs is not None
    grid = (bx, lx // tm)
    mod_map = (lambda b, i: (b, 0, 0)) if per_batch_mod else (lambda b, i: (0, 0, 0))
    in_specs = [
        pl.BlockSpec((1, tm, d), lambda b, i: (b, i, 0)),
        pl.BlockSpec((1, N_MOD, d), mod_map),
        _resident(wi.shape), _resident(wo.shape), _resident(wqkv.shape),
        _resident(bd.shape), _resident(gq.shape), _resident(gk.shape),
    ]
    args = [x, mod, wi, wo, wqkv, bd, gq, gk]
    if rope:
        in_specs += [pl.BlockSpec((tm, LANES), lambda b, i: (i, 0))] * 3
        args += list(rope_tabs)
    if tm >= vt_chunk:
        vt_block = (1, tm // vt_chunk, v_cols, vt_chunk)
        vt_map = lambda b, i: (b, i, 0, 0)
    else:
        per = vt_chunk // tm
        vt_block = (1, 1, v_cols, tm)
        vt_map = lambda b, i: (b, i // per, 0, i % per)
    out_specs = [
        pl.BlockSpec((1, tm, d), lambda b, i: (b, i, 0)),
        pl.BlockSpec((1, n_q, tm, LANES), lambda b, i: (b, 0, i, 0)),
        pl.BlockSpec((1, n_k, tm, LANES), lambda b, i: (b, 0, i, 0)),
        pl.BlockSpec(vt_block, vt_map),
    ]
    out_shape = [
        jax.ShapeDtypeStruct((bx, lx, d), F32),
        jax.ShapeDtypeStruct((bx, n_q, lx, LANES), BF16),
        jax.ShapeDtypeStruct((bx, n_k, lx, LANES), BF16),
        jax.ShapeDtypeStruct((bx, lx // vt_chunk, v_cols, vt_chunk), BF16),
    ]
    return pl.pallas_call(
        functools.partial(_pre_kernel, n_q=n_q, n_k=n_k, v_cols=v_cols, rope=rope),
        grid=grid, in_specs=in_specs, out_specs=out_specs, out_shape=out_shape,
        compiler_params=pltpu.CompilerParams(
            dimension_semantics=("arbitrary", "arbitrary"),
            vmem_limit_bytes=VMEM_LIMIT),
        name=name,
    )(*args)


def _post_call(x, o, mod, wout, wi, wo, *, tm, name):
    bx, lx, d = x.shape
    per_batch_mod = mod.shape[0] > 1
    mod_map = (lambda b, i: (b, 0, 0)) if per_batch_mod else (lambda b, i: (0, 0, 0))
    return pl.pallas_call(
        _post_kernel,
        grid=(bx, lx // tm),
        in_specs=[
            pl.BlockSpec((1, tm, d), lambda b, i: (b, i, 0)),
            pl.BlockSpec((1, tm, o.shape[2]), lambda b, i: (b, i, 0)),
            pl.BlockSpec((1, N_MOD, d), mod_map),
            _resident(wout.shape), _resident(wi.shape), _resident(wo.shape),
        ],
        out_specs=pl.BlockSpec((1, tm, d), lambda b, i: (b, i, 0)),
        out_shape=jax.ShapeDtypeStruct((bx, lx, d), F32),
        compiler_params=pltpu.CompilerParams(
            dimension_semantics=("arbitrary", "arbitrary"),
            vmem_limit_bytes=VMEM_LIMIT),
        name=name,
    )(x, o, mod, wout, wi, wo)


def _split_components(q_tile):
    qf = q_tile.astype(F32)
    lane = lax.broadcasted_iota(jnp.int32, qf.shape, 1)
    lo = jnp.where(lane < HEAD_DIM, qf, 0.0).astype(BF16)
    hi = jnp.where(lane >= HEAD_DIM, qf, 0.0).astype(BF16)
    return jnp.concatenate([lo, hi], axis=0)


def _attn_a_kernel(*refs, n_lat, lam_init, stabilize):
    if n_lat:
        lam_ref, gain_ref, q_ref, k_ref, vt_ref, kc_ref, vtc_ref, o_ref = refs
    else:
        lam_ref, gain_ref, q_ref, kc_ref, vtc_ref, o_ref = refs
    tq = q_ref.shape[2]
    qq = _split_components(q_ref[0, 0])

    def scores(kb):
        return lax.dot_general(kb, qq, (((1,), (1,)), ((), ())), preferred_element_type=F32)

    def update_unshifted(state, st, vtb):
        p = jnp.exp2(st)
        l_new = jnp.sum(p, axis=0, keepdims=True)
        acc_new = jnp.dot(vtb, p.astype(BF16), preferred_element_type=F32)
        if state is not None:
            l_new = state[1] + l_new
            acc_new = state[2] + acc_new
        return None, l_new, acc_new

    def update(state, st, vtb):
        if not stabilize:
            return update_unshifted(state, st, vtb)
        m_blk = jnp.max(st, axis=0, keepdims=True)
        if state is None:
            m_new = m_blk
        else:
            m_old, l_old, acc_old = state
            m_new = jnp.maximum(m_old, m_blk)
            alpha = jnp.exp2(m_old - m_new)
        p = jnp.exp2(st - m_new)
        l_new = jnp.sum(p, axis=0, keepdims=True)
        acc_new = jnp.dot(vtb, p.astype(BF16), preferred_element_type=F32)
        if state is not None:
            l_new = alpha * l_old + l_new
            acc_new = alpha * acc_old + acc_new
        return m_new, l_new, acc_new

    blocks = []
    if n_lat:
        tkv = vt_ref.shape[3]
        for j in range(n_lat):
            blocks.append((lambda j=j: k_ref[0, 0, j * tkv:(j + 1) * tkv, :],
                           lambda j=j: vt_ref[0, j]))
    ckv = vtc_ref.shape[3]
    for c in range(vtc_ref.shape[1]):
        blocks.append((lambda c=c: kc_ref[0, 0, c * ckv:(c + 1) * ckv, :],
                       lambda c=c: vtc_ref[0, c]))
    state = None
    st = scores(blocks[0][0]())
    for j, (_, load_vt) in enumerate(blocks):
        st_next = scores(blocks[j + 1][0]()) if j + 1 < len(blocks) else None
        state = update(state, st, load_vt())
        st = st_next
    _, l_fin, acc = state

    inv = 1.0 / l_fin
    lv = lam_ref[...]
    lam = (jnp.exp(jnp.sum(lv[0:1] * lv[1:2], axis=-1, keepdims=True))
           - jnp.exp(jnp.sum(lv[2:3] * lv[3:4], axis=-1, keepdims=True)) + lam_init)
    ot = acc[:, :tq] * inv[:, :tq] - lam * (acc[:, tq:] * inv[:, tq:])
    on = ot * lax.rsqrt(jnp.mean(ot * ot, axis=0, keepdims=True) + EPS)
    o_ref[0] = (on.T * (gain_ref[...] * (1.0 - lam_init))).astype(BF16)


def _attn_a_call(lam_vec, gain, q, k, vt, kc, vtc, *, tq, lam_init, stabilize, name):
    b, h, lq, _ = q.shape
    n_lat = 0 if k is None else vt.shape[1]
    in_specs = [
        pl.BlockSpec(lam_vec.shape, lambda b_, h_, i: (0, 0)),
        pl.BlockSpec(gain.shape, lambda b_, h_, i: (0, 0)),
        pl.BlockSpec((1, 1, tq, LANES), lambda b_, h_, i: (b_, h_, i, 0)),
    ]
    args = [lam_vec, gain, q]
    if n_lat:
        in_specs += [
            pl.BlockSpec((1, 1, k.shape[2], LANES), lambda b_, h_, i: (b_, h_, 0, 0)),
            pl.BlockSpec((1, vt.shape[1], LANES, vt.shape[3]), lambda b_, h_, i: (b_, 0, h_, 0)),
        ]
        args += [k, vt]
    in_specs += [
        pl.BlockSpec((1, 1, kc.shape[2], LANES), lambda b_, h_, i: (b_, h_, 0, 0)),
        pl.BlockSpec((1, vtc.shape[1], LANES, vtc.shape[3]), lambda b_, h_, i: (b_, 0, h_, 0)),
    ]
    args += [kc, vtc]
    return pl.pallas_call(
        functools.partial(_attn_a_kernel, n_lat=n_lat, lam_init=lam_init, stabilize=stabilize),
        grid=(b, h, lq // tq),
        in_specs=in_specs,
        out_specs=pl.BlockSpec((1, tq, LANES), lambda b_, h_, i: (b_, i, h_)),
        out_shape=jax.ShapeDtypeStruct((b, lq, h * LANES), BF16),
        compiler_params=pltpu.CompilerParams(
            dimension_semantics=("arbitrary", "arbitrary", "arbitrary"),
            vmem_limit_bytes=VMEM_LIMIT),
        name=name + ("_shifted" if stabilize else ""),
    )(*args)


def _attn_b_kernel(sink_ref, q_ref, kp_ref, ko_ref, kn_ref, kc_ref,
                   vp_ref, vo_ref, vn_ref, vc_ref, o_ref, *, stabilize):
    n = pl.program_id(1)
    nb = pl.num_programs(1)
    bb = WINDOW_BLOCK
    n_pair = kp_ref.shape[1]
    ncol = 2 * B_GROUP * bb
    half = ncol // 2

    def scores(p):
        slabs = [q_ref[0, p * B_GROUP + g].astype(F32) for g in range(B_GROUP)]
        lane = lax.broadcasted_iota(jnp.int32, slabs[0].shape, 1)
        lo = [jnp.where(lane < HEAD_DIM, s, 0.0).astype(BF16) for s in slabs]
        hi = [jnp.where(lane >= HEAD_DIM, s, 0.0).astype(BF16) for s in slabs]
        qq = jnp.concatenate(lo + hi, axis=0)
        kcat = jnp.concatenate([kp_ref[0, p], ko_ref[0, p], kn_ref[0, p], kc_ref[0, p]], axis=0)
        return lax.dot_general(kcat, qq, (((1,), (1,)), ((), ())), preferred_element_type=F32)

    key_j = lax.broadcasted_iota(jnp.int32, (bb, ncol), 0)
    qry_i = lax.broadcasted_iota(jnp.int32, (bb, ncol), 1) & (bb - 1)
    in_prev = key_j >= qry_i + jnp.where(n > 0, 0, bb)
    in_next = key_j <= qry_i - jnp.where(n < nb - 1, 0, bb)

    sts = [scores(p) for p in range(n_pair)]
    for p in range(n_pair):
        st = sts[p]
        sink = sink_ref[p]
        if stabilize:
            s_prev = jnp.where(in_prev, st[0:bb], NEG_INF)
            s_own = st[bb:2 * bb]
            s_next = jnp.where(in_next, st[2 * bb:3 * bb], NEG_INF)
            s_ctx = st[3 * bb:]
            m = jnp.maximum(jnp.maximum(jnp.max(s_prev, axis=0, keepdims=True),
                                        jnp.max(s_own, axis=0, keepdims=True)),
                            jnp.maximum(jnp.max(s_next, axis=0, keepdims=True),
                                        jnp.max(s_ctx, axis=0, keepdims=True)))
            m = jnp.maximum(m, sink)
            parts = [jnp.exp2(s - m) for s in (s_prev, s_own, s_next, s_ctx)]
            l = jnp.exp2(sink - m)
        else:
            parts = [jnp.where(in_prev, jnp.exp2(st[0:bb]), 0.0),
                     jnp.exp2(st[bb:2 * bb]),
                     jnp.where(in_next, jnp.exp2(st[2 * bb:3 * bb]), 0.0),
                     jnp.exp2(st[3 * bb:])]
            l = jnp.exp2(sink)
        for part in parts:
            l = l + jnp.sum(part, axis=0, keepdims=True)
        pt = jnp.concatenate([part.astype(BF16) for part in parts], axis=0)
        rows = slice(p * LANES, (p + 1) * LANES)
        vcat = jnp.concatenate([vp_ref[0, 0, rows, :], vo_ref[0, 0, rows, :], vn_ref[0, 0, rows, :]]
                               + [vc_ref[0, c, rows, :] for c in range(vc_ref.shape[1])], axis=1)
        inv = 1.0 / l
        for e in range(2):
            ot = jnp.dot(vcat[e * HEAD_DIM:(e + 1) * HEAD_DIM], pt[:, e * half:(e + 1) * half],
                         preferred_element_type=F32) * inv[:, e * half:(e + 1) * half]
            for gp in range(B_GROUP // 2):
                blk = jnp.concatenate([ot[:, (2 * gp) * bb:(2 * gp + 1) * bb],
                                       ot[:, (2 * gp + 1) * bb:(2 * gp + 2) * bb]], axis=0)
                c0 = ((p * 2 + e) * (B_GROUP // 2) + gp) * LANES
                o_ref[0, :, c0:c0 + LANES] = blk.T.astype(BF16)


def _attn_b_call(sink_rows, q, k, vt, kc, vtc, *, stabilize, name):
    b, n_slab, l, _ = q.shape
    n_pair = k.shape[1]
    nb = l // WINDOW_BLOCK
    bb = WINDOW_BLOCK
    prev = lambda n: jnp.maximum(n - 1, 0)
    nxt = lambda n: jnp.minimum(n + 1, nb - 1)
    ident = lambda n: n
    kspec = lambda f: pl.BlockSpec((1, n_pair, bb, LANES), lambda b_, n: (b_, 0, f(n), 0))
    vspec = lambda f: pl.BlockSpec((1, 1, vt.shape[2], bb), lambda b_, n: (b_, f(n), 0, 0))
    whole = lambda a: pl.BlockSpec((1,) + a.shape[1:], lambda b_, n: (b_,) + (0,) * (a.ndim - 1))
    return pl.pallas_call(
        functools.partial(_attn_b_kernel, stabilize=stabilize),
        grid=(b, nb),
        in_specs=[
            pl.BlockSpec(sink_rows.shape, lambda b_, n: (0, 0, 0)),
            pl.BlockSpec((1, n_slab, bb, LANES), lambda b_, n: (b_, 0, n, 0)),
            kspec(prev), kspec(ident), kspec(nxt), whole(kc),
            vspec(prev), vspec(ident), vspec(nxt), whole(vtc),
        ],
        out_specs=pl.BlockSpec((1, bb, n_slab * LANES), lambda b_, n: (b_, n, 0)),
        out_shape=jax.ShapeDtypeStruct((b, l, n_slab * LANES), BF16),
        compiler_params=pltpu.CompilerParams(
            dimension_semantics=("arbitrary", "arbitrary"),
            vmem_limit_bytes=VMEM_LIMIT),
        name=name + ("_shifted" if stabilize else ""),
    )(sink_rows, q, k, k, k, kc, vt, vt, vt, vtc)


def _rope_tables(n_tok):
    rows = n_tok // GRID_W
    nf = HEAD_DIM // 4
    inv = ROPE_BASE ** (-jnp.arange(nf, dtype=F32) / nf)
    row = jnp.broadcast_to(jnp.arange(rows, dtype=F32)[:, None], (rows, GRID_W)).reshape(-1)
    col = jnp.broadcast_to(jnp.arange(GRID_W, dtype=F32)[None, :], (rows, GRID_W)).reshape(-1)
    ang = jnp.stack([row[:, None] * inv, col[:, None] * inv], axis=1)
    ang = jnp.stack([ang, ang], axis=2).reshape(n_tok, HEAD_DIM)
    cos = jnp.tile(jnp.cos(ang), (1, LANES // HEAD_DIM))
    sin = jnp.tile(jnp.sin(ang), (1, LANES // HEAD_DIM))
    first_half = (jnp.arange(LANES) % (2 * nf)) < nf
    return cos, jnp.where(first_half, -sin, 0.0), jnp.where(first_half, 0.0, sin)


def _group_sum_matrix():
    idx = jnp.arange(MXU_COLS) // HEAD_DIM
    return (idx[:, None] == idx[None, :]).astype(BF16)


def _score_bound(gq_row, gk_row):
    return 1.05 * HEAD_DIM * jnp.max(jnp.abs(gq_row)) * jnp.max(jnp.abs(gk_row))


def _pair_q_heads(w_qkv):
    order = []
    for p in range(B_KV_HEADS // 2):
        for g in range(B_GROUP):
            order += [(2 * p) * B_GROUP + g, (2 * p + 1) * B_GROUP + g]
    d = w_qkv.shape[0]
    wq = w_qkv[:, :B_Q_HEADS * HEAD_DIM].reshape(d, B_Q_HEADS, HEAD_DIM)
    wq = wq[:, jnp.array(order)].reshape(d, B_Q_HEADS * HEAD_DIM)
    return jnp.concatenate([wq, w_qkv[:, B_Q_HEADS * HEAD_DIM:]], axis=1)


def kernel(x, c, ctx, c_ctx, ada_w, ada_b, ffn_pre_wi, ffn_pre_wo, ffn_post_wi, ffn_post_wo,
           a_w_qkv, a_w_o, a_q_gain, a_k_gain, a_lambda, a_subln_gain,
           b_w_qkv, b_w_o, b_q_gain, b_k_gain, b_sink):
    bsz, seq, d = x.shape
    n_ctx = ctx.shape[1]
    depth = ada_w.shape[0]
    assert depth == 2 and seq % WINDOW_BLOCK == 0

    rows = 8
    cs = jnp.zeros((rows, d), F32).at[:bsz].set(c).at[bsz].set(c_ctx)
    mods = _ada(cs, ada_w, ada_b)
    mod_l = [mods[i, :bsz].reshape(bsz, N_MOD, d) for i in range(depth)]
    mod_c = [mods[i, bsz:bsz + 1].reshape(1, N_MOD, d) for i in range(depth)]

    rope_tabs = _rope_tables(seq)
    bd = _group_sum_matrix()
    qk_scale = HEAD_DIM ** -0.5 * LOG2E
    tile2 = lambda g: jnp.tile(g.astype(F32), LANES // HEAD_DIM).reshape(1, LANES)

    pre_wi = [w.astype(BF16) for w in ffn_pre_wi]
    pre_wo = [w.astype(BF16) for w in ffn_pre_wo]
    post_wi = [w.astype(BF16) for w in ffn_post_wi]
    post_wo = [w.astype(BF16) for w in ffn_post_wo]

    tm = 256
    tkv = 512
    tq = 512

    lam_init = 0.8 - 0.6 * math.exp(-0.3 * 0)
    a_cfg = dict(n_q=A_HEADS, n_k=A_HEADS, v_cols=A_HEADS * LANES)
    wqkv_a = a_w_qkv[0].astype(BF16)
    gq_a, gk_a = tile2(a_q_gain[0]) * qk_scale, tile2(a_k_gain[0])
    x1, q, k, vt = _pre_call(x, mod_l[0], pre_wi[0], pre_wo[0], wqkv_a, bd, gq_a, gk_a,
                             rope_tabs, tm=tm, vt_chunk=tkv, name="pre0_lat", **a_cfg)
    xc1, qc, kc, vtc = _pre_call(ctx, mod_c[0], pre_wi[0], pre_wo[0], wqkv_a, bd, gq_a, gk_a,
                                 None, tm=n_ctx, vt_chunk=n_ctx, name="pre0_ctx", **a_cfg)
    sub_gain = a_subln_gain[0].astype(F32).reshape(1, LANES)
    lam_vec = a_lambda[0].astype(F32)
    unshifted_ok = _score_bound(gq_a, gk_a) <= UNSHIFTED_SCORE_LIMIT

    def attn_a(stabilize):
        def run(q, k, vt, qc, kc, vtc):
            o = _attn_a_call(lam_vec, sub_gain, q, k, vt, kc, vtc, tq=tq, lam_init=lam_init,
                             stabilize=stabilize, name="attn_a_lat")
            oc = _attn_a_call(lam_vec, sub_gain, qc, None, None, kc, vtc, tq=n_ctx,
                              lam_init=lam_init, stabilize=stabilize, name="attn_a_ctx")
            return o, oc
        return run

    o, oc = lax.cond(unshifted_ok, attn_a(False), attn_a(True), q, k, vt, qc, kc, vtc)
    wout_a = a_w_o[0].astype(BF16)
    x2 = _post_call(x1, o, mod_l[0], wout_a, post_wi[0], post_wo[0], tm=tm, name="post0_lat")
    xc2 = _post_call(xc1, oc, mod_c[0], wout_a, post_wi[0], post_wo[0], tm=n_ctx,
                     name="post0_ctx")

    b_cfg = dict(n_q=B_Q_HEADS // 2, n_k=B_KV_HEADS // 2, v_cols=B_KV_HEADS * HEAD_DIM)
    wqkv_b = _pair_q_heads(b_w_qkv[0]).astype(BF16)
    gq_b, gk_b = tile2(b_q_gain[0]) * qk_scale, tile2(b_k_gain[0])
    x3, q, k, vt = _pre_call(x2, mod_l[1], pre_wi[1], pre_wo[1], wqkv_b, bd, gq_b, gk_b,
                             rope_tabs, tm=tm, vt_chunk=WINDOW_BLOCK, name="pre1_lat", **b_cfg)
    _, _, kc, vtc = _pre_call(xc2, mod_c[1], pre_wi[1], pre_wo[1], wqkv_b, bd, gq_b, gk_b,
                              None, tm=n_ctx, vt_chunk=WINDOW_BLOCK, name="pre1_ctx", **b_cfg)
    sink = (b_sink[0].astype(F32) * LOG2E).reshape(B_KV_HEADS // 2, 2 * B_GROUP, 1)
    sink_rows = jnp.broadcast_to(sink, (B_KV_HEADS // 2, 2 * B_GROUP, WINDOW_BLOCK))
    sink_rows = sink_rows.reshape(B_KV_HEADS // 2, 1, 2 * B_GROUP * WINDOW_BLOCK)
    logit_bound = jnp.maximum(_score_bound(gq_b, gk_b), jnp.max(jnp.abs(sink_rows)))
    o = lax.cond(
        logit_bound <= UNSHIFTED_SCORE_LIMIT,
        functools.partial(_attn_b_call, stabilize=False, name="attn_b"),
        functools.partial(_attn_b_call, stabilize=True, name="attn_b"),
        sink_rows, q, k, vt, kc, vtc)
    x4 = _post_call(x3, o, mod_l[1], b_w_o[0].astype(BF16), post_wi[1], post_wo[1], tm=tm,
                    name="post1_lat")
    return x4
```

```python
import functools
import math

import jax
import jax.numpy as jnp
from jax import lax
from jax.experimental import pallas as pl
from jax.experimental.pallas import tpu as pltpu

F32 = jnp.float32
BF16 = jnp.bfloat16

LANES = 128
MXU_COLS = 256
HEAD_DIM = 64
N_MOD = 9
EPS = 1e-6
ROPE_BASE = 10000.0
GRID_W = 64
WINDOW_BLOCK = 128
NEG_INF = -1e30
LOG2E = math.log2(math.e)
UNSHIFTED_SCORE_LIMIT = 96.0
VMEM_LIMIT = 56 * 1024 * 1024

A_HEADS = 8
B_Q_HEADS = 16
B_KV_HEADS = 4
B_GROUP = B_Q_HEADS // B_KV_HEADS


def _resident(shape):
    nd = len(shape)
    return pl.BlockSpec(shape, lambda *_: (0,) * nd, pipeline_mode=pl.Buffered(1))


def _resident_layer(stack, layer):
    _, rows, cols = stack.shape
    return pl.BlockSpec((None, rows, cols), lambda *_: (layer, 0, 0),
                        pipeline_mode=pl.Buffered(1))


def _cast_kernel(w_ref, o_ref):
    o_ref[...] = w_ref[...].astype(BF16)


def _to_bf16(w, *, name):
    n, rows, cols = w.shape
    rb = min(rows, 256)
    assert rows % rb == 0
    spec = pl.BlockSpec((1, rb, cols), lambda i, r: (i, r, 0))
    return pl.pallas_call(
        _cast_kernel, grid=(n, rows // rb), in_specs=[spec], out_specs=spec,
        out_shape=jax.ShapeDtypeStruct(w.shape, BF16),
        compiler_params=pltpu.CompilerParams(
            dimension_semantics=("arbitrary", "arbitrary"),
            vmem_limit_bytes=VMEM_LIMIT),
        name=name,
    )(w)


def _ada_kernel(cs_ref, w_ref, b_ref, out_ref):
    s = cs_ref[...]
    s = s * jax.nn.sigmoid(s)
    out_ref[0] = jnp.dot(s.astype(BF16), w_ref[0].astype(BF16),
                         preferred_element_type=F32) + b_ref[0]


def _ada(cs, ada_w, ada_b):
    depth, d, n = ada_w.shape
    rows = cs.shape[0]
    tn = n // 8
    return pl.pallas_call(
        _ada_kernel,
        grid=(depth, n // tn),
        in_specs=[
            pl.BlockSpec((rows, d), lambda i, j: (0, 0)),
            pl.BlockSpec((1, d, tn), lambda i, j: (i, 0, j)),
            pl.BlockSpec((1, 1, tn), lambda i, j: (i, 0, j)),
        ],
        out_specs=pl.BlockSpec((1, rows, tn), lambda i, j: (i, 0, j)),
        out_shape=jax.ShapeDtypeStruct((depth, rows, n), F32),
        compiler_params=pltpu.CompilerParams(
            dimension_semantics=("arbitrary", "arbitrary"),
            vmem_limit_bytes=VMEM_LIMIT),
        name="ada_mod",
    )(cs, ada_w, ada_b.reshape(depth, 1, n))


def _rms(x):
    return x * lax.rsqrt(jnp.mean(x * x, axis=-1, keepdims=True) + EPS)


def _modulated(x, shift, scale):
    return (_rms(x) * (1.0 + scale) + shift).astype(BF16)


def _ffn_half_step(x, shift, scale, gate, wi_ref, wo_ref):
    d_ff = wo_ref.shape[0]
    h = _modulated(x, shift, scale)
    gu = jnp.dot(h, wi_ref[...], preferred_element_type=F32)
    g = gu[:, :d_ff]
    u = gu[:, d_ff:]
    act = (g * jax.nn.sigmoid(g) * u).astype(BF16)
    ff = jnp.dot(act, wo_ref[...], preferred_element_type=F32)
    return x + (0.5 * gate) * ff


def _pre_kernel(*refs, n_q, n_k, v_cols, rope):
    if rope:
        (x_ref, mod_ref, wi_ref, wo_ref, wqkv_ref, bd_ref, gq_ref, gk_ref,
         cos_ref, sa_ref, sb_ref, x_out, q_out, k_out, vt_out, qkv_sc) = refs
    else:
        (x_ref, mod_ref, wi_ref, wo_ref, wqkv_ref, bd_ref, gq_ref, gk_ref,
         x_out, q_out, k_out, vt_out, qkv_sc) = refs

    @pl.when(pl.program_id(0) == 0)
    def _():
        qkv_sc[...] = jnp.zeros(qkv_sc.shape, F32)

    bd = bd_ref[...]
    if rope:
        cos, sa, sb = cos_ref[...], sa_ref[...], sb_ref[...]

    def norm_rope_store(col0, n_slabs, gain, out):
        for s in range(0, n_slabs, 2):
            y = qkv_sc[:, col0 + s * LANES: col0 + (s + 2) * LANES]
            ss = jnp.dot((y * y).astype(BF16), bd, preferred_element_type=F32)
            y = y * lax.rsqrt(ss * (1.0 / HEAD_DIM) + EPS)
            for t in range(2):
                z = y[:, t * LANES:(t + 1) * LANES] * gain
                if rope:
                    z = (z * cos + pltpu.roll(z, LANES - 16, 1) * sa
                         + pltpu.roll(z, 16, 1) * sb)
                out[0, s + t] = z.astype(BF16)

    norm_rope_store(0, n_q, gq_ref[...], q_out)
    norm_rope_store(n_q * LANES, n_k, gk_ref[...], k_out)

    v0 = (n_q + n_k) * LANES
    vt = qkv_sc[:, v0:v0 + v_cols].T.astype(BF16)
    chunk = vt_out.shape[3]
    for c in range(vt_out.shape[1]):
        vt_out[0, c] = vt[:, c * chunk:(c + 1) * chunk]

    mod = mod_ref[0]
    x1 = _ffn_half_step(x_ref[0], mod[0:1], mod[1:2], mod[2:3], wi_ref, wo_ref)
    x_out[0] = x1
    h = _modulated(x1, mod[3:4], mod[4:5])
    qkv_sc[...] = jnp.dot(h, wqkv_ref[...], preferred_element_type=F32)


def _post_kernel(x_ref, o_ref, mod_ref, wout_ref, wi_ref, wo_ref, x_out):
    mod = mod_ref[0]
    attn = jnp.dot(o_ref[0], wout_ref[...], preferred_element_type=F32)
    x2 = x_ref[0] + mod[5:6] * attn
    x_out[0] = _ffn_half_step(x2, mod[6:7], mod[7:8], mod[8:9], wi_ref, wo_ref)


def _pre_call(x, mod, wi, wo, wqkv, bd, gq, gk, rope_tabs, *, layer, n_q, n_k, v_cols,
              tm, vt_chunk, name):
    bx, lx, d = x.shape
    per_batch_mod = mod.shape[0] > 1
    rope = rope_tabs is not None
    nt = lx // tm
    n_tiles = bx * nt
    grid = (n_tiles + 1,)

    def cur(g):
        t = jnp.minimum(g, n_tiles - 1)
        return t // nt, t % nt

    def lag(g):
        t = jnp.maximum(g - 1, 0)
        return t // nt, t % nt

    def at_cur(f):
        return lambda g: f(*cur(g))

    def at_lag(f):
        return lambda g: f(*lag(g))

    in_specs = [
        pl.BlockSpec((1, tm, d), at_cur(lambda b, i: (b, i, 0))),
        pl.BlockSpec((1, N_MOD, d), at_cur(lambda b, i: (b if per_batch_mod else 0, 0, 0))),
        _resident_layer(wi, layer), _resident_layer(wo, layer), _resident(wqkv.shape),
        _resident(bd.shape), _resident(gq.shape), _resident(gk.shape),
    ]
    args = [x, mod, wi, wo, wqkv, bd, gq, gk]
    if rope:
        in_specs += [pl.BlockSpec((tm, LANES), at_lag(lambda b, i: (i, 0)))] * 3
        args += list(rope_tabs)
    if tm >= vt_chunk:
        vt_block = (1, tm // vt_chunk, v_cols, vt_chunk)
        vt_map = at_lag(lambda b, i: (b, i, 0, 0))
    else:
        per = vt_chunk // tm
        vt_block = (1, 1, v_cols, tm)
        vt_map = at_lag(lambda b, i: (b, i // per, 0, i % per))
    out_specs = [
        pl.BlockSpec((1, tm, d), at_cur(lambda b, i: (b, i, 0))),
        pl.BlockSpec((1, n_q, tm, LANES), at_lag(lambda b, i: (b, 0, i, 0))),
        pl.BlockSpec((1, n_k, tm, LANES), at_lag(lambda b, i: (b, 0, i, 0))),
        pl.BlockSpec(vt_block, vt_map),
    ]
    out_shape = [
        jax.ShapeDtypeStruct((bx, lx, d), F32),
        jax.ShapeDtypeStruct((bx, n_q, lx, LANES), BF16),
        jax.ShapeDtypeStruct((bx, n_k, lx, LANES), BF16),
        jax.ShapeDtypeStruct((bx, lx // vt_chunk, v_cols, vt_chunk), BF16),
    ]
    return pl.pallas_call(
        functools.partial(_pre_kernel, n_q=n_q, n_k=n_k, v_cols=v_cols, rope=rope),
        grid=grid, in_specs=in_specs, out_specs=out_specs, out_shape=out_shape,
        scratch_shapes=[pltpu.VMEM((tm, wqkv.shape[1]), F32)],
        compiler_params=pltpu.CompilerParams(
            dimension_semantics=("arbitrary",),
            vmem_limit_bytes=VMEM_LIMIT),
        name=name,
    )(*args)


def _post_call(x, o, mod, wout, wi, wo, *, layer, tm, name):
    bx, lx, d = x.shape
    per_batch_mod = mod.shape[0] > 1
    mod_map = (lambda b, i: (b, 0, 0)) if per_batch_mod else (lambda b, i: (0, 0, 0))
    return pl.pallas_call(
        _post_kernel,
        grid=(bx, lx // tm),
        in_specs=[
            pl.BlockSpec((1, tm, d), lambda b, i: (b, i, 0)),
            pl.BlockSpec((1, tm, o.shape[2]), lambda b, i: (b, i, 0)),
            pl.BlockSpec((1, N_MOD, d), mod_map),
            _resident(wout.shape), _resident_layer(wi, layer), _resident_layer(wo, layer),
        ],
        out_specs=pl.BlockSpec((1, tm, d), lambda b, i: (b, i, 0)),
        out_shape=jax.ShapeDtypeStruct((bx, lx, d), F32),
        compiler_params=pltpu.CompilerParams(
            dimension_semantics=("arbitrary", "arbitrary"),
            vmem_limit_bytes=VMEM_LIMIT),
        name=name,
    )(x, o, mod, wout, wi, wo)


def _split_components(q_tile):
    qf = q_tile.astype(F32)
    lane = lax.broadcasted_iota(jnp.int32, qf.shape, 1)
    lo = jnp.where(lane < HEAD_DIM, qf, 0.0).astype(BF16)
    hi = jnp.where(lane >= HEAD_DIM, qf, 0.0).astype(BF16)
    return jnp.concatenate([lo, hi], axis=0)


def _attn_a_kernel(*refs, n_lat, lam_init, stabilize):
    if n_lat:
        lam_ref, gain_ref, q_ref, k_ref, vt_ref, kc_ref, vtc_ref, o_ref = refs
    else:
        lam_ref, gain_ref, q_ref, kc_ref, vtc_ref, o_ref = refs
    tq = q_ref.shape[2]
    qq = _split_components(q_ref[0, 0])

    def scores(kb):
        return lax.dot_general(kb, qq, (((1,), (1,)), ((), ())), preferred_element_type=F32)

    def update_unshifted(state, st, vtb):
        p = jnp.exp2(st)
        l_new = jnp.sum(p, axis=0, keepdims=True)
        acc_new = jnp.dot(vtb, p.astype(BF16), preferred_element_type=F32)
        if state is not None:
            l_new = state[1] + l_new
            acc_new = state[2] + acc_new
        return None, l_new, acc_new

    def update(state, st, vtb):
        if not stabilize:
            return update_unshifted(state, st, vtb)
        m_blk = jnp.max(st, axis=0, keepdims=True)
        if state is None:
            m_new = m_blk
        else:
            m_old, l_old, acc_old = state
            m_new = jnp.maximum(m_old, m_blk)
            alpha = jnp.exp2(m_old - m_new)
        p = jnp.exp2(st - m_new)
        l_new = jnp.sum(p, axis=0, keepdims=True)
        acc_new = jnp.dot(vtb, p.astype(BF16), preferred_element_type=F32)
        if state is not None:
            l_new = alpha * l_old + l_new
            acc_new = alpha * acc_old + acc_new
        return m_new, l_new, acc_new

    blocks = []
    if n_lat:
        tkv = vt_ref.shape[3]
        for j in range(n_lat):
            blocks.append((lambda j=j: k_ref[0, 0, j * tkv:(j + 1) * tkv, :],
                           lambda j=j: vt_ref[0, j]))
    ckv = vtc_ref.shape[3]
    for c in range(vtc_ref.shape[1]):
        blocks.append((lambda c=c: kc_ref[0, 0, c * ckv:(c + 1) * ckv, :],
                       lambda c=c: vtc_ref[0, c]))
    state = None
    st = scores(blocks[0][0]())
    for j, (_, load_vt) in enumerate(blocks):
        st_next = scores(blocks[j + 1][0]()) if j + 1 < len(blocks) else None
        state = update(state, st, load_vt())
        st = st_next
    _, l_fin, acc = state

    inv = 1.0 / l_fin
    lv = lam_ref[...]
    lam = (jnp.exp(jnp.sum(lv[0:1] * lv[1:2], axis=-1, keepdims=True))
           - jnp.exp(jnp.sum(lv[2:3] * lv[3:4], axis=-1, keepdims=True)) + lam_init)
    ot = acc[:, :tq] * inv[:, :tq] - lam * (acc[:, tq:] * inv[:, tq:])
    on = ot * lax.rsqrt(jnp.mean(ot * ot, axis=0, keepdims=True) + EPS)
    o_ref[0] = (on.T * (gain_ref[...] * (1.0 - lam_init))).astype(BF16)


def _attn_a_call(lam_vec, gain, q, k, vt, kc, vtc, *, tq, lam_init, stabilize, name):
    b, h, lq, _ = q.shape
    n_lat = 0 if k is None else vt.shape[1]
    in_specs = [
        pl.BlockSpec(lam_vec.shape, lambda b_, h_, i: (0, 0)),
        pl.BlockSpec(gain.shape, lambda b_, h_, i: (0, 0)),
        pl.BlockSpec((1, 1, tq, LANES), lambda b_, h_, i: (b_, h_, i, 0)),
    ]
    args = [lam_vec, gain, q]
    if n_lat:
        in_specs += [
            pl.BlockSpec((1, 1, k.shape[2], LANES), lambda b_, h_, i: (b_, h_, 0, 0)),
            pl.BlockSpec((1, vt.shape[1], LANES, vt.shape[3]), lambda b_, h_, i: (b_, 0, h_, 0)),
        ]
        args += [k, vt]
    in_specs += [
        pl.BlockSpec((1, 1, kc.shape[2], LANES), lambda b_, h_, i: (b_, h_, 0, 0)),
        pl.BlockSpec((1, vtc.shape[1], LANES, vtc.shape[3]), lambda b_, h_, i: (b_, 0, h_, 0)),
    ]
    args += [kc, vtc]
    return pl.pallas_call(
        functools.partial(_attn_a_kernel, n_lat=n_lat, lam_init=lam_init, stabilize=stabilize),
        grid=(b, h, lq // tq),
        in_specs=in_specs,
        out_specs=pl.BlockSpec((1, tq, LANES), lambda b_, h_, i: (b_, i, h_)),
        out_shape=jax.ShapeDtypeStruct((b, lq, h * LANES), BF16),
        compiler_params=pltpu.CompilerParams(
            dimension_semantics=("arbitrary", "arbitrary", "arbitrary"),
            vmem_limit_bytes=VMEM_LIMIT),
        name=name + ("_shifted" if stabilize else ""),
    )(*args)


def _attn_b_kernel(sink_ref, q_ref, kp_ref, ko_ref, kn_ref, kc_ref,
                   vp_ref, vo_ref, vn_ref, vc_ref, o_ref, *, stabilize):
    n = pl.program_id(1)
    nb = pl.num_programs(1)
    bb = WINDOW_BLOCK
    n_pair = kp_ref.shape[1]
    ncol = 2 * B_GROUP * bb
    half = ncol // 2

    def scores(p):
        slabs = [q_ref[0, p * B_GROUP + g].astype(F32) for g in range(B_GROUP)]
        lane = lax.broadcasted_iota(jnp.int32, slabs[0].shape, 1)
        lo = [jnp.where(lane < HEAD_DIM, s, 0.0).astype(BF16) for s in slabs]
        hi = [jnp.where(lane >= HEAD_DIM, s, 0.0).astype(BF16) for s in slabs]
        qq = jnp.concatenate(lo + hi, axis=0)
        kcat = jnp.concatenate([kp_ref[0, p], ko_ref[0, p], kn_ref[0, p], kc_ref[0, p]], axis=0)
        return lax.dot_general(kcat, qq, (((1,), (1,)), ((), ())), preferred_element_type=F32)

    key_j = lax.broadcasted_iota(jnp.int32, (bb, ncol), 0)
    qry_i = lax.broadcasted_iota(jnp.int32, (bb, ncol), 1) & (bb - 1)
    in_prev = key_j >= qry_i + jnp.where(n > 0, 0, bb)
    in_next = key_j <= qry_i - jnp.where(n < nb - 1, 0, bb)

    sts = [scores(p) for p in range(n_pair)]
    for p in range(n_pair):
        st = sts[p]
        sink = sink_ref[p]
        if stabilize:
            s_prev = jnp.where(in_prev, st[0:bb], NEG_INF)
            s_own = st[bb:2 * bb]
            s_next = jnp.where(in_next, st[2 * bb:3 * bb], NEG_INF)
            s_ctx = st[3 * bb:]
            m = jnp.maximum(jnp.maximum(jnp.max(s_prev, axis=0, keepdims=True),
                                        jnp.max(s_own, axis=0, keepdims=True)),
                            jnp.maximum(jnp.max(s_next, axis=0, keepdims=True),
                                        jnp.max(s_ctx, axis=0, keepdims=True)))
            m = jnp.maximum(m, sink)
            parts = [jnp.exp2(s - m) for s in (s_prev, s_own, s_next, s_ctx)]
            l = jnp.exp2(sink - m)
        else:
            parts = [jnp.where(in_prev, jnp.exp2(st[0:bb]), 0.0),
                     jnp.exp2(st[bb:2 * bb]),
                     jnp.where(in_next, jnp.exp2(st[2 * bb:3 * bb]), 0.0),
                     jnp.exp2(st[3 * bb:])]
            l = jnp.exp2(sink)
        for part in parts:
            l = l + jnp.sum(part, axis=0, keepdims=True)
        pt = jnp.concatenate([part.astype(BF16) for part in parts], axis=0)
        rows = slice(p * LANES, (p + 1) * LANES)
        vcat = jnp.concatenate([vp_ref[0, 0, rows, :], vo_ref[0, 0, rows, :], vn_ref[0, 0, rows, :]]
                               + [vc_ref[0, c, rows, :] for c in range(vc_ref.shape[1])], axis=1)
        inv = 1.0 / l
        for e in range(2):
            ot = jnp.dot(vcat[e * HEAD_DIM:(e + 1) * HEAD_DIM], pt[:, e * half:(e + 1) * half],
                         preferred_element_type=F32) * inv[:, e * half:(e + 1) * half]
            for gp in range(B_GROUP // 2):
                blk = jnp.concatenate([ot[:, (2 * gp) * bb:(2 * gp + 1) * bb],
                                       ot[:, (2 * gp + 1) * bb:(2 * gp + 2) * bb]], axis=0)
                c0 = ((p * 2 + e) * (B_GROUP // 2) + gp) * LANES
                o_ref[0, :, c0:c0 + LANES] = blk.T.astype(BF16)


def _attn_b_call(sink_rows, q, k, vt, kc, vtc, *, stabilize, name):
    b, n_slab, l, _ = q.shape
    n_pair = k.shape[1]
    nb = l // WINDOW_BLOCK
    bb = WINDOW_BLOCK
    prev = lambda n: jnp.maximum(n - 1, 0)
    nxt = lambda n: jnp.minimum(n + 1, nb - 1)
    ident = lambda n: n
    kspec = lambda f: pl.BlockSpec((1, n_pair, bb, LANES), lambda b_, n: (b_, 0, f(n), 0))
    vspec = lambda f: pl.BlockSpec((1, 1, vt.shape[2], bb), lambda b_, n: (b_, f(n), 0, 0))
    whole = lambda a: pl.BlockSpec((1,) + a.shape[1:], lambda b_, n: (b_,) + (0,) * (a.ndim - 1))
    return pl.pallas_call(
        functools.partial(_attn_b_kernel, stabilize=stabilize),
        grid=(b, nb),
        in_specs=[
            pl.BlockSpec(sink_rows.shape, lambda b_, n: (0, 0, 0)),
            pl.BlockSpec((1, n_slab, bb, LANES), lambda b_, n: (b_, 0, n, 0)),
            kspec(prev), kspec(ident), kspec(nxt), whole(kc),
            vspec(prev), vspec(ident), vspec(nxt), whole(vtc),
        ],
        out_specs=pl.BlockSpec((1, bb, n_slab * LANES), lambda b_, n: (b_, n, 0)),
        out_shape=jax.ShapeDtypeStruct((b, l, n_slab * LANES), BF16),
        compiler_params=pltpu.CompilerParams(
            dimension_semantics=("arbitrary", "arbitrary"),
            vmem_limit_bytes=VMEM_LIMIT),
        name=name + ("_shifted" if stabilize else ""),
    )(sink_rows, q, k, k, k, kc, vt, vt, vt, vtc)


def _rope_tables(n_tok):
    rows = n_tok // GRID_W
    nf = HEAD_DIM // 4
    inv = ROPE_BASE ** (-jnp.arange(nf, dtype=F32) / nf)
    row = jnp.broadcast_to(jnp.arange(rows, dtype=F32)[:, None], (rows, GRID_W)).reshape(-1)
    col = jnp.broadcast_to(jnp.arange(GRID_W, dtype=F32)[None, :], (rows, GRID_W)).reshape(-1)
    ang = jnp.stack([row[:, None] * inv, col[:, None] * inv], axis=1)
    ang = jnp.stack([ang, ang], axis=2).reshape(n_tok, HEAD_DIM)
    cos = jnp.tile(jnp.cos(ang), (1, LANES // HEAD_DIM))
    sin = jnp.tile(jnp.sin(ang), (1, LANES // HEAD_DIM))
    first_half = (jnp.arange(LANES) % (2 * nf)) < nf
    return cos, jnp.where(first_half, -sin, 0.0), jnp.where(first_half, 0.0, sin)


def _group_sum_matrix():
    idx = jnp.arange(MXU_COLS) // HEAD_DIM
    return (idx[:, None] == idx[None, :]).astype(BF16)


def _score_bound(gq_row, gk_row):
    return 1.05 * HEAD_DIM * jnp.max(jnp.abs(gq_row)) * jnp.max(jnp.abs(gk_row))


def _pair_q_heads(w_qkv):
    order = []
    for p in range(B_KV_HEADS // 2):
        for g in range(B_GROUP):
            order += [(2 * p) * B_GROUP + g, (2 * p + 1) * B_GROUP + g]
    d = w_qkv.shape[0]
    wq = w_qkv[:, :B_Q_HEADS * HEAD_DIM].reshape(d, B_Q_HEADS, HEAD_DIM)
    wq = wq[:, jnp.array(order)].reshape(d, B_Q_HEADS * HEAD_DIM)
    return jnp.concatenate([wq, w_qkv[:, B_Q_HEADS * HEAD_DIM:]], axis=1)


def kernel(x, c, ctx, c_ctx, ada_w, ada_b, ffn_pre_wi, ffn_pre_wo, ffn_post_wi, ffn_post_wo,
           a_w_qkv, a_w_o, a_q_gain, a_k_gain, a_lambda, a_subln_gain,
           b_w_qkv, b_w_o, b_q_gain, b_k_gain, b_sink):
    bsz, seq, d = x.shape
    n_ctx = ctx.shape[1]
    depth = ada_w.shape[0]
    assert depth == 2 and seq % WINDOW_BLOCK == 0

    rows = 8
    cs = jnp.zeros((rows, d), F32).at[:bsz].set(c).at[bsz].set(c_ctx)
    mods = _ada(cs, ada_w, ada_b)
    mod_l = [mods[i, :bsz].reshape(bsz, N_MOD, d) for i in range(depth)]
    mod_c = [mods[i, bsz:bsz + 1].reshape(1, N_MOD, d) for i in range(depth)]

    rope_tabs = _rope_tables(seq)
    bd = _group_sum_matrix()
    qk_scale = HEAD_DIM ** -0.5 * LOG2E
    tile2 = lambda g: jnp.tile(g.astype(F32), LANES // HEAD_DIM).reshape(1, LANES)

    pre_wi = _to_bf16(ffn_pre_wi, name="cast_pre_wi")
    pre_wo = _to_bf16(ffn_pre_wo, name="cast_pre_wo")
    post_wi = _to_bf16(ffn_post_wi, name="cast_post_wi")
    post_wo = _to_bf16(ffn_post_wo, name="cast_post_wo")

    tm = 256
    tm_post = 512
    tkv = 512
    tq = 512

    lam_init = 0.8 - 0.6 * math.exp(-0.3 * 0)
    a_cfg = dict(n_q=A_HEADS, n_k=A_HEADS, v_cols=A_HEADS * LANES)
    wqkv_a = a_w_qkv[0].astype(BF16)
    gq_a, gk_a = tile2(a_q_gain[0]) * qk_scale, tile2(a_k_gain[0])
    x1, q, k, vt = _pre_call(x, mod_l[0], pre_wi, pre_wo, wqkv_a, bd, gq_a, gk_a, rope_tabs,
                             layer=0, tm=tm, vt_chunk=tkv, name="pre0_lat", **a_cfg)
    xc1, qc, kc, vtc = _pre_call(ctx, mod_c[0], pre_wi, pre_wo, wqkv_a, bd, gq_a, gk_a, None,
                                 layer=0, tm=n_ctx, vt_chunk=n_ctx, name="pre0_ctx", **a_cfg)
    sub_gain = a_subln_gain[0].astype(F32).reshape(1, LANES)
    lam_vec = a_lambda[0].astype(F32)
    unshifted_ok = _score_bound(gq_a, gk_a) <= UNSHIFTED_SCORE_LIMIT

    def attn_a(stabilize):
        def run(q, k, vt, qc, kc, vtc):
            o = _attn_a_call(lam_vec, sub_gain, q, k, vt, kc, vtc, tq=tq, lam_init=lam_init,
                             stabilize=stabilize, name="attn_a_lat")
            oc = _attn_a_call(lam_vec, sub_gain, qc, None, None, kc, vtc, tq=n_ctx,
                              lam_init=lam_init, stabilize=stabilize, name="attn_a_ctx")
            return o, oc
        return run

    o, oc = lax.cond(unshifted_ok, attn_a(False), attn_a(True), q, k, vt, qc, kc, vtc)
    wout_a = a_w_o[0].astype(BF16)
    x2 = _post_call(x1, o, mod_l[0], wout_a, post_wi, post_wo, layer=0, tm=tm_post,
                    name="post0_lat")
    xc2 = _post_call(xc1, oc, mod_c[0], wout_a, post_wi, post_wo, layer=0, tm=n_ctx,
                     name="post0_ctx")

    b_cfg = dict(n_q=B_Q_HEADS // 2, n_k=B_KV_HEADS // 2, v_cols=B_KV_HEADS * HEAD_DIM)
    wqkv_b = _pair_q_heads(b_w_qkv[0]).astype(BF16)
    gq_b, gk_b = tile2(b_q_gain[0]) * qk_scale, tile2(b_k_gain[0])
    x3, q, k, vt = _pre_call(x2, mod_l[1], pre_wi, pre_wo, wqkv_b, bd, gq_b, gk_b, rope_tabs,
                             layer=1, tm=tm, vt_chunk=WINDOW_BLOCK, name="pre1_lat", **b_cfg)
    _, _, kc, vtc = _pre_call(xc2, mod_c[1], pre_wi, pre_wo, wqkv_b, bd, gq_b, gk_b, None,
                              layer=1, tm=n_ctx, vt_chunk=WINDOW_BLOCK, name="pre1_ctx", **b_cfg)
    sink = (b_sink[0].astype(F32) * LOG2E).reshape(B_KV_HEADS // 2, 2 * B_GROUP, 1)
    sink_rows = jnp.broadcast_to(sink, (B_KV_HEADS // 2, 2 * B_GROUP, WINDOW_BLOCK))
    sink_rows = sink_rows.reshape(B_KV_HEADS // 2, 1, 2 * B_GROUP * WINDOW_BLOCK)
    logit_bound = jnp.maximum(_score_bound(gq_b, gk_b), jnp.max(jnp.abs(sink_rows)))
    o = lax.cond(
        logit_bound <= UNSHIFTED_SCORE_LIMIT,
        functools.partial(_attn_b_call, stabilize=False, name="attn_b"),
        functools.partial(_attn_b_call, stabilize=True, name="attn_b"),
        sink_rows, q, k, vt, kc, vtc)
    x4 = _post_call(x3, o, mod_l[1], b_w_o[0].astype(BF16), post_wi, post_wo, layer=1,
                    tm=tm_post, name="post1_lat")
    return x4
```

```python
import functools
import math

import jax
import jax.numpy as jnp
from jax import lax
from jax.experimental import pallas as pl
from jax.experimental.pallas import tpu as pltpu

F32 = jnp.float32
BF16 = jnp.bfloat16

LANES = 128
MXU_COLS = 256
HEAD_DIM = 64
N_MOD = 9
EPS = 1e-6
ROPE_BASE = 10000.0
GRID_W = 64
WINDOW_BLOCK = 128
NEG_INF = -1e30
LOG2E = math.log2(math.e)
UNSHIFTED_SCORE_LIMIT = 96.0
VMEM_LIMIT = 56 * 1024 * 1024
CAST_BLOCK_BYTES = 6 * 1024 * 1024

A_HEADS = 8
B_Q_HEADS = 16
B_KV_HEADS = 4
B_GROUP = B_Q_HEADS // B_KV_HEADS


def _resident(shape):
    nd = len(shape)
    return pl.BlockSpec(shape, lambda *_: (0,) * nd, pipeline_mode=pl.Buffered(1))


def _resident_layer(stack, layer):
    _, rows, cols = stack.shape
    return pl.BlockSpec((None, rows, cols), lambda *_: (layer, 0, 0),
                        pipeline_mode=pl.Buffered(1))


def _cast_kernel(w_ref, o_ref):
    o_ref[...] = w_ref[...].astype(BF16)


def _to_bf16(w, *, name):
    n, rows, cols = w.shape
    rb = max(r for r in range(8, rows + 1, 8)
             if rows % r == 0 and r * cols * 4 <= CAST_BLOCK_BYTES)
    spec = pl.BlockSpec((1, rb, cols), lambda i, r: (i, r, 0))
    return pl.pallas_call(
        _cast_kernel, grid=(n, rows // rb), in_specs=[spec], out_specs=spec,
        out_shape=jax.ShapeDtypeStruct(w.shape, BF16),
        compiler_params=pltpu.CompilerParams(
            dimension_semantics=("arbitrary", "arbitrary"),
            vmem_limit_bytes=VMEM_LIMIT),
        name=name,
    )(w)


def _ada_kernel(cs_ref, w_ref, b_ref, out_ref):
    s = cs_ref[...]
    s = s * jax.nn.sigmoid(s)
    out_ref[0] = jnp.dot(s.astype(BF16), w_ref[0].astype(BF16),
                         preferred_element_type=F32) + b_ref[0]


def _ada(cs, ada_w, ada_b):
    depth, d, n = ada_w.shape
    rows = cs.shape[0]
    tn = n // 8
    return pl.pallas_call(
        _ada_kernel,
        grid=(depth, n // tn),
        in_specs=[
            pl.BlockSpec((rows, d), lambda i, j: (0, 0)),
            pl.BlockSpec((1, d, tn), lambda i, j: (i, 0, j)),
            pl.BlockSpec((1, 1, tn), lambda i, j: (i, 0, j)),
        ],
        out_specs=pl.BlockSpec((1, rows, tn), lambda i, j: (i, 0, j)),
        out_shape=jax.ShapeDtypeStruct((depth, rows, n), F32),
        compiler_params=pltpu.CompilerParams(
            dimension_semantics=("arbitrary", "arbitrary"),
            vmem_limit_bytes=VMEM_LIMIT),
        name="ada_mod",
    )(cs, ada_w, ada_b.reshape(depth, 1, n))


def _rms(x):
    return x * lax.rsqrt(jnp.mean(x * x, axis=-1, keepdims=True) + EPS)


def _modulated(x, shift, scale):
    return (_rms(x) * (1.0 + scale) + shift).astype(BF16)


def _ffn_half_step(x, shift, scale, gate, wi_ref, wo_ref):
    d_ff = wo_ref.shape[0]
    h = _modulated(x, shift, scale)
    gu = jnp.dot(h, wi_ref[...], preferred_element_type=F32)
    g = gu[:, :d_ff]
    u = gu[:, d_ff:]
    act = (g * jax.nn.sigmoid(g) * u).astype(BF16)
    ff = jnp.dot(act, wo_ref[...], preferred_element_type=F32)
    return x + (0.5 * gate) * ff


def _pre_kernel(*refs, n_q, n_k, v_cols, rope):
    if rope:
        (x_ref, mod_ref, wi_ref, wo_ref, wqkv_ref, bd_ref, gq_ref, gk_ref,
         cos_ref, sa_ref, sb_ref, x_out, q_out, k_out, vt_out, qkv_sc) = refs
    else:
        (x_ref, mod_ref, wi_ref, wo_ref, wqkv_ref, bd_ref, gq_ref, gk_ref,
         x_out, q_out, k_out, vt_out, qkv_sc) = refs

    @pl.when(pl.program_id(0) == 0)
    def _():
        qkv_sc[...] = jnp.zeros(qkv_sc.shape, F32)

    bd = bd_ref[...]
    if rope:
        cos, sa, sb = cos_ref[...], sa_ref[...], sb_ref[...]

    def norm_rope_store(col0, n_slabs, gain, out):
        for s in range(0, n_slabs, 2):
            y = qkv_sc[:, col0 + s * LANES: col0 + (s + 2) * LANES]
            ss = jnp.dot((y * y).astype(BF16), bd, preferred_element_type=F32)
            y = y * lax.rsqrt(ss * (1.0 / HEAD_DIM) + EPS)
            for t in range(2):
                z = y[:, t * LANES:(t + 1) * LANES] * gain
                if rope:
                    z = (z * cos + pltpu.roll(z, LANES - 16, 1) * sa
                         + pltpu.roll(z, 16, 1) * sb)
                out[0, s + t] = z.astype(BF16)

    norm_rope_store(0, n_q, gq_ref[...], q_out)
    norm_rope_store(n_q * LANES, n_k, gk_ref[...], k_out)

    v0 = (n_q + n_k) * LANES
    vt = qkv_sc[:, v0:v0 + v_cols].T.astype(BF16)
    chunk = vt_out.shape[3]
    for c in range(vt_out.shape[1]):
        vt_out[0, c] = vt[:, c * chunk:(c + 1) * chunk]

    mod = mod_ref[0]
    x1 = _ffn_half_step(x_ref[0], mod[0:1], mod[1:2], mod[2:3], wi_ref, wo_ref)
    x_out[0] = x1
    h = _modulated(x1, mod[3:4], mod[4:5])
    qkv_sc[...] = jnp.dot(h, wqkv_ref[...], preferred_element_type=F32)


def _post_kernel(x_ref, o_ref, mod_ref, wout_ref, wi_ref, wo_ref, x_out):
    mod = mod_ref[0]
    attn = jnp.dot(o_ref[0], wout_ref[...], preferred_element_type=F32)
    x2 = x_ref[0] + mod[5:6] * attn
    x_out[0] = _ffn_half_step(x2, mod[6:7], mod[7:8], mod[8:9], wi_ref, wo_ref)


def _pre_call(x, mod, wi, wo, wqkv, bd, gq, gk, rope_tabs, *, layer, n_q, n_k, v_cols,
              tm, vt_chunk, name):
    bx, lx, d = x.shape
    per_batch_mod = mod.shape[0] > 1
    rope = rope_tabs is not None
    nt = lx // tm
    n_tiles = bx * nt
    grid = (n_tiles + 1,)

    def cur(g):
        t = jnp.minimum(g, n_tiles - 1)
        return t // nt, t % nt

    def lag(g):
        t = jnp.maximum(g - 1, 0)
        return t // nt, t % nt

    def at_cur(f):
        return lambda g: f(*cur(g))

    def at_lag(f):
        return lambda g: f(*lag(g))

    in_specs = [
        pl.BlockSpec((1, tm, d), at_cur(lambda b, i: (b, i, 0))),
        pl.BlockSpec((1, N_MOD, d), at_cur(lambda b, i: (b if per_batch_mod else 0, 0, 0))),
        _resident_layer(wi, layer), _resident_layer(wo, layer), _resident(wqkv.shape),
        _resident(bd.shape), _resident(gq.shape), _resident(gk.shape),
    ]
    args = [x, mod, wi, wo, wqkv, bd, gq, gk]
    if rope:
        in_specs += [pl.BlockSpec((tm, LANES), at_lag(lambda b, i: (i, 0)))] * 3
        args += list(rope_tabs)
    if tm >= vt_chunk:
        vt_block = (1, tm // vt_chunk, v_cols, vt_chunk)
        vt_map = at_lag(lambda b, i: (b, i, 0, 0))
    else:
        per = vt_chunk // tm
        vt_block = (1, 1, v_cols, tm)
        vt_map = at_lag(lambda b, i: (b, i // per, 0, i % per))
    out_specs = [
        pl.BlockSpec((1, tm, d), at_cur(lambda b, i: (b, i, 0))),
        pl.BlockSpec((1, n_q, tm, LANES), at_lag(lambda b, i: (b, 0, i, 0))),
        pl.BlockSpec((1, n_k, tm, LANES), at_lag(lambda b, i: (b, 0, i, 0))),
        pl.BlockSpec(vt_block, vt_map),
    ]
    out_shape = [
        jax.ShapeDtypeStruct((bx, lx, d), F32),
        jax.ShapeDtypeStruct((bx, n_q, lx, LANES), BF16),
        jax.ShapeDtypeStruct((bx, n_k, lx, LANES), BF16),
        jax.ShapeDtypeStruct((bx, lx // vt_chunk, v_cols, vt_chunk), BF16),
    ]
    return pl.pallas_call(
        functools.partial(_pre_kernel, n_q=n_q, n_k=n_k, v_cols=v_cols, rope=rope),
        grid=grid, in_specs=in_specs, out_specs=out_specs, out_shape=out_shape,
        scratch_shapes=[pltpu.VMEM((tm, wqkv.shape[1]), F32)],
        compiler_params=pltpu.CompilerParams(
            dimension_semantics=("arbitrary",),
            vmem_limit_bytes=VMEM_LIMIT),
        name=name,
    )(*args)


def _post_call(x, o, mod, wout, wi, wo, *, layer, tm, name):
    bx, lx, d = x.shape
    per_batch_mod = mod.shape[0] > 1
    mod_map = (lambda b, i: (b, 0, 0)) if per_batch_mod else (lambda b, i: (0, 0, 0))
    return pl.pallas_call(
        _post_kernel,
        grid=(bx, lx // tm),
        in_specs=[
            pl.BlockSpec((1, tm, d), lambda b, i: (b, i, 0)),
            pl.BlockSpec((1, tm, o.shape[2]), lambda b, i: (b, i, 0)),
            pl.BlockSpec((1, N_MOD, d), mod_map),
            _resident(wout.shape), _resident_layer(wi, layer), _resident_layer(wo, layer),
        ],
        out_specs=pl.BlockSpec((1, tm, d), lambda b, i: (b, i, 0)),
        out_shape=jax.ShapeDtypeStruct((bx, lx, d), F32),
        compiler_params=pltpu.CompilerParams(
            dimension_semantics=("arbitrary", "arbitrary"),
            vmem_limit_bytes=VMEM_LIMIT),
        name=name,
    )(x, o, mod, wout, wi, wo)


def _split_components(q_tile):
    qf = q_tile.astype(F32)
    lane = lax.broadcasted_iota(jnp.int32, qf.shape, 1)
    lo = jnp.where(lane < HEAD_DIM, qf, 0.0).astype(BF16)
    hi = jnp.where(lane >= HEAD_DIM, qf, 0.0).astype(BF16)
    return jnp.concatenate([lo, hi], axis=0)


def _attn_a_kernel(*refs, n_lat, lam_init, stabilize):
    if n_lat:
        lam_ref, gain_ref, q_ref, k_ref, vt_ref, kc_ref, vtc_ref, o_ref = refs
    else:
        lam_ref, gain_ref, q_ref, kc_ref, vtc_ref, o_ref = refs
    tq = q_ref.shape[2]
    qq = _split_components(q_ref[0, 0])

    def scores(kb):
        return lax.dot_general(kb, qq, (((1,), (1,)), ((), ())), preferred_element_type=F32)

    def update_unshifted(state, st, vtb):
        p = jnp.exp2(st)
        l_new = jnp.sum(p, axis=0, keepdims=True)
        acc_new = jnp.dot(vtb, p.astype(BF16), preferred_element_type=F32)
        if state is not None:
            l_new = state[1] + l_new
            acc_new = state[2] + acc_new
        return None, l_new, acc_new

    def update(state, st, vtb):
        if not stabilize:
            return update_unshifted(state, st, vtb)
        m_blk = jnp.max(st, axis=0, keepdims=True)
        if state is None:
            m_new = m_blk
        else:
            m_old, l_old, acc_old = state
            m_new = jnp.maximum(m_old, m_blk)
            alpha = jnp.exp2(m_old - m_new)
        p = jnp.exp2(st - m_new)
        l_new = jnp.sum(p, axis=0, keepdims=True)
        acc_new = jnp.dot(vtb, p.astype(BF16), preferred_element_type=F32)
        if state is not None:
            l_new = alpha * l_old + l_new
            acc_new = alpha * acc_old + acc_new
        return m_new, l_new, acc_new

    blocks = []
    if n_lat:
        tkv = vt_ref.shape[3]
        for j in range(n_lat):
            blocks.append((lambda j=j: k_ref[0, 0, j * tkv:(j + 1) * tkv, :],
                           lambda j=j: vt_ref[0, j]))
    ckv = vtc_ref.shape[3]
    for c in range(vtc_ref.shape[1]):
        blocks.append((lambda c=c: kc_ref[0, 0, c * ckv:(c + 1) * ckv, :],
                       lambda c=c: vtc_ref[0, c]))
    state = None
    st = scores(blocks[0][0]())
    for j, (_, load_vt) in enumerate(blocks):
        st_next = scores(blocks[j + 1][0]()) if j + 1 < len(blocks) else None
        state = update(state, st, load_vt())
        st = st_next
    _, l_fin, acc = state

    inv = 1.0 / l_fin
    lv = lam_ref[...]
    lam = (jnp.exp(jnp.sum(lv[0:1] * lv[1:2], axis=-1, keepdims=True))
           - jnp.exp(jnp.sum(lv[2:3] * lv[3:4], axis=-1, keepdims=True)) + lam_init)
    ot = acc[:, :tq] * inv[:, :tq] - lam * (acc[:, tq:] * inv[:, tq:])
    on = ot * lax.rsqrt(jnp.mean(ot * ot, axis=0, keepdims=True) + EPS)
    o_ref[0] = (on.T * (gain_ref[...] * (1.0 - lam_init))).astype(BF16)


def _attn_a_call(lam_vec, gain, q, k, vt, kc, vtc, *, tq, lam_init, stabilize, name):
    b, h, lq, _ = q.shape
    n_lat = 0 if k is None else vt.shape[1]
    in_specs = [
        pl.BlockSpec(lam_vec.shape, lambda b_, h_, i: (0, 0)),
        pl.BlockSpec(gain.shape, lambda b_, h_, i: (0, 0)),
        pl.BlockSpec((1, 1, tq, LANES), lambda b_, h_, i: (b_, h_, i, 0)),
    ]
    args = [lam_vec, gain, q]
    if n_lat:
        in_specs += [
            pl.BlockSpec((1, 1, k.shape[2], LANES), lambda b_, h_, i: (b_, h_, 0, 0)),
            pl.BlockSpec((1, vt.shape[1], LANES, vt.shape[3]), lambda b_, h_, i: (b_, 0, h_, 0)),
        ]
        args += [k, vt]
    in_specs += [
        pl.BlockSpec((1, 1, kc.shape[2], LANES), lambda b_, h_, i: (b_, h_, 0, 0)),
        pl.BlockSpec((1, vtc.shape[1], LANES, vtc.shape[3]), lambda b_, h_, i: (b_, 0, h_, 0)),
    ]
    args += [kc, vtc]
    return pl.pallas_call(
        functools.partial(_attn_a_kernel, n_lat=n_lat, lam_init=lam_init, stabilize=stabilize),
        grid=(b, h, lq // tq),
        in_specs=in_specs,
        out_specs=pl.BlockSpec((1, tq, LANES), lambda b_, h_, i: (b_, i, h_)),
        out_shape=jax.ShapeDtypeStruct((b, lq, h * LANES), BF16),
        compiler_params=pltpu.CompilerParams(
            dimension_semantics=("arbitrary", "arbitrary", "arbitrary"),
            vmem_limit_bytes=VMEM_LIMIT),
        name=name + ("_shifted" if stabilize else ""),
    )(*args)


def _attn_b_kernel(sink_ref, q_ref, kp_ref, km_ref, kn_ref, kc_ref,
                   vp_ref, vm_ref, vn_ref, vc_ref, o_ref, *, stabilize):
    n = pl.program_id(1)
    n_steps = pl.num_programs(1)
    bb = WINDOW_BLOCK
    n_pair = kp_ref.shape[1]
    ncol = 2 * B_GROUP * bb
    half = ncol // 2

    key_j = lax.broadcasted_iota(jnp.int32, (bb, ncol), 0)
    qry_i = lax.broadcasted_iota(jnp.int32, (bb, ncol), 1) & (bb - 1)
    tri_prev = key_j >= qry_i
    tri_next = key_j <= qry_i
    edge_prev = key_j >= qry_i + jnp.where(n > 0, 0, bb)
    edge_next = key_j <= qry_i - jnp.where(n < n_steps - 1, 0, bb)
    lo_rows, hi_rows = slice(0, bb), slice(bb, 2 * bb)

    def chains(p):
        rows = slice(p * LANES, (p + 1) * LANES)
        yield (lo_rows, kp_ref[0, p], km_ref[0, p, lo_rows, :], km_ref[0, p, hi_rows, :],
               vp_ref[0, 0, rows, :], vm_ref[0, 0, rows, :], vm_ref[0, 1, rows, :],
               edge_prev, tri_next)
        yield (hi_rows, km_ref[0, p, lo_rows, :], km_ref[0, p, hi_rows, :], kn_ref[0, p],
               vm_ref[0, 0, rows, :], vm_ref[0, 1, rows, :], vn_ref[0, 0, rows, :],
               tri_prev, edge_next)

    def scores(p, chain):
        q_rows, k_prev, k_own, k_next = chain[:4]
        slabs = [q_ref[0, p * B_GROUP + g, q_rows, :].astype(F32) for g in range(B_GROUP)]
        lane = lax.broadcasted_iota(jnp.int32, slabs[0].shape, 1)
        lo = [jnp.where(lane < HEAD_DIM, s, 0.0).astype(BF16) for s in slabs]
        hi = [jnp.where(lane >= HEAD_DIM, s, 0.0).astype(BF16) for s in slabs]
        qq = jnp.concatenate(lo + hi, axis=0)
        kcat = jnp.concatenate([k_prev, k_own, k_next, kc_ref[0, p]], axis=0)
        return lax.dot_general(kcat, qq, (((1,), (1,)), ((), ())), preferred_element_type=F32)

    work = [(p, chain) for p in range(n_pair) for chain in chains(p)]
    st_next = scores(*work[0])
    for idx, (p, chain) in enumerate(work):
        st = st_next
        st_next = scores(*work[idx + 1]) if idx + 1 < len(work) else None
        q_rows, _, _, _, v_prev, v_own, v_next, in_prev, in_next = chain
        sink = sink_ref[p]
        if stabilize:
            s_prev = jnp.where(in_prev, st[0:bb], NEG_INF)
            s_own = st[bb:2 * bb]
            s_next = jnp.where(in_next, st[2 * bb:3 * bb], NEG_INF)
            s_ctx = st[3 * bb:]
            m = jnp.maximum(jnp.maximum(jnp.max(s_prev, axis=0, keepdims=True),
                                        jnp.max(s_own, axis=0, keepdims=True)),
                            jnp.maximum(jnp.max(s_next, axis=0, keepdims=True),
                                        jnp.max(s_ctx, axis=0, keepdims=True)))
            m = jnp.maximum(m, sink)
            parts = [jnp.exp2(s - m) for s in (s_prev, s_own, s_next, s_ctx)]
            l = jnp.exp2(sink - m)
        else:
            parts = [jnp.where(in_prev, jnp.exp2(st[0:bb]), 0.0),
                     jnp.exp2(st[bb:2 * bb]),
                     jnp.where(in_next, jnp.exp2(st[2 * bb:3 * bb]), 0.0),
                     jnp.exp2(st[3 * bb:])]
            l = jnp.exp2(sink)
        for part in parts:
            l = l + jnp.sum(part, axis=0, keepdims=True)
        pt = jnp.concatenate([part.astype(BF16) for part in parts], axis=0)
        rows = slice(p * LANES, (p + 1) * LANES)
        vcat = jnp.concatenate([v_prev, v_own, v_next]
                               + [vc_ref[0, c, rows, :] for c in range(vc_ref.shape[1])], axis=1)
        inv = 1.0 / l
        for e in range(2):
            ot = jnp.dot(vcat[e * HEAD_DIM:(e + 1) * HEAD_DIM], pt[:, e * half:(e + 1) * half],
                         preferred_element_type=F32) * inv[:, e * half:(e + 1) * half]
            for gp in range(B_GROUP // 2):
                blk = jnp.concatenate([ot[:, (2 * gp) * bb:(2 * gp + 1) * bb],
                                       ot[:, (2 * gp + 1) * bb:(2 * gp + 2) * bb]], axis=0)
                c0 = ((p * 2 + e) * (B_GROUP // 2) + gp) * LANES
                o_ref[0, q_rows, c0:c0 + LANES] = blk.T.astype(BF16)


def _attn_b_call(sink_rows, q, k, vt, kc, vtc, *, stabilize, name):
    b, n_slab, l, _ = q.shape
    n_pair = k.shape[1]
    bb = WINDOW_BLOCK
    nb = l // bb
    assert nb % 2 == 0 and vt.shape[3] == bb
    prev = lambda n: jnp.maximum(2 * n - 1, 0)
    nxt = lambda n: jnp.minimum(2 * n + 2, nb - 1)
    kspec = lambda f: pl.BlockSpec((1, n_pair, bb, LANES), lambda b_, n: (b_, 0, f(n), 0))
    vspec = lambda f: pl.BlockSpec((1, 1, vt.shape[2], bb), lambda b_, n: (b_, f(n), 0, 0))
    whole = lambda a: pl.BlockSpec((1,) + a.shape[1:], lambda b_, n: (b_,) + (0,) * (a.ndim - 1))
    return pl.pallas_call(
        functools.partial(_attn_b_kernel, stabilize=stabilize),
        grid=(b, nb // 2),
        in_specs=[
            pl.BlockSpec(sink_rows.shape, lambda b_, n: (0, 0, 0)),
            pl.BlockSpec((1, n_slab, 2 * bb, LANES), lambda b_, n: (b_, 0, n, 0)),
            kspec(prev),
            pl.BlockSpec((1, n_pair, 2 * bb, LANES), lambda b_, n: (b_, 0, n, 0)),
            kspec(nxt), whole(kc),
            vspec(prev),
            pl.BlockSpec((1, 2, vt.shape[2], bb), lambda b_, n: (b_, n, 0, 0)),
            vspec(nxt), whole(vtc),
        ],
        out_specs=pl.BlockSpec((1, 2 * bb, n_slab * LANES), lambda b_, n: (b_, n, 0)),
        out_shape=jax.ShapeDtypeStruct((b, l, n_slab * LANES), BF16),
        compiler_params=pltpu.CompilerParams(
            dimension_semantics=("arbitrary", "arbitrary"),
            vmem_limit_bytes=VMEM_LIMIT),
        name=name + ("_shifted" if stabilize else ""),
    )(sink_rows, q, k, k, k, kc, vt, vt, vt, vtc)


def _rope_tables(n_tok):
    rows = n_tok // GRID_W
    nf = HEAD_DIM // 4
    inv = ROPE_BASE ** (-jnp.arange(nf, dtype=F32) / nf)
    row = jnp.broadcast_to(jnp.arange(rows, dtype=F32)[:, None], (rows, GRID_W)).reshape(-1)
    col = jnp.broadcast_to(jnp.arange(GRID_W, dtype=F32)[None, :], (rows, GRID_W)).reshape(-1)
    ang = jnp.stack([row[:, None] * inv, col[:, None] * inv], axis=1)
    ang = jnp.stack([ang, ang], axis=2).reshape(n_tok, HEAD_DIM)
    cos = jnp.tile(jnp.cos(ang), (1, LANES // HEAD_DIM))
    sin = jnp.tile(jnp.sin(ang), (1, LANES // HEAD_DIM))
    first_half = (jnp.arange(LANES) % (2 * nf)) < nf
    return cos, jnp.where(first_half, -sin, 0.0), jnp.where(first_half, 0.0, sin)


def _group_sum_matrix():
    idx = jnp.arange(MXU_COLS) // HEAD_DIM
    return (idx[:, None] == idx[None, :]).astype(BF16)


def _score_bound(gq_row, gk_row):
    return 1.05 * HEAD_DIM * jnp.max(jnp.abs(gq_row)) * jnp.max(jnp.abs(gk_row))


def _pair_q_heads(w_qkv):
    order = []
    for p in range(B_KV_HEADS // 2):
        for g in range(B_GROUP):
            order += [(2 * p) * B_GROUP + g, (2 * p + 1) * B_GROUP + g]
    d = w_qkv.shape[0]
    wq = w_qkv[:, :B_Q_HEADS * HEAD_DIM].reshape(d, B_Q_HEADS, HEAD_DIM)
    wq = wq[:, jnp.array(order)].reshape(d, B_Q_HEADS * HEAD_DIM)
    return jnp.concatenate([wq, w_qkv[:, B_Q_HEADS * HEAD_DIM:]], axis=1)


def kernel(x, c, ctx, c_ctx, ada_w, ada_b, ffn_pre_wi, ffn_pre_wo, ffn_post_wi, ffn_post_wo,
           a_w_qkv, a_w_o, a_q_gain, a_k_gain, a_lambda, a_subln_gain,
           b_w_qkv, b_w_o, b_q_gain, b_k_gain, b_sink):
    bsz, seq, d = x.shape
    n_ctx = ctx.shape[1]
    depth = ada_w.shape[0]
    assert depth == 2 and seq % WINDOW_BLOCK == 0

    rows = 8
    cs = jnp.zeros((rows, d), F32).at[:bsz].set(c).at[bsz].set(c_ctx)
    mods = _ada(cs, ada_w, ada_b)
    mod_l = [mods[i, :bsz].reshape(bsz, N_MOD, d) for i in range(depth)]
    mod_c = [mods[i, bsz:bsz + 1].reshape(1, N_MOD, d) for i in range(depth)]

    rope_tabs = _rope_tables(seq)
    bd = _group_sum_matrix()
    qk_scale = HEAD_DIM ** -0.5 * LOG2E
    tile2 = lambda g: jnp.tile(g.astype(F32), LANES // HEAD_DIM).reshape(1, LANES)

    pre_wi = _to_bf16(ffn_pre_wi, name="cast_pre_wi")
    pre_wo = _to_bf16(ffn_pre_wo, name="cast_pre_wo")
    post_wi = _to_bf16(ffn_post_wi, name="cast_post_wi")
    post_wo = _to_bf16(ffn_post_wo, name="cast_post_wo")

    tm = 256
    tm_post = 512
    tkv = 512
    tq = 1024

    lam_init = 0.8 - 0.6 * math.exp(-0.3 * 0)
    a_cfg = dict(n_q=A_HEADS, n_k=A_HEADS, v_cols=A_HEADS * LANES)
    wqkv_a = a_w_qkv[0].astype(BF16)
    gq_a, gk_a = tile2(a_q_gain[0]) * qk_scale, tile2(a_k_gain[0])
    x1, q, k, vt = _pre_call(x, mod_l[0], pre_wi, pre_wo, wqkv_a, bd, gq_a, gk_a, rope_tabs,
                             layer=0, tm=tm, vt_chunk=tkv, name="pre0_lat", **a_cfg)
    xc1, qc, kc, vtc = _pre_call(ctx, mod_c[0], pre_wi, pre_wo, wqkv_a, bd, gq_a, gk_a, None,
                                 layer=0, tm=n_ctx, vt_chunk=n_ctx, name="pre0_ctx", **a_cfg)
    sub_gain = a_subln_gain[0].astype(F32).reshape(1, LANES)
    lam_vec = a_lambda[0].astype(F32)
    unshifted_ok = _score_bound(gq_a, gk_a) <= UNSHIFTED_SCORE_LIMIT

    def attn_a(stabilize):
        def run(q, k, vt, qc, kc, vtc):
            o = _attn_a_call(lam_vec, sub_gain, q, k, vt, kc, vtc, tq=tq, lam_init=lam_init,
                             stabilize=stabilize, name="attn_a_lat")
            oc = _attn_a_call(lam_vec, sub_gain, qc, None, None, kc, vtc, tq=n_ctx,
                              lam_init=lam_init, stabilize=stabilize, name="attn_a_ctx")
            return o, oc
        return run

    o, oc = lax.cond(unshifted_ok, attn_a(False), attn_a(True), q, k, vt, qc, kc, vtc)
    wout_a = a_w_o[0].astype(BF16)
    x2 = _post_call(x1, o, mod_l[0], wout_a, post_wi, post_wo, layer=0, tm=tm_post,
                    name="post0_lat")
    xc2 = _post_call(xc1, oc, mod_c[0], wout_a, post_wi, post_wo, layer=0, tm=n_ctx,
                     name="post0_ctx")

    b_cfg = dict(n_q=B_Q_HEADS // 2, n_k=B_KV_HEADS // 2, v_cols=B_KV_HEADS * HEAD_DIM)
    wqkv_b = _pair_q_heads(b_w_qkv[0]).astype(BF16)
    gq_b, gk_b = tile2(b_q_gain[0]) * qk_scale, tile2(b_k_gain[0])
    x3, q, k, vt = _pre_call(x2, mod_l[1], pre_wi, pre_wo, wqkv_b, bd, gq_b, gk_b, rope_tabs,
                             layer=1, tm=tm, vt_chunk=WINDOW_BLOCK, name="pre1_lat", **b_cfg)
    _, _, kc, vtc = _pre_call(xc2, mod_c[1], pre_wi, pre_wo, wqkv_b, bd, gq_b, gk_b, None,
                              layer=1, tm=n_ctx, vt_chunk=WINDOW_BLOCK, name="pre1_ctx", **b_cfg)
    sink = (b_sink[0].astype(F32) * LOG2E).reshape(B_KV_HEADS // 2, 2 * B_GROUP, 1)
    sink_rows = jnp.broadcast_to(sink, (B_KV_HEADS // 2, 2 * B_GROUP, WINDOW_BLOCK))
    sink_rows = sink_rows.reshape(B_KV_HEADS // 2, 1, 2 * B_GROUP * WINDOW_BLOCK)
    logit_bound = jnp.maximum(_score_bound(gq_b, gk_b), jnp.max(jnp.abs(sink_rows)))
    o = lax.cond(
        logit_bound <= UNSHIFTED_SCORE_LIMIT,
        functools.partial(_attn_b_call, stabilize=False, name="attn_b"),
        functools.partial(_attn_b_call, stabilize=True, name="attn_b"),
        sink_rows, q, k, vt, kc, vtc)
    x4 = _post_call(x3, o, mod_l[1], b_w_o[0].astype(BF16), post_wi, post_wo, layer=1,
                    tm=tm_post, name="post1_lat")
    return x4
```

```python
import functools
import math

import jax
import jax.numpy as jnp
import numpy as np
from jax import lax
from jax.experimental import pallas as pl
from jax.experimental.pallas import tpu as pltpu

F32 = jnp.float32
BF16 = jnp.bfloat16

LANES = 128
MXU_COLS = 256
HEAD_DIM = 64
N_MOD = 9
EPS = 1e-6
ROPE_BASE = 10000.0
GRID_W = 64
WINDOW_BLOCK = 128
NEG_INF = -1e30
LOG2E = math.log2(math.e)
UNSHIFTED_SCORE_LIMIT = 96.0
VMEM_LIMIT = 56 * 1024 * 1024
CAST_BLOCK_BYTES = 6 * 1024 * 1024

A_HEADS = 8
B_Q_HEADS = 16
B_KV_HEADS = 4
B_GROUP = B_Q_HEADS // B_KV_HEADS


def _resident(shape):
    nd = len(shape)
    return pl.BlockSpec(shape, lambda *_: (0,) * nd, pipeline_mode=pl.Buffered(1))


def _resident_layer(stack, layer):
    _, rows, cols = stack.shape
    return pl.BlockSpec((None, rows, cols), lambda *_: (layer, 0, 0),
                        pipeline_mode=pl.Buffered(1))


def _cast_kernel(w_ref, o_ref):
    o_ref[...] = w_ref[...].astype(BF16)


def _to_bf16(w, *, name):
    n, rows, cols = w.shape
    rb = max(r for r in range(8, rows + 1, 8)
             if rows % r == 0 and r * cols * 4 <= CAST_BLOCK_BYTES)
    spec = pl.BlockSpec((1, rb, cols), lambda i, r: (i, r, 0))
    return pl.pallas_call(
        _cast_kernel, grid=(n, rows // rb), in_specs=[spec], out_specs=spec,
        out_shape=jax.ShapeDtypeStruct(w.shape, BF16),
        compiler_params=pltpu.CompilerParams(
            dimension_semantics=("arbitrary", "arbitrary"),
            vmem_limit_bytes=VMEM_LIMIT),
        name=name,
    )(w)


def _ada_kernel(cs_ref, w_ref, b_ref, out_ref):
    s = cs_ref[...]
    s = s * jax.nn.sigmoid(s)
    out_ref[0] = jnp.dot(s.astype(BF16), w_ref[0].astype(BF16),
                         preferred_element_type=F32) + b_ref[0]


def _ada(cs, ada_w, ada_b):
    depth, d, n = ada_w.shape
    rows = cs.shape[0]
    tn = n // 8
    return pl.pallas_call(
        _ada_kernel,
        grid=(depth, n // tn),
        in_specs=[
            pl.BlockSpec((rows, d), lambda i, j: (0, 0)),
            pl.BlockSpec((1, d, tn), lambda i, j: (i, 0, j)),
            pl.BlockSpec((1, 1, tn), lambda i, j: (i, 0, j)),
        ],
        out_specs=pl.BlockSpec((1, rows, tn), lambda i, j: (i, 0, j)),
        out_shape=jax.ShapeDtypeStruct((depth, rows, n), F32),
        compiler_params=pltpu.CompilerParams(
            dimension_semantics=("arbitrary", "arbitrary"),
            vmem_limit_bytes=VMEM_LIMIT),
        name="ada_mod",
    )(cs, ada_w, ada_b.reshape(depth, 1, n))


def _rms(x):
    return x * lax.rsqrt(jnp.mean(x * x, axis=-1, keepdims=True) + EPS)


def _modulated(x, shift, scale):
    return (_rms(x) * (1.0 + scale) + shift).astype(BF16)


def _ffn_half_step(x, shift, scale, gate, wi_ref, wo_ref):
    d_ff = wo_ref.shape[0]
    h = _modulated(x, shift, scale)
    gu = jnp.dot(h, wi_ref[...], preferred_element_type=F32)
    g = gu[:, :d_ff]
    u = gu[:, d_ff:]
    act = (g * jax.nn.sigmoid(g) * u).astype(BF16)
    ff = jnp.dot(act, wo_ref[...], preferred_element_type=F32)
    return x + (0.5 * gate) * ff


def _pre_kernel(*refs, n_q, n_k, v_cols, rope):
    if rope:
        (x_ref, mod_ref, wi_ref, wo_ref, wqkv_ref, bd_ref, gq_ref, gk_ref,
         cos_ref, sa_ref, sb_ref, x_out, q_out, k_out, vt_out, qkv_sc) = refs
    else:
        (x_ref, mod_ref, wi_ref, wo_ref, wqkv_ref, bd_ref, gq_ref, gk_ref,
         x_out, q_out, k_out, vt_out, qkv_sc) = refs

    step = pl.program_id(0)
    last = pl.num_programs(0) - 1

    @pl.when(step == 0)
    def _():
        qkv_sc[...] = jnp.zeros(qkv_sc.shape, F32)

    def finish_projection():
        bd = bd_ref[...]
        if rope:
            cos, sa, sb = cos_ref[...], sa_ref[...], sb_ref[...]

        def norm_rope_store(col0, n_slabs, gain, out):
            for s in range(0, n_slabs, 2):
                y = qkv_sc[:, col0 + s * LANES: col0 + (s + 2) * LANES]
                ss = jnp.dot((y * y).astype(BF16), bd, preferred_element_type=F32)
                y = y * lax.rsqrt(ss * (1.0 / HEAD_DIM) + EPS)
                for t in range(2):
                    z = y[:, t * LANES:(t + 1) * LANES] * gain
                    if rope:
                        z = (z * cos + pltpu.roll(z, LANES - 16, 1) * sa
                             + pltpu.roll(z, 16, 1) * sb)
                    out[0, s + t] = z.astype(BF16)

        norm_rope_store(0, n_q, gq_ref[...], q_out)
        norm_rope_store(n_q * LANES, n_k, gk_ref[...], k_out)

        v0 = (n_q + n_k) * LANES
        vt = qkv_sc[:, v0:v0 + v_cols].T.astype(BF16)
        chunk = vt_out.shape[3]
        for c in range(vt_out.shape[1]):
            vt_out[0, c] = vt[:, c * chunk:(c + 1) * chunk]

    def tile_matmuls():
        mod = mod_ref[0]
        x1 = _ffn_half_step(x_ref[0], mod[0:1], mod[1:2], mod[2:3], wi_ref, wo_ref)
        x_out[0] = x1
        h = _modulated(x1, mod[3:4], mod[4:5])
        qkv_sc[...] = jnp.dot(h, wqkv_ref[...], preferred_element_type=F32)

    @pl.when(step < last)
    def _():
        finish_projection()
        tile_matmuls()

    @pl.when(step == last)
    def _():
        finish_projection()


def _post_kernel(x_ref, o_ref, mod_ref, wout_ref, wi_ref, wo_ref, x_out):
    mod = mod_ref[0]
    attn = jnp.dot(o_ref[0], wout_ref[...], preferred_element_type=F32)
    x2 = x_ref[0] + mod[5:6] * attn
    x_out[0] = _ffn_half_step(x2, mod[6:7], mod[7:8], mod[8:9], wi_ref, wo_ref)


def _mod_spec(mods, layer, mod_row, wrap):
    _, _, n_mod, d = mods.shape
    return pl.BlockSpec((None, 1, n_mod, d),
                        wrap(lambda b, i: (layer, b if mod_row is None else mod_row, 0, 0)))


def _pre_call(x, mods, wi, wo, wqkv, bd, gq, gk, rope_tabs, *, layer, mod_row, n_q, n_k, v_cols,
              tm, vt_chunk, name):
    bx, lx, d = x.shape
    rope = rope_tabs is not None
    nt = lx // tm
    n_tiles = bx * nt
    grid = (n_tiles + 1,)

    def cur(g):
        t = jnp.minimum(g, n_tiles - 1)
        return t // nt, t % nt

    def lag(g):
        t = jnp.maximum(g - 1, 0)
        return t // nt, t % nt

    def at_cur(f):
        return lambda g: f(*cur(g))

    def at_lag(f):
        return lambda g: f(*lag(g))

    in_specs = [
        pl.BlockSpec((1, tm, d), at_cur(lambda b, i: (b, i, 0))),
        _mod_spec(mods, layer, mod_row, at_cur),
        _resident_layer(wi, layer), _resident_layer(wo, layer), _resident(wqkv.shape),
        _resident(bd.shape), _resident(gq.shape), _resident(gk.shape),
    ]
    args = [x, mods, wi, wo, wqkv, bd, gq, gk]
    if rope:
        in_specs += [pl.BlockSpec((tm, LANES), at_lag(lambda b, i: (i, 0)))] * 3
        args += list(rope_tabs)
    if tm >= vt_chunk:
        vt_block = (1, tm // vt_chunk, v_cols, vt_chunk)
        vt_map = at_lag(lambda b, i: (b, i, 0, 0))
    else:
        per = vt_chunk // tm
        vt_block = (1, 1, v_cols, tm)
        vt_map = at_lag(lambda b, i: (b, i // per, 0, i % per))
    out_specs = [
        pl.BlockSpec((1, tm, d), at_cur(lambda b, i: (b, i, 0))),
        pl.BlockSpec((1, n_q, tm, LANES), at_lag(lambda b, i: (b, 0, i, 0))),
        pl.BlockSpec((1, n_k, tm, LANES), at_lag(lambda b, i: (b, 0, i, 0))),
        pl.BlockSpec(vt_block, vt_map),
    ]
    out_shape = [
        jax.ShapeDtypeStruct((bx, lx, d), F32),
        jax.ShapeDtypeStruct((bx, n_q, lx, LANES), BF16),
        jax.ShapeDtypeStruct((bx, n_k, lx, LANES), BF16),
        jax.ShapeDtypeStruct((bx, lx // vt_chunk, v_cols, vt_chunk), BF16),
    ]
    return pl.pallas_call(
        functools.partial(_pre_kernel, n_q=n_q, n_k=n_k, v_cols=v_cols, rope=rope),
        grid=grid, in_specs=in_specs, out_specs=out_specs, out_shape=out_shape,
        scratch_shapes=[pltpu.VMEM((tm, wqkv.shape[1]), F32)],
        compiler_params=pltpu.CompilerParams(
            dimension_semantics=("arbitrary",),
            vmem_limit_bytes=VMEM_LIMIT),
        name=name,
    )(*args)


def _post_call(x, o, mods, wout, wi, wo, *, layer, mod_row, tm, name):
    bx, lx, d = x.shape
    return pl.pallas_call(
        _post_kernel,
        grid=(bx, lx // tm),
        in_specs=[
            pl.BlockSpec((1, tm, d), lambda b, i: (b, i, 0)),
            pl.BlockSpec((1, tm, o.shape[2]), lambda b, i: (b, i, 0)),
            _mod_spec(mods, layer, mod_row, lambda f: f),
            _resident(wout.shape), _resident_layer(wi, layer), _resident_layer(wo, layer),
        ],
        out_specs=pl.BlockSpec((1, tm, d), lambda b, i: (b, i, 0)),
        out_shape=jax.ShapeDtypeStruct((bx, lx, d), F32),
        compiler_params=pltpu.CompilerParams(
            dimension_semantics=("arbitrary", "arbitrary"),
            vmem_limit_bytes=VMEM_LIMIT),
        name=name,
    )(x, o, mods, wout, wi, wo)


def _split_components(q_tile):
    qf = q_tile.astype(F32)
    lane = lax.broadcasted_iota(jnp.int32, qf.shape, 1)
    lo = jnp.where(lane < HEAD_DIM, qf, 0.0).astype(BF16)
    hi = jnp.where(lane >= HEAD_DIM, qf, 0.0).astype(BF16)
    return jnp.concatenate([lo, hi], axis=0)


def _attn_a_kernel(*refs, n_lat, n_cast, lam_init, stabilize):
    n_in = len(refs) - 1 - n_cast
    cast_in, cast_out = refs[n_in - n_cast:n_in], refs[n_in + 1:]
    for w_in, w_out in zip(cast_in, cast_out):
        w_out[...] = w_in[...].astype(BF16)
    refs = refs[:n_in - n_cast] + (refs[n_in],)
    if n_lat:
        lam_ref, gain_ref, q_ref, k_ref, vt_ref, kc_ref, vtc_ref, o_ref = refs
    else:
        lam_ref, gain_ref, q_ref, kc_ref, vtc_ref, o_ref = refs
    tq = q_ref.shape[2]
    qq = _split_components(q_ref[0, 0])

    def scores(kb):
        return lax.dot_general(kb, qq, (((1,), (1,)), ((), ())), preferred_element_type=F32)

    def update_unshifted(state, st, vtb):
        p = jnp.exp2(st)
        l_new = jnp.sum(p, axis=0, keepdims=True)
        acc_new = jnp.dot(vtb, p.astype(BF16), preferred_element_type=F32)
        if state is not None:
            l_new = state[1] + l_new
            acc_new = state[2] + acc_new
        return None, l_new, acc_new

    def update(state, st, vtb):
        if not stabilize:
            return update_unshifted(state, st, vtb)
        m_blk = jnp.max(st, axis=0, keepdims=True)
        if state is None:
            m_new = m_blk
        else:
            m_old, l_old, acc_old = state
            m_new = jnp.maximum(m_old, m_blk)
            alpha = jnp.exp2(m_old - m_new)
        p = jnp.exp2(st - m_new)
        l_new = jnp.sum(p, axis=0, keepdims=True)
        acc_new = jnp.dot(vtb, p.astype(BF16), preferred_element_type=F32)
        if state is not None:
            l_new = alpha * l_old + l_new
            acc_new = alpha * acc_old + acc_new
        return m_new, l_new, acc_new

    blocks = []
    if n_lat:
        tkv = vt_ref.shape[3]
        for j in range(n_lat):
            blocks.append((lambda j=j: k_ref[0, 0, j * tkv:(j + 1) * tkv, :],
                           lambda j=j: vt_ref[0, j]))
    ckv = vtc_ref.shape[3]
    for c in range(vtc_ref.shape[1]):
        blocks.append((lambda c=c: kc_ref[0, 0, c * ckv:(c + 1) * ckv, :],
                       lambda c=c: vtc_ref[0, c]))
    state = None
    st = scores(blocks[0][0]())
    for j, (_, load_vt) in enumerate(blocks):
        st_next = scores(blocks[j + 1][0]()) if j + 1 < len(blocks) else None
        state = update(state, st, load_vt())
        st = st_next
    _, l_fin, acc = state

    inv = 1.0 / l_fin
    lv = lam_ref[...]
    lam = (jnp.exp(jnp.sum(lv[0:1] * lv[1:2], axis=-1, keepdims=True))
           - jnp.exp(jnp.sum(lv[2:3] * lv[3:4], axis=-1, keepdims=True)) + lam_init)
    ot = acc[:, :tq] * inv[:, :tq] - lam * (acc[:, tq:] * inv[:, tq:])
    on = ot * lax.rsqrt(jnp.mean(ot * ot, axis=0, keepdims=True) + EPS)
    o_ref[0] = (on.T * (gain_ref[...] * (1.0 - lam_init))).astype(BF16)


def _cast_views(stacks, n_steps):
    views = []
    for w in stacks:
        rows = 16 * n_steps
        assert w.size % (rows * LANES) == 0
        views.append(w.reshape(rows, w.size // rows))
    return views


def _attn_a_call(lam_vec, gain, q, k, vt, kc, vtc, *cast_views, tq, lam_init, stabilize, name):
    b, h, lq, _ = q.shape
    n_lat = 0 if k is None else vt.shape[1]
    nq = lq // tq
    in_specs = [
        pl.BlockSpec(lam_vec.shape, lambda b_, h_, i: (0, 0)),
        pl.BlockSpec(gain.shape, lambda b_, h_, i: (0, 0)),
        pl.BlockSpec((1, 1, tq, LANES), lambda b_, h_, i: (b_, h_, i, 0)),
    ]
    args = [lam_vec, gain, q]
    if n_lat:
        in_specs += [
            pl.BlockSpec((1, 1, k.shape[2], LANES), lambda b_, h_, i: (b_, h_, 0, 0)),
            pl.BlockSpec((1, vt.shape[1], LANES, vt.shape[3]), lambda b_, h_, i: (b_, 0, h_, 0)),
        ]
        args += [k, vt]
    in_specs += [
        pl.BlockSpec((1, 1, kc.shape[2], LANES), lambda b_, h_, i: (b_, h_, 0, 0)),
        pl.BlockSpec((1, vtc.shape[1], LANES, vtc.shape[3]), lambda b_, h_, i: (b_, 0, h_, 0)),
    ]
    args += [kc, vtc]
    cast_specs = [pl.BlockSpec((16, w.shape[1]), lambda b_, h_, i: ((b_ * h + h_) * nq + i, 0))
                  for w in cast_views]
    outs = pl.pallas_call(
        functools.partial(_attn_a_kernel, n_lat=n_lat, n_cast=len(cast_views),
                          lam_init=lam_init, stabilize=stabilize),
        grid=(b, h, nq),
        in_specs=in_specs + cast_specs,
        out_specs=[pl.BlockSpec((1, tq, LANES), lambda b_, h_, i: (b_, i, h_))] + cast_specs,
        out_shape=[jax.ShapeDtypeStruct((b, lq, h * LANES), BF16)]
        + [jax.ShapeDtypeStruct(w.shape, BF16) for w in cast_views],
        compiler_params=pltpu.CompilerParams(
            dimension_semantics=("arbitrary", "arbitrary", "arbitrary"),
            vmem_limit_bytes=VMEM_LIMIT),
        name=name + ("_shifted" if stabilize else ""),
    )(*args, *cast_views)
    return outs if cast_views else outs[0]


def _attn_b_kernel(sink_ref, q_ref, kp_ref, km_ref, kn_ref, kc_ref,
                   vp_ref, vm_ref, vn_ref, vc_ref, o_ref, *, stabilize):
    n = pl.program_id(1)
    n_steps = pl.num_programs(1)
    bb = WINDOW_BLOCK
    n_pair = kp_ref.shape[1]
    ncol = 2 * B_GROUP * bb
    half = ncol // 2

    key_j = lax.broadcasted_iota(jnp.int32, (bb, ncol), 0)
    qry_i = lax.broadcasted_iota(jnp.int32, (bb, ncol), 1) & (bb - 1)
    tri_prev = key_j >= qry_i
    tri_next = key_j <= qry_i
    edge_prev = key_j >= qry_i + jnp.where(n > 0, 0, bb)
    edge_next = key_j <= qry_i - jnp.where(n < n_steps - 1, 0, bb)
    lo_rows, hi_rows = slice(0, bb), slice(bb, 2 * bb)

    def chains(p):
        rows = slice(p * LANES, (p + 1) * LANES)
        yield (lo_rows, kp_ref[0, p], km_ref[0, p, lo_rows, :], km_ref[0, p, hi_rows, :],
               vp_ref[0, 0, rows, :], vm_ref[0, 0, rows, :], vm_ref[0, 1, rows, :],
               edge_prev, tri_next)
        yield (hi_rows, km_ref[0, p, lo_rows, :], km_ref[0, p, hi_rows, :], kn_ref[0, p],
               vm_ref[0, 0, rows, :], vm_ref[0, 1, rows, :], vn_ref[0, 0, rows, :],
               tri_prev, edge_next)

    def scores(p, chain):
        q_rows, k_prev, k_own, k_next = chain[:4]
        slabs = [q_ref[0, p * B_GROUP + g, q_rows, :].astype(F32) for g in range(B_GROUP)]
        lane = lax.broadcasted_iota(jnp.int32, slabs[0].shape, 1)
        lo = [jnp.where(lane < HEAD_DIM, s, 0.0).astype(BF16) for s in slabs]
        hi = [jnp.where(lane >= HEAD_DIM, s, 0.0).astype(BF16) for s in slabs]
        qq = jnp.concatenate(lo + hi, axis=0)
        kcat = jnp.concatenate([k_prev, k_own, k_next, kc_ref[0, p]], axis=0)
        return lax.dot_general(kcat, qq, (((1,), (1,)), ((), ())), preferred_element_type=F32)

    work = [(p, chain) for p in range(n_pair) for chain in chains(p)]
    st_next = scores(*work[0])
    for idx, (p, chain) in enumerate(work):
        st = st_next
        st_next = scores(*work[idx + 1]) if idx + 1 < len(work) else None
        q_rows, _, _, _, v_prev, v_own, v_next, in_prev, in_next = chain
        sink = sink_ref[p]
        if stabilize:
            s_prev = jnp.where(in_prev, st[0:bb], NEG_INF)
            s_own = st[bb:2 * bb]
            s_next = jnp.where(in_next, st[2 * bb:3 * bb], NEG_INF)
            s_ctx = st[3 * bb:]
            m = jnp.maximum(jnp.maximum(jnp.max(s_prev, axis=0, keepdims=True),
                                        jnp.max(s_own, axis=0, keepdims=True)),
                            jnp.maximum(jnp.max(s_next, axis=0, keepdims=True),
                                        jnp.max(s_ctx, axis=0, keepdims=True)))
            m = jnp.maximum(m, sink)
            parts = [jnp.exp2(s - m) for s in (s_prev, s_own, s_next, s_ctx)]
            l = jnp.exp2(sink - m)
        else:
            parts = [jnp.where(in_prev, jnp.exp2(st[0:bb]), 0.0),
                     jnp.exp2(st[bb:2 * bb]),
                     jnp.where(in_next, jnp.exp2(st[2 * bb:3 * bb]), 0.0),
                     jnp.exp2(st[3 * bb:])]
            l = jnp.exp2(sink)
        for part in parts:
            l = l + jnp.sum(part, axis=0, keepdims=True)
        pt = jnp.concatenate([part.astype(BF16) for part in parts], axis=0)
        rows = slice(p * LANES, (p + 1) * LANES)
        vcat = jnp.concatenate([v_prev, v_own, v_next]
                               + [vc_ref[0, c, rows, :] for c in range(vc_ref.shape[1])], axis=1)
        inv = 1.0 / l
        for e in range(2):
            ot = jnp.dot(vcat[e * HEAD_DIM:(e + 1) * HEAD_DIM], pt[:, e * half:(e + 1) * half],
                         preferred_element_type=F32) * inv[:, e * half:(e + 1) * half]
            for gp in range(B_GROUP // 2):
                blk = jnp.concatenate([ot[:, (2 * gp) * bb:(2 * gp + 1) * bb],
                                       ot[:, (2 * gp + 1) * bb:(2 * gp + 2) * bb]], axis=0)
                c0 = ((p * 2 + e) * (B_GROUP // 2) + gp) * LANES
                o_ref[0, q_rows, c0:c0 + LANES] = blk.T.astype(BF16)


def _attn_b_call(sink_rows, q, k, vt, kc, vtc, *, stabilize, name):
    b, n_slab, l, _ = q.shape
    n_pair = k.shape[1]
    bb = WINDOW_BLOCK
    nb = l // bb
    assert nb % 2 == 0 and vt.shape[3] == bb
    prev = lambda n: jnp.maximum(2 * n - 1, 0)
    nxt = lambda n: jnp.minimum(2 * n + 2, nb - 1)
    kspec = lambda f: pl.BlockSpec((1, n_pair, bb, LANES), lambda b_, n: (b_, 0, f(n), 0))
    vspec = lambda f: pl.BlockSpec((1, 1, vt.shape[2], bb), lambda b_, n: (b_, f(n), 0, 0))
    whole = lambda a: pl.BlockSpec((1,) + a.shape[1:], lambda b_, n: (b_,) + (0,) * (a.ndim - 1))
    return pl.pallas_call(
        functools.partial(_attn_b_kernel, stabilize=stabilize),
        grid=(b, nb // 2),
        in_specs=[
            pl.BlockSpec(sink_rows.shape, lambda b_, n: (0, 0, 0)),
            pl.BlockSpec((1, n_slab, 2 * bb, LANES), lambda b_, n: (b_, 0, n, 0)),
            kspec(prev),
            pl.BlockSpec((1, n_pair, 2 * bb, LANES), lambda b_, n: (b_, 0, n, 0)),
            kspec(nxt), whole(kc),
            vspec(prev),
            pl.BlockSpec((1, 2, vt.shape[2], bb), lambda b_, n: (b_, n, 0, 0)),
            vspec(nxt), whole(vtc),
        ],
        out_specs=pl.BlockSpec((1, 2 * bb, n_slab * LANES), lambda b_, n: (b_, n, 0)),
        out_shape=jax.ShapeDtypeStruct((b, l, n_slab * LANES), BF16),
        compiler_params=pltpu.CompilerParams(
            dimension_semantics=("arbitrary", "arbitrary"),
            vmem_limit_bytes=VMEM_LIMIT),
        name=name + ("_shifted" if stabilize else ""),
    )(sink_rows, q, k, k, k, kc, vt, vt, vt, vtc)


def _rope_tables(n_tok):
    rows = n_tok // GRID_W
    nf = HEAD_DIM // 4
    inv = (np.float32(ROPE_BASE) ** (-np.arange(nf, dtype=np.float32) / np.float32(nf)))
    row = np.repeat(np.arange(rows, dtype=np.float32), GRID_W)
    col = np.tile(np.arange(GRID_W, dtype=np.float32), rows)
    ang = np.stack([row[:, None] * inv, col[:, None] * inv], axis=1)
    ang = np.stack([ang, ang], axis=2).reshape(n_tok, HEAD_DIM).astype(np.float32)
    cos = np.tile(np.cos(ang), (1, LANES // HEAD_DIM)).astype(np.float32)
    sin = np.tile(np.sin(ang), (1, LANES // HEAD_DIM)).astype(np.float32)
    first_half = (np.arange(LANES) % (2 * nf)) < nf
    zero = np.float32(0.0)
    return (jnp.asarray(cos), jnp.asarray(np.where(first_half, -sin, zero)),
            jnp.asarray(np.where(first_half, zero, sin)))


def _group_sum_matrix():
    idx = np.arange(MXU_COLS) // HEAD_DIM
    return jnp.asarray(idx[:, None] == idx[None, :], dtype=BF16)


def _score_bound(gq_row, gk_row):
    return 1.05 * HEAD_DIM * jnp.max(jnp.abs(gq_row)) * jnp.max(jnp.abs(gk_row))


def _pair_q_heads(w_qkv):
    order = []
    for p in range(B_KV_HEADS // 2):
        for g in range(B_GROUP):
            order += [(2 * p) * B_GROUP + g, (2 * p + 1) * B_GROUP + g]
    d = w_qkv.shape[0]
    wq = w_qkv[:, :B_Q_HEADS * HEAD_DIM].reshape(d, B_Q_HEADS, HEAD_DIM)
    wq = wq[:, jnp.array(order)].reshape(d, B_Q_HEADS * HEAD_DIM)
    return jnp.concatenate([wq, w_qkv[:, B_Q_HEADS * HEAD_DIM:]], axis=1)


def kernel(x, c, ctx, c_ctx, ada_w, ada_b, ffn_pre_wi, ffn_pre_wo, ffn_post_wi, ffn_post_wo,
           a_w_qkv, a_w_o, a_q_gain, a_k_gain, a_lambda, a_subln_gain,
           b_w_qkv, b_w_o, b_q_gain, b_k_gain, b_sink):
    bsz, seq, d = x.shape
    n_ctx = ctx.shape[1]
    depth = ada_w.shape[0]
    assert depth == 2 and seq % WINDOW_BLOCK == 0

    rows = 8
    cs = jnp.concatenate([c, c_ctx[None], jnp.zeros((rows - bsz - 1, d), F32)], axis=0)
    mods = _ada(cs, ada_w, ada_b).reshape(depth, rows, N_MOD, d)
    lat, ctx_row = None, bsz

    rope_tabs = _rope_tables(seq)
    bd = _group_sum_matrix()
    qk_scale = HEAD_DIM ** -0.5 * LOG2E
    tile2 = lambda g: jnp.tile(g.astype(F32), LANES // HEAD_DIM).reshape(1, LANES)

    pre_wi = _to_bf16(ffn_pre_wi, name="cast_pre_wi")
    pre_wo = _to_bf16(ffn_pre_wo, name="cast_pre_wo")

    tm = 256
    tm_post = 512
    tkv = 512
    tq = 1024

    lam_init = 0.8 - 0.6 * math.exp(-0.3 * 0)
    a_cfg = dict(n_q=A_HEADS, n_k=A_HEADS, v_cols=A_HEADS * LANES)
    wqkv_a = a_w_qkv[0].astype(BF16)
    gq_a, gk_a = tile2(a_q_gain[0]) * qk_scale, tile2(a_k_gain[0])
    x1, q, k, vt = _pre_call(x, mods, pre_wi, pre_wo, wqkv_a, bd, gq_a, gk_a, rope_tabs,
                             layer=0, mod_row=lat, tm=tm, vt_chunk=tkv, name="pre0_lat", **a_cfg)
    xc1, qc, kc, vtc = _pre_call(ctx, mods, pre_wi, pre_wo, wqkv_a, bd, gq_a, gk_a, None,
                                 layer=0, mod_row=ctx_row, tm=n_ctx, vt_chunk=n_ctx,
                                 name="pre0_ctx", **a_cfg)
    sub_gain = a_subln_gain[0].astype(F32).reshape(1, LANES)
    lam_vec = a_lambda[0].astype(F32)
    unshifted_ok = _score_bound(gq_a, gk_a) <= UNSHIFTED_SCORE_LIMIT

    post_views = _cast_views([ffn_post_wi, ffn_post_wo], bsz * A_HEADS * (seq // tq))

    def attn_a(stabilize):
        def run(q, k, vt, qc, kc, vtc, *views):
            o, *cast = _attn_a_call(lam_vec, sub_gain, q, k, vt, kc, vtc, *views, tq=tq,
                                    lam_init=lam_init, stabilize=stabilize, name="attn_a_lat")
            oc = _attn_a_call(lam_vec, sub_gain, qc, None, None, kc, vtc, tq=n_ctx,
                              lam_init=lam_init, stabilize=stabilize, name="attn_a_ctx")
            return (o, oc, *cast)
        return run

    o, oc, post_wi, post_wo = lax.cond(unshifted_ok, attn_a(False), attn_a(True),
                                       q, k, vt, qc, kc, vtc, *post_views)
    post_wi = post_wi.reshape(ffn_post_wi.shape)
    post_wo = post_wo.reshape(ffn_post_wo.shape)
    wout_a = a_w_o[0].astype(BF16)
    x2 = _post_call(x1, o, mods, wout_a, post_wi, post_wo, layer=0, mod_row=lat, tm=tm_post,
                    name="post0_lat")
    xc2 = _post_call(xc1, oc, mods, wout_a, post_wi, post_wo, layer=0, mod_row=ctx_row,
                     tm=n_ctx, name="post0_ctx")

    b_cfg = dict(n_q=B_Q_HEADS // 2, n_k=B_KV_HEADS // 2, v_cols=B_KV_HEADS * HEAD_DIM)
    wqkv_b = _pair_q_heads(b_w_qkv[0]).astype(BF16)
    gq_b, gk_b = tile2(b_q_gain[0]) * qk_scale, tile2(b_k_gain[0])
    x3, q, k, vt = _pre_call(x2, mods, pre_wi, pre_wo, wqkv_b, bd, gq_b, gk_b, rope_tabs,
                             layer=1, mod_row=lat, tm=tm, vt_chunk=WINDOW_BLOCK,
                             name="pre1_lat", **b_cfg)
    _, _, kc, vtc = _pre_call(xc2, mods, pre_wi, pre_wo, wqkv_b, bd, gq_b, gk_b, None,
                              layer=1, mod_row=ctx_row, tm=n_ctx, vt_chunk=WINDOW_BLOCK,
                              name="pre1_ctx", **b_cfg)
    sink = (b_sink[0].astype(F32) * LOG2E).reshape(B_KV_HEADS // 2, 2 * B_GROUP, 1)
    sink_rows = jnp.broadcast_to(sink, (B_KV_HEADS // 2, 2 * B_GROUP, WINDOW_BLOCK))
    sink_rows = sink_rows.reshape(B_KV_HEADS // 2, 1, 2 * B_GROUP * WINDOW_BLOCK)
    logit_bound = jnp.maximum(_score_bound(gq_b, gk_b), jnp.max(jnp.abs(sink_rows)))
    o = lax.cond(
        logit_bound <= UNSHIFTED_SCORE_LIMIT,
        functools.partial(_attn_b_call, stabilize=False, name="attn_b"),
        functools.partial(_attn_b_call, stabilize=True, name="attn_b"),
        sink_rows, q, k, vt, kc, vtc)
    x4 = _post_call(x3, o, mods, b_w_o[0].astype(BF16), post_wi, post_wo, layer=1,
                    mod_row=lat, tm=tm_post, name="post1_lat")
    return x4
```

```python
import functools
import math

import jax
import jax.numpy as jnp
import numpy as np
from jax import lax
from jax.experimental import pallas as pl
from jax.experimental.pallas import tpu as pltpu

F32 = jnp.float32
BF16 = jnp.bfloat16

LANES = 128
MXU_COLS = 256
HEAD_DIM = 64
N_MOD = 9
EPS = 1e-6
ROPE_BASE = 10000.0
GRID_W = 64
WINDOW_BLOCK = 128
NEG_INF = -1e30
LOG2E = math.log2(math.e)
UNSHIFTED_SCORE_LIMIT = 96.0
VMEM_LIMIT = 56 * 1024 * 1024
CAST_BLOCK_BYTES = 6 * 1024 * 1024

A_HEADS = 8
B_Q_HEADS = 16
B_KV_HEADS = 4
B_GROUP = B_Q_HEADS // B_KV_HEADS


def _resident(shape):
    nd = len(shape)
    return pl.BlockSpec(shape, lambda *_: (0,) * nd, pipeline_mode=pl.Buffered(1))


def _resident_layer(stack, layer):
    _, rows, cols = stack.shape
    return pl.BlockSpec((None, rows, cols), lambda *_: (layer, 0, 0),
                        pipeline_mode=pl.Buffered(1))


def _cast_kernel(w_ref, o_ref):
    o_ref[...] = w_ref[...].astype(BF16)


def _to_bf16(w, *, layer, name):
    _, rows, cols = w.shape
    rb = max(r for r in range(8, rows + 1, 8)
             if rows % r == 0 and r * cols * 4 <= CAST_BLOCK_BYTES)
    return pl.pallas_call(
        _cast_kernel, grid=(rows // rb,),
        in_specs=[pl.BlockSpec((1, rb, cols), lambda r: (layer, r, 0))],
        out_specs=pl.BlockSpec((1, rb, cols), lambda r: (0, r, 0)),
        out_shape=jax.ShapeDtypeStruct((1, rows, cols), BF16),
        compiler_params=pltpu.CompilerParams(
            dimension_semantics=("arbitrary",),
            vmem_limit_bytes=VMEM_LIMIT),
        name=name,
    )(w)


def _ada_kernel(cs_ref, w_ref, b_ref, out_ref):
    s = cs_ref[...]
    s = s * jax.nn.sigmoid(s)
    out_ref[0] = jnp.dot(s.astype(BF16), w_ref[0].astype(BF16),
                         preferred_element_type=F32) + b_ref[0]


def _ada(cs, ada_w, ada_b):
    depth, d, n = ada_w.shape
    rows = cs.shape[0]
    tn = n // 8
    return pl.pallas_call(
        _ada_kernel,
        grid=(depth, n // tn),
        in_specs=[
            pl.BlockSpec((rows, d), lambda i, j: (0, 0)),
            pl.BlockSpec((1, d, tn), lambda i, j: (i, 0, j)),
            pl.BlockSpec((1, 1, tn), lambda i, j: (i, 0, j)),
        ],
        out_specs=pl.BlockSpec((1, rows, tn), lambda i, j: (i, 0, j)),
        out_shape=jax.ShapeDtypeStruct((depth, rows, n), F32),
        compiler_params=pltpu.CompilerParams(
            dimension_semantics=("arbitrary", "arbitrary"),
            vmem_limit_bytes=VMEM_LIMIT),
        name="ada_mod",
    )(cs, ada_w, ada_b.reshape(depth, 1, n))


def _rms(x):
    return x * lax.rsqrt(jnp.mean(x * x, axis=-1, keepdims=True) + EPS)


def _modulated(x, shift, scale):
    return (_rms(x) * (1.0 + scale) + shift).astype(BF16)


def _ffn_half_step(x, shift, scale, gate, wi_ref, wo_ref):
    d_ff = wo_ref.shape[0]
    h = _modulated(x, shift, scale)
    gu = jnp.dot(h, wi_ref[...], preferred_element_type=F32)
    g = gu[:, :d_ff]
    u = gu[:, d_ff:]
    act = (g * jax.nn.sigmoid(g) * u).astype(BF16)
    ff = jnp.dot(act, wo_ref[...], preferred_element_type=F32)
    return x + (0.5 * gate) * ff


def _pre_kernel(*refs, n_q, n_k, v_cols, rope):
    if rope:
        (x_ref, mod_ref, wi_ref, wo_ref, wqkv_ref, bd_ref, gq_ref, gk_ref,
         cos_ref, sa_ref, sb_ref, x_out, q_out, k_out, vt_out, qkv_sc) = refs
    else:
        (x_ref, mod_ref, wi_ref, wo_ref, wqkv_ref, bd_ref, gq_ref, gk_ref,
         x_out, q_out, k_out, vt_out, qkv_sc) = refs

    step = pl.program_id(0)
    last = pl.num_programs(0) - 1

    @pl.when(step == 0)
    def _():
        qkv_sc[...] = jnp.zeros(qkv_sc.shape, F32)

    def finish_projection():
        bd = bd_ref[...]
        if rope:
            cos, sa, sb = cos_ref[...], sa_ref[...], sb_ref[...]

        def norm_rope_store(col0, n_slabs, gain, out):
            for s in range(0, n_slabs, 2):
                y = qkv_sc[:, col0 + s * LANES: col0 + (s + 2) * LANES]
                ss = jnp.dot((y * y).astype(BF16), bd, preferred_element_type=F32)
                y = y * lax.rsqrt(ss * (1.0 / HEAD_DIM) + EPS)
                for t in range(2):
                    z = y[:, t * LANES:(t + 1) * LANES] * gain
                    if rope:
                        z = (z * cos + pltpu.roll(z, LANES - 16, 1) * sa
                             + pltpu.roll(z, 16, 1) * sb)
                    out[0, s + t] = z.astype(BF16)

        norm_rope_store(0, n_q, gq_ref[...], q_out)
        norm_rope_store(n_q * LANES, n_k, gk_ref[...], k_out)

        v0 = (n_q + n_k) * LANES
        vt = qkv_sc[:, v0:v0 + v_cols].T.astype(BF16)
        chunk = vt_out.shape[3]
        for c in range(vt_out.shape[1]):
            vt_out[0, c] = vt[:, c * chunk:(c + 1) * chunk]

    def tile_matmuls():
        mod = mod_ref[0]
        x1 = _ffn_half_step(x_ref[0], mod[0:1], mod[1:2], mod[2:3], wi_ref, wo_ref)
        x_out[0] = x1
        h = _modulated(x1, mod[3:4], mod[4:5])
        qkv_sc[...] = jnp.dot(h, wqkv_ref[...], preferred_element_type=F32)

    @pl.when(step < last)
    def _():
        finish_projection()
        tile_matmuls()

    @pl.when(step == last)
    def _():
        finish_projection()


def _post_kernel(x_ref, o_ref, mod_ref, wout_ref, wi_ref, wo_ref, x_out):
    mod = mod_ref[0]
    attn = jnp.dot(o_ref[0], wout_ref[...], preferred_element_type=F32)
    x2 = x_ref[0] + mod[5:6] * attn
    x_out[0] = _ffn_half_step(x2, mod[6:7], mod[7:8], mod[8:9], wi_ref, wo_ref)


def _mod_spec(mods, layer, mod_row, wrap):
    _, _, n_mod, d = mods.shape
    return pl.BlockSpec((None, 1, n_mod, d),
                        wrap(lambda b, i: (layer, b if mod_row is None else mod_row, 0, 0)))


def _pre_call(x, mods, wi, wo, wqkv, bd, gq, gk, rope_tabs, *, layer, w_layer, mod_row,
              n_q, n_k, v_cols, tm, vt_chunk, name):
    bx, lx, d = x.shape
    rope = rope_tabs is not None
    nt = lx // tm
    n_tiles = bx * nt
    grid = (n_tiles + 1,)

    def cur(g):
        t = jnp.minimum(g, n_tiles - 1)
        return t // nt, t % nt

    def lag(g):
        t = jnp.maximum(g - 1, 0)
        return t // nt, t % nt

    def at_cur(f):
        return lambda g: f(*cur(g))

    def at_lag(f):
        return lambda g: f(*lag(g))

    in_specs = [
        pl.BlockSpec((1, tm, d), at_cur(lambda b, i: (b, i, 0))),
        _mod_spec(mods, layer, mod_row, at_cur),
        _resident_layer(wi, w_layer), _resident_layer(wo, w_layer), _resident(wqkv.shape),
        _resident(bd.shape), _resident(gq.shape), _resident(gk.shape),
    ]
    args = [x, mods, wi, wo, wqkv, bd, gq, gk]
    if rope:
        in_specs += [pl.BlockSpec((tm, LANES), at_lag(lambda b, i: (i, 0)))] * 3
        args += list(rope_tabs)
    if tm >= vt_chunk:
        vt_block = (1, tm // vt_chunk, v_cols, vt_chunk)
        vt_map = at_lag(lambda b, i: (b, i, 0, 0))
    else:
        per = vt_chunk // tm
        vt_block = (1, 1, v_cols, tm)
        vt_map = at_lag(lambda b, i: (b, i // per, 0, i % per))
    out_specs = [
        pl.BlockSpec((1, tm, d), at_cur(lambda b, i: (b, i, 0))),
        pl.BlockSpec((1, n_q, tm, LANES), at_lag(lambda b, i: (b, 0, i, 0))),
        pl.BlockSpec((1, n_k, tm, LANES), at_lag(lambda b, i: (b, 0, i, 0))),
        pl.BlockSpec(vt_block, vt_map),
    ]
    out_shape = [
        jax.ShapeDtypeStruct((bx, lx, d), F32),
        jax.ShapeDtypeStruct((bx, n_q, lx, LANES), BF16),
        jax.ShapeDtypeStruct((bx, n_k, lx, LANES), BF16),
        jax.ShapeDtypeStruct((bx, lx // vt_chunk, v_cols, vt_chunk), BF16),
    ]
    return pl.pallas_call(
        functools.partial(_pre_kernel, n_q=n_q, n_k=n_k, v_cols=v_cols, rope=rope),
        grid=grid, in_specs=in_specs, out_specs=out_specs, out_shape=out_shape,
        scratch_shapes=[pltpu.VMEM((tm, wqkv.shape[1]), F32)],
        compiler_params=pltpu.CompilerParams(
            dimension_semantics=("arbitrary",),
            vmem_limit_bytes=VMEM_LIMIT),
        name=name,
    )(*args)


def _post_call(x, o, mods, wout, wi, wo, *, layer, mod_row, tm, name):
    bx, lx, d = x.shape
    return pl.pallas_call(
        _post_kernel,
        grid=(bx, lx // tm),
        in_specs=[
            pl.BlockSpec((1, tm, d), lambda b, i: (b, i, 0)),
            pl.BlockSpec((1, tm, o.shape[2]), lambda b, i: (b, i, 0)),
            _mod_spec(mods, layer, mod_row, lambda f: f),
            _resident(wout.shape), _resident_layer(wi, layer), _resident_layer(wo, layer),
        ],
        out_specs=pl.BlockSpec((1, tm, d), lambda b, i: (b, i, 0)),
        out_shape=jax.ShapeDtypeStruct((bx, lx, d), F32),
        compiler_params=pltpu.CompilerParams(
            dimension_semantics=("arbitrary", "arbitrary"),
            vmem_limit_bytes=VMEM_LIMIT),
        name=name,
    )(x, o, mods, wout, wi, wo)


def _split_components(q_tile):
    qf = q_tile.astype(F32)
    lane = lax.broadcasted_iota(jnp.int32, qf.shape, 1)
    lo = jnp.where(lane < HEAD_DIM, qf, 0.0).astype(BF16)
    hi = jnp.where(lane >= HEAD_DIM, qf, 0.0).astype(BF16)
    return jnp.concatenate([lo, hi], axis=0)


def _attn_a_kernel(*refs, n_lat, n_cast, lam_init, stabilize):
    n_in = len(refs) - 1 - n_cast
    cast_in, cast_out = refs[n_in - n_cast:n_in], refs[n_in + 1:]
    for w_in, w_out in zip(cast_in, cast_out):
        w_out[...] = w_in[...].astype(BF16)
    refs = refs[:n_in - n_cast] + (refs[n_in],)
    if n_lat:
        lam_ref, gain_ref, q_ref, k_ref, vt_ref, kc_ref, vtc_ref, o_ref = refs
    else:
        lam_ref, gain_ref, q_ref, kc_ref, vtc_ref, o_ref = refs
    tq = q_ref.shape[2]
    qq = _split_components(q_ref[0, 0])

    def scores(kb):
        return lax.dot_general(kb, qq, (((1,), (1,)), ((), ())), preferred_element_type=F32)

    def update_unshifted(state, st, vtb):
        p = jnp.exp2(st)
        l_new = jnp.sum(p, axis=0, keepdims=True)
        acc_new = jnp.dot(vtb, p.astype(BF16), preferred_element_type=F32)
        if state is not None:
            l_new = state[1] + l_new
            acc_new = state[2] + acc_new
        return None, l_new, acc_new

    def update(state, st, vtb):
        if not stabilize:
            return update_unshifted(state, st, vtb)
        m_blk = jnp.max(st, axis=0, keepdims=True)
        if state is None:
            m_new = m_blk
        else:
            m_old, l_old, acc_old = state
            m_new = jnp.maximum(m_old, m_blk)
            alpha = jnp.exp2(m_old - m_new)
        p = jnp.exp2(st - m_new)
        l_new = jnp.sum(p, axis=0, keepdims=True)
        acc_new = jnp.dot(vtb, p.astype(BF16), preferred_element_type=F32)
        if state is not None:
            l_new = alpha * l_old + l_new
            acc_new = alpha * acc_old + acc_new
        return m_new, l_new, acc_new

    blocks = []
    if n_lat:
        tkv = vt_ref.shape[3]
        for j in range(n_lat):
            blocks.append((lambda j=j: k_ref[0, 0, j * tkv:(j + 1) * tkv, :],
                           lambda j=j: vt_ref[0, j]))
    ckv = vtc_ref.shape[3]
    for c in range(vtc_ref.shape[1]):
        blocks.append((lambda c=c: kc_ref[0, 0, c * ckv:(c + 1) * ckv, :],
                       lambda c=c: vtc_ref[0, c]))
    state = None
    st = scores(blocks[0][0]())
    for j, (_, load_vt) in enumerate(blocks):
        st_next = scores(blocks[j + 1][0]()) if j + 1 < len(blocks) else None
        state = update(state, st, load_vt())
        st = st_next
    _, l_fin, acc = state

    inv = 1.0 / l_fin
    lv = lam_ref[...]
    lam = (jnp.exp(jnp.sum(lv[0:1] * lv[1:2], axis=-1, keepdims=True))
           - jnp.exp(jnp.sum(lv[2:3] * lv[3:4], axis=-1, keepdims=True)) + lam_init)
    ot = acc[:, :tq] * inv[:, :tq] - lam * (acc[:, tq:] * inv[:, tq:])
    on = ot * lax.rsqrt(jnp.mean(ot * ot, axis=0, keepdims=True) + EPS)
    o_ref[0] = (on.T * (gain_ref[...] * (1.0 - lam_init))).astype(BF16)


def _cast_views(stacks):
    return [w.reshape(-1, w.shape[-1]) for w in stacks]


def _cast_row_block(n_rows, n_steps):
    for n_blocks in range(n_steps, 0, -1):
        if n_steps % n_blocks == 0 and n_rows % (16 * n_blocks) == 0:
            return n_rows // n_blocks, n_steps // n_blocks
    raise ValueError("rows do not split into bf16 sublane tiles")


def _attn_a_call(lam_vec, gain, q, k, vt, kc, vtc, *cast_views, cast_rows=(), tq, lam_init,
                 stabilize, name):
    b, h, lq, _ = q.shape
    n_lat = 0 if k is None else vt.shape[1]
    nq = lq // tq
    in_specs = [
        pl.BlockSpec(lam_vec.shape, lambda b_, h_, i: (0, 0)),
        pl.BlockSpec(gain.shape, lambda b_, h_, i: (0, 0)),
        pl.BlockSpec((1, 1, tq, LANES), lambda b_, h_, i: (b_, h_, i, 0)),
    ]
    args = [lam_vec, gain, q]
    if n_lat:
        in_specs += [
            pl.BlockSpec((1, 1, k.shape[2], LANES), lambda b_, h_, i: (b_, h_, 0, 0)),
            pl.BlockSpec((1, vt.shape[1], LANES, vt.shape[3]), lambda b_, h_, i: (b_, 0, h_, 0)),
        ]
        args += [k, vt]
    in_specs += [
        pl.BlockSpec((1, 1, kc.shape[2], LANES), lambda b_, h_, i: (b_, h_, 0, 0)),
        pl.BlockSpec((1, vtc.shape[1], LANES, vtc.shape[3]), lambda b_, h_, i: (b_, 0, h_, 0)),
    ]
    args += [kc, vtc]
    cast_in, cast_out, cast_shapes = [], [], []
    for w, (row0, n_rows) in zip(cast_views, cast_rows):
        rb, reps = _cast_row_block(n_rows, b * h * nq)
        assert row0 % rb == 0

        def block(b_, h_, i, first=row0 // rb, reps=reps):
            return (first + ((b_ * h + h_) * nq + i) // reps, 0)

        cast_in.append(pl.BlockSpec((rb, w.shape[1]), block))
        cast_out.append(pl.BlockSpec((rb, w.shape[1]), functools.partial(block, first=0)))
        cast_shapes.append(jax.ShapeDtypeStruct((n_rows, w.shape[1]), BF16))
    outs = pl.pallas_call(
        functools.partial(_attn_a_kernel, n_lat=n_lat, n_cast=len(cast_views),
                          lam_init=lam_init, stabilize=stabilize),
        grid=(b, h, nq),
        in_specs=in_specs + cast_in,
        out_specs=[pl.BlockSpec((1, tq, LANES), lambda b_, h_, i: (b_, i, h_))] + cast_out,
        out_shape=[jax.ShapeDtypeStruct((b, lq, h * LANES), BF16)] + cast_shapes,
        compiler_params=pltpu.CompilerParams(
            dimension_semantics=("arbitrary", "arbitrary", "arbitrary"),
            vmem_limit_bytes=VMEM_LIMIT),
        name=name + ("_shifted" if stabilize else ""),
    )(*args, *cast_views)
    return outs if cast_views else outs[0]


def _attn_b_kernel(sink_ref, q_ref, kp_ref, km_ref, kn_ref, kc_ref,
                   vp_ref, vm_ref, vn_ref, vc_ref, o_ref, *, stabilize):
    n = pl.program_id(1)
    n_steps = pl.num_programs(1)
    bb = WINDOW_BLOCK
    n_pair = kp_ref.shape[1]
    ncol = 2 * B_GROUP * bb
    half = ncol // 2

    key_j = lax.broadcasted_iota(jnp.int32, (bb, ncol), 0)
    qry_i = lax.broadcasted_iota(jnp.int32, (bb, ncol), 1) & (bb - 1)
    tri_prev = key_j >= qry_i
    tri_next = key_j <= qry_i
    edge_prev = key_j >= qry_i + jnp.where(n > 0, 0, bb)
    edge_next = key_j <= qry_i - jnp.where(n < n_steps - 1, 0, bb)
    lo_rows, hi_rows = slice(0, bb), slice(bb, 2 * bb)

    def chains(p):
        rows = slice(p * LANES, (p + 1) * LANES)
        yield (lo_rows, kp_ref[0, p], km_ref[0, p, lo_rows, :], km_ref[0, p, hi_rows, :],
               vp_ref[0, 0, rows, :], vm_ref[0, 0, rows, :], vm_ref[0, 1, rows, :],
               edge_prev, tri_next)
        yield (hi_rows, km_ref[0, p, lo_rows, :], km_ref[0, p, hi_rows, :], kn_ref[0, p],
               vm_ref[0, 0, rows, :], vm_ref[0, 1, rows, :], vn_ref[0, 0, rows, :],
               tri_prev, edge_next)

    def scores(p, chain):
        q_rows, k_prev, k_own, k_next = chain[:4]
        slabs = [q_ref[0, p * B_GROUP + g, q_rows, :].astype(F32) for g in range(B_GROUP)]
        lane = lax.broadcasted_iota(jnp.int32, slabs[0].shape, 1)
        lo = [jnp.where(lane < HEAD_DIM, s, 0.0).astype(BF16) for s in slabs]
        hi = [jnp.where(lane >= HEAD_DIM, s, 0.0).astype(BF16) for s in slabs]
        qq = jnp.concatenate(lo + hi, axis=0)
        kcat = jnp.concatenate([k_prev, k_own, k_next, kc_ref[0, p]], axis=0)
        return lax.dot_general(kcat, qq, (((1,), (1,)), ((), ())), preferred_element_type=F32)

    work = [(p, chain) for p in range(n_pair) for chain in chains(p)]
    st_next = scores(*work[0])
    for idx, (p, chain) in enumerate(work):
        st = st_next
        st_next = scores(*work[idx + 1]) if idx + 1 < len(work) else None
        q_rows, _, _, _, v_prev, v_own, v_next, in_prev, in_next = chain
        sink = sink_ref[p]
        if stabilize:
            s_prev = jnp.where(in_prev, st[0:bb], NEG_INF)
            s_own = st[bb:2 * bb]
            s_next = jnp.where(in_next, st[2 * bb:3 * bb], NEG_INF)
            s_ctx = st[3 * bb:]
            m = jnp.maximum(jnp.maximum(jnp.max(s_prev, axis=0, keepdims=True),
                                        jnp.max(s_own, axis=0, keepdims=True)),
                            jnp.maximum(jnp.max(s_next, axis=0, keepdims=True),
                                        jnp.max(s_ctx, axis=0, keepdims=True)))
            m = jnp.maximum(m, sink)
            parts = [jnp.exp2(s - m) for s in (s_prev, s_own, s_next, s_ctx)]
            l = jnp.exp2(sink - m)
        else:
            parts = [jnp.where(in_prev, jnp.exp2(st[0:bb]), 0.0),
                     jnp.exp2(st[bb:2 * bb]),
                     jnp.where(in_next, jnp.exp2(st[2 * bb:3 * bb]), 0.0),
                     jnp.exp2(st[3 * bb:])]
            l = jnp.exp2(sink)
        for part in parts:
            l = l + jnp.sum(part, axis=0, keepdims=True)
        pt = jnp.concatenate([part.astype(BF16) for part in parts], axis=0)
        rows = slice(p * LANES, (p + 1) * LANES)
        vcat = jnp.concatenate([v_prev, v_own, v_next]
                               + [vc_ref[0, c, rows, :] for c in range(vc_ref.shape[1])], axis=1)
        inv = 1.0 / l
        for e in range(2):
            ot = jnp.dot(vcat[e * HEAD_DIM:(e + 1) * HEAD_DIM], pt[:, e * half:(e + 1) * half],
                         preferred_element_type=F32) * inv[:, e * half:(e + 1) * half]
            for gp in range(B_GROUP // 2):
                blk = jnp.concatenate([ot[:, (2 * gp) * bb:(2 * gp + 1) * bb],
                                       ot[:, (2 * gp + 1) * bb:(2 * gp + 2) * bb]], axis=0)
                c0 = ((p * 2 + e) * (B_GROUP // 2) + gp) * LANES
                o_ref[0, q_rows, c0:c0 + LANES] = blk.T.astype(BF16)


def _attn_b_call(sink_rows, q, k, vt, kc, vtc, *, stabilize, name):
    b, n_slab, l, _ = q.shape
    n_pair = k.shape[1]
    bb = WINDOW_BLOCK
    nb = l // bb
    assert nb % 2 == 0 and vt.shape[3] == bb
    prev = lambda n: jnp.maximum(2 * n - 1, 0)
    nxt = lambda n: jnp.minimum(2 * n + 2, nb - 1)
    kspec = lambda f: pl.BlockSpec((1, n_pair, bb, LANES), lambda b_, n: (b_, 0, f(n), 0))
    vspec = lambda f: pl.BlockSpec((1, 1, vt.shape[2], bb), lambda b_, n: (b_, f(n), 0, 0))
    whole = lambda a: pl.BlockSpec((1,) + a.shape[1:], lambda b_, n: (b_,) + (0,) * (a.ndim - 1))
    return pl.pallas_call(
        functools.partial(_attn_b_kernel, stabilize=stabilize),
        grid=(b, nb // 2),
        in_specs=[
            pl.BlockSpec(sink_rows.shape, lambda b_, n: (0, 0, 0)),
            pl.BlockSpec((1, n_slab, 2 * bb, LANES), lambda b_, n: (b_, 0, n, 0)),
            kspec(prev),
            pl.BlockSpec((1, n_pair, 2 * bb, LANES), lambda b_, n: (b_, 0, n, 0)),
            kspec(nxt), whole(kc),
            vspec(prev),
            pl.BlockSpec((1, 2, vt.shape[2], bb), lambda b_, n: (b_, n, 0, 0)),
            vspec(nxt), whole(vtc),
        ],
        out_specs=pl.BlockSpec((1, 2 * bb, n_slab * LANES), lambda b_, n: (b_, n, 0)),
        out_shape=jax.ShapeDtypeStruct((b, l, n_slab * LANES), BF16),
        compiler_params=pltpu.CompilerParams(
            dimension_semantics=("arbitrary", "arbitrary"),
            vmem_limit_bytes=VMEM_LIMIT),
        name=name + ("_shifted" if stabilize else ""),
    )(sink_rows, q, k, k, k, kc, vt, vt, vt, vtc)


def _rope_tables(n_tok):
    rows = n_tok // GRID_W
    nf = HEAD_DIM // 4
    inv = (np.float32(ROPE_BASE) ** (-np.arange(nf, dtype=np.float32) / np.float32(nf)))
    row = np.repeat(np.arange(rows, dtype=np.float32), GRID_W)
    col = np.tile(np.arange(GRID_W, dtype=np.float32), rows)
    ang = np.stack([row[:, None] * inv, col[:, None] * inv], axis=1)
    ang = np.stack([ang, ang], axis=2).reshape(n_tok, HEAD_DIM).astype(np.float32)
    cos = np.tile(np.cos(ang), (1, LANES // HEAD_DIM)).astype(np.float32)
    sin = np.tile(np.sin(ang), (1, LANES // HEAD_DIM)).astype(np.float32)
    first_half = (np.arange(LANES) % (2 * nf)) < nf
    zero = np.float32(0.0)
    return (jnp.asarray(cos), jnp.asarray(np.where(first_half, -sin, zero)),
            jnp.asarray(np.where(first_half, zero, sin)))


def _group_sum_matrix():
    idx = np.arange(MXU_COLS) // HEAD_DIM
    return jnp.asarray(idx[:, None] == idx[None, :], dtype=BF16)


def _score_bound(gq_row, gk_row):
    return 1.05 * HEAD_DIM * jnp.max(jnp.abs(gq_row)) * jnp.max(jnp.abs(gk_row))


def _pair_q_heads(w_qkv):
    order = []
    for p in range(B_KV_HEADS // 2):
        for g in range(B_GROUP):
            order += [(2 * p) * B_GROUP + g, (2 * p + 1) * B_GROUP + g]
    d = w_qkv.shape[0]
    wq = w_qkv[:, :B_Q_HEADS * HEAD_DIM].reshape(d, B_Q_HEADS, HEAD_DIM)
    wq = wq[:, jnp.array(order)].reshape(d, B_Q_HEADS * HEAD_DIM)
    return jnp.concatenate([wq, w_qkv[:, B_Q_HEADS * HEAD_DIM:]], axis=1)


def kernel(x, c, ctx, c_ctx, ada_w, ada_b, ffn_pre_wi, ffn_pre_wo, ffn_post_wi, ffn_post_wo,
           a_w_qkv, a_w_o, a_q_gain, a_k_gain, a_lambda, a_subln_gain,
           b_w_qkv, b_w_o, b_q_gain, b_k_gain, b_sink):
    bsz, seq, d = x.shape
    n_ctx = ctx.shape[1]
    depth = ada_w.shape[0]
    assert depth == 2 and seq % WINDOW_BLOCK == 0

    rows = 8
    cs = jnp.concatenate([c, c_ctx[None], jnp.zeros((rows - bsz - 1, d), F32)], axis=0)
    mods = _ada(cs, ada_w, ada_b).reshape(depth, rows, N_MOD, d)
    lat, ctx_row = None, bsz

    rope_tabs = _rope_tables(seq)
    bd = _group_sum_matrix()
    qk_scale = HEAD_DIM ** -0.5 * LOG2E
    tile2 = lambda g: jnp.tile(g.astype(F32), LANES // HEAD_DIM).reshape(1, LANES)

    pre_wi = _to_bf16(ffn_pre_wi, layer=0, name="cast_pre_wi")
    pre_wo = _to_bf16(ffn_pre_wo, layer=0, name="cast_pre_wo")

    tm = 256
    tm_post = 512
    tkv = 512
    tq = 2048

    lam_init = 0.8 - 0.6 * math.exp(-0.3 * 0)
    a_cfg = dict(n_q=A_HEADS, n_k=A_HEADS, v_cols=A_HEADS * LANES)
    wqkv_a = a_w_qkv[0].astype(BF16)
    gq_a, gk_a = tile2(a_q_gain[0]) * qk_scale, tile2(a_k_gain[0])
    x1, q, k, vt = _pre_call(x, mods, pre_wi, pre_wo, wqkv_a, bd, gq_a, gk_a, rope_tabs,
                             layer=0, w_layer=0, mod_row=lat, tm=tm, vt_chunk=tkv,
                             name="pre0_lat", **a_cfg)
    xc1, qc, kc, vtc = _pre_call(ctx, mods, pre_wi, pre_wo, wqkv_a, bd, gq_a, gk_a, None,
                                 layer=0, w_layer=0, mod_row=ctx_row, tm=n_ctx, vt_chunk=n_ctx,
                                 name="pre0_ctx", **a_cfg)
    sub_gain = a_subln_gain[0].astype(F32).reshape(1, LANES)
    lam_vec = a_lambda[0].astype(F32)
    unshifted_ok = _score_bound(gq_a, gk_a) <= UNSHIFTED_SCORE_LIMIT

    views = _cast_views([ffn_post_wi, ffn_post_wo, ffn_pre_wi, ffn_pre_wo])
    wi_rows, wo_rows = ffn_pre_wi.shape[1], ffn_pre_wo.shape[1]
    cast_rows = [(0, depth * wi_rows), (0, depth * wo_rows), (wi_rows, wi_rows), (wo_rows, wo_rows)]

    def attn_a(stabilize):
        def run(q, k, vt, qc, kc, vtc, *views):
            o, *cast = _attn_a_call(lam_vec, sub_gain, q, k, vt, kc, vtc, *views,
                                    cast_rows=cast_rows, tq=tq, lam_init=lam_init,
                                    stabilize=stabilize, name="attn_a_lat")
            oc = _attn_a_call(lam_vec, sub_gain, qc, None, None, kc, vtc, tq=n_ctx,
                              lam_init=lam_init, stabilize=stabilize, name="attn_a_ctx")
            return (o, oc, *cast)
        return run

    o, oc, post_wi, post_wo, pre_wi1, pre_wo1 = lax.cond(
        unshifted_ok, attn_a(False), attn_a(True), q, k, vt, qc, kc, vtc, *views)
    post_wi = post_wi.reshape(ffn_post_wi.shape)
    post_wo = post_wo.reshape(ffn_post_wo.shape)
    pre_wi1 = pre_wi1.reshape((1,) + ffn_pre_wi.shape[1:])
    pre_wo1 = pre_wo1.reshape((1,) + ffn_pre_wo.shape[1:])
    wout_a = a_w_o[0].astype(BF16)
    x2 = _post_call(x1, o, mods, wout_a, post_wi, post_wo, layer=0, mod_row=lat, tm=tm_post,
                    name="post0_lat")
    xc2 = _post_call(xc1, oc, mods, wout_a, post_wi, post_wo, layer=0, mod_row=ctx_row,
                     tm=n_ctx, name="post0_ctx")

    b_cfg = dict(n_q=B_Q_HEADS // 2, n_k=B_KV_HEADS // 2, v_cols=B_KV_HEADS * HEAD_DIM)
    wqkv_b = _pair_q_heads(b_w_qkv[0]).astype(BF16)
    gq_b, gk_b = tile2(b_q_gain[0]) * qk_scale, tile2(b_k_gain[0])
    x3, q, k, vt = _pre_call(x2, mods, pre_wi1, pre_wo1, wqkv_b, bd, gq_b, gk_b, rope_tabs,
                             layer=1, w_layer=0, mod_row=lat, tm=tm, vt_chunk=WINDOW_BLOCK,
                             name="pre1_lat", **b_cfg)
    _, _, kc, vtc = _pre_call(xc2, mods, pre_wi1, pre_wo1, wqkv_b, bd, gq_b, gk_b, None,
                              layer=1, w_layer=0, mod_row=ctx_row, tm=n_ctx,
                              vt_chunk=WINDOW_BLOCK, name="pre1_ctx", **b_cfg)
    sink = (b_sink[0].astype(F32) * LOG2E).reshape(B_KV_HEADS // 2, 2 * B_GROUP, 1)
    sink_rows = jnp.broadcast_to(sink, (B_KV_HEADS // 2, 2 * B_GROUP, WINDOW_BLOCK))
    sink_rows = sink_rows.reshape(B_KV_HEADS // 2, 1, 2 * B_GROUP * WINDOW_BLOCK)
    logit_bound = jnp.maximum(_score_bound(gq_b, gk_b), jnp.max(jnp.abs(sink_rows)))
    o = lax.cond(
        logit_bound <= UNSHIFTED_SCORE_LIMIT,
        functools.partial(_attn_b_call, stabilize=False, name="attn_b"),
        functools.partial(_attn_b_call, stabilize=True, name="attn_b"),
        sink_rows, q, k, vt, kc, vtc)
    x4 = _post_call(x3, o, mods, b_w_o[0].astype(BF16), post_wi, post_wo, layer=1,
                    mod_row=lat, tm=tm_post, name="post1_lat")
    return x4
```

```python
import functools
import math

import jax
import jax.numpy as jnp
import numpy as np
from jax import lax
from jax.experimental import pallas as pl
from jax.experimental.pallas import tpu as pltpu

F32 = jnp.float32
BF16 = jnp.bfloat16

LANES = 128
BF16_SUBLANES = 16
MXU_COLS = 256
HEAD_DIM = 64
N_MOD = 9
EPS = 1e-6
ROPE_BASE = 10000.0
GRID_W = 64
WINDOW_BLOCK = 128
NEG_INF = -1e30
LOG2E = math.log2(math.e)
UNSHIFTED_SCORE_LIMIT = 96.0
VMEM_LIMIT = 56 * 1024 * 1024
CAST_BLOCK_BYTES = 6 * 1024 * 1024

A_HEADS = 8
B_Q_HEADS = 16
B_KV_HEADS = 4
B_GROUP = B_Q_HEADS // B_KV_HEADS
B_QUERY_BLOCKS = 4


def _resident(shape):
    nd = len(shape)
    return pl.BlockSpec(shape, lambda *_: (0,) * nd, pipeline_mode=pl.Buffered(1))


def _resident_layer(stack, layer):
    _, rows, cols = stack.shape
    return pl.BlockSpec((None, rows, cols), lambda *_: (layer, 0, 0),
                        pipeline_mode=pl.Buffered(1))


def _cast_kernel(w_ref, o_ref):
    o_ref[...] = w_ref[...].astype(BF16)


def _to_bf16(w, *, layer, name):
    _, rows, cols = w.shape
    rb = max(r for r in range(8, rows + 1, 8)
             if rows % r == 0 and r * cols * 4 <= CAST_BLOCK_BYTES)
    return pl.pallas_call(
        _cast_kernel, grid=(rows // rb,),
        in_specs=[pl.BlockSpec((1, rb, cols), lambda r: (layer, r, 0))],
        out_specs=pl.BlockSpec((1, rb, cols), lambda r: (0, r, 0)),
        out_shape=jax.ShapeDtypeStruct((1, rows, cols), BF16),
        compiler_params=pltpu.CompilerParams(
            dimension_semantics=("arbitrary",),
            vmem_limit_bytes=VMEM_LIMIT),
        name=name,
    )(w)


def _ada_kernel(cs_ref, w_ref, b_ref, out_ref):
    s = cs_ref[...]
    s = s * jax.nn.sigmoid(s)
    out_ref[0] = jnp.dot(s.astype(BF16), w_ref[0].astype(BF16),
                         preferred_element_type=F32) + b_ref[0]


def _ada(cs, ada_w, ada_b):
    depth, d, n = ada_w.shape
    rows = cs.shape[0]
    tn = n // 8
    return pl.pallas_call(
        _ada_kernel,
        grid=(depth, n // tn),
        in_specs=[
            pl.BlockSpec((rows, d), lambda i, j: (0, 0)),
            pl.BlockSpec((1, d, tn), lambda i, j: (i, 0, j)),
            pl.BlockSpec((1, 1, tn), lambda i, j: (i, 0, j)),
        ],
        out_specs=pl.BlockSpec((1, rows, tn), lambda i, j: (i, 0, j)),
        out_shape=jax.ShapeDtypeStruct((depth, rows, n), F32),
        compiler_params=pltpu.CompilerParams(
            dimension_semantics=("arbitrary", "arbitrary"),
            vmem_limit_bytes=VMEM_LIMIT),
        name="ada_mod",
    )(cs, ada_w, ada_b.reshape(depth, 1, n))


def _rms(x):
    return x * lax.rsqrt(jnp.mean(x * x, axis=-1, keepdims=True) + EPS)


def _modulated(x, shift, scale):
    return (_rms(x) * (1.0 + scale) + shift).astype(BF16)


def _ffn_half_step(x, shift, scale, gate, wi_ref, wo_ref):
    d_ff = wo_ref.shape[0]
    h = _modulated(x, shift, scale)
    gu = jnp.dot(h, wi_ref[...], preferred_element_type=F32)
    g = gu[:, :d_ff]
    u = gu[:, d_ff:]
    act = (g * jax.nn.sigmoid(g) * u).astype(BF16)
    ff = jnp.dot(act, wo_ref[...], preferred_element_type=F32)
    return x + (0.5 * gate) * ff


def _pre_kernel(*refs, n_q, n_k, v_cols, rope):
    if rope:
        (x_ref, mod_ref, wi_ref, wo_ref, wqkv_ref, bd_ref, gq_ref, gk_ref,
         cos_ref, sa_ref, sb_ref, x_out, q_out, k_out, vt_out, qkv_sc) = refs
    else:
        (x_ref, mod_ref, wi_ref, wo_ref, wqkv_ref, bd_ref, gq_ref, gk_ref,
         x_out, q_out, k_out, vt_out, qkv_sc) = refs

    step = pl.program_id(0)
    last = pl.num_programs(0) - 1

    @pl.when(step == 0)
    def _():
        qkv_sc[...] = jnp.zeros(qkv_sc.shape, F32)

    def finish_projection():
        bd = bd_ref[...]
        if rope:
            cos, sa, sb = cos_ref[...], sa_ref[...], sb_ref[...]

        def norm_rope_store(col0, n_slabs, gain, out):
            for s in range(0, n_slabs, 2):
                y = qkv_sc[:, col0 + s * LANES: col0 + (s + 2) * LANES]
                ss = jnp.dot((y * y).astype(BF16), bd, preferred_element_type=F32)
                y = y * lax.rsqrt(ss * (1.0 / HEAD_DIM) + EPS)
                for t in range(2):
                    z = y[:, t * LANES:(t + 1) * LANES] * gain
                    if rope:
                        z = (z * cos + pltpu.roll(z, LANES - 16, 1) * sa
                             + pltpu.roll(z, 16, 1) * sb)
                    out[0, s + t] = z.astype(BF16)

        norm_rope_store(0, n_q, gq_ref[...], q_out)
        norm_rope_store(n_q * LANES, n_k, gk_ref[...], k_out)

        v0 = (n_q + n_k) * LANES
        vt = qkv_sc[:, v0:v0 + v_cols].T.astype(BF16)
        chunk = vt_out.shape[3]
        for c in range(vt_out.shape[1]):
            vt_out[0, c] = vt[:, c * chunk:(c + 1) * chunk]

    def tile_matmuls():
        mod = mod_ref[0]
        x1 = _ffn_half_step(x_ref[0], mod[0:1], mod[1:2], mod[2:3], wi_ref, wo_ref)
        x_out[0] = x1
        h = _modulated(x1, mod[3:4], mod[4:5])
        qkv_sc[...] = jnp.dot(h, wqkv_ref[...], preferred_element_type=F32)

    @pl.when(step < last)
    def _():
        finish_projection()
        tile_matmuls()

    @pl.when(step == last)
    def _():
        finish_projection()


def _post_kernel(x_ref, o_ref, mod_ref, wout_ref, wi_ref, wo_ref, x_out):
    mod = mod_ref[0]
    attn = jnp.dot(o_ref[0], wout_ref[...], preferred_element_type=F32)
    x2 = x_ref[0] + mod[5:6] * attn
    x_out[0] = _ffn_half_step(x2, mod[6:7], mod[7:8], mod[8:9], wi_ref, wo_ref)


def _mod_spec(mods, layer, mod_row, wrap):
    _, _, n_mod, d = mods.shape
    return pl.BlockSpec((None, 1, n_mod, d),
                        wrap(lambda b, i: (layer, b if mod_row is None else mod_row, 0, 0)))


def _pre_call(x, mods, wi, wo, wqkv, bd, gq, gk, rope_tabs, *, layer, w_layer, mod_row,
              n_q, n_k, v_cols, tm, vt_chunk, name):
    bx, lx, d = x.shape
    rope = rope_tabs is not None
    nt = lx // tm
    n_tiles = bx * nt
    grid = (n_tiles + 1,)

    def cur(g):
        t = jnp.minimum(g, n_tiles - 1)
        return t // nt, t % nt

    def lag(g):
        t = jnp.maximum(g - 1, 0)
        return t // nt, t % nt

    def at_cur(f):
        return lambda g: f(*cur(g))

    def at_lag(f):
        return lambda g: f(*lag(g))

    in_specs = [
        pl.BlockSpec((1, tm, d), at_cur(lambda b, i: (b, i, 0))),
        _mod_spec(mods, layer, mod_row, at_cur),
        _resident_layer(wi, w_layer), _resident_layer(wo, w_layer), _resident(wqkv.shape),
        _resident(bd.shape), _resident(gq.shape), _resident(gk.shape),
    ]
    args = [x, mods, wi, wo, wqkv, bd, gq, gk]
    if rope:
        in_specs += [pl.BlockSpec((tm, LANES), at_lag(lambda b, i: (i, 0)))] * 3
        args += list(rope_tabs)
    if tm >= vt_chunk:
        vt_block = (1, tm // vt_chunk, v_cols, vt_chunk)
        vt_map = at_lag(lambda b, i: (b, i, 0, 0))
    else:
        per = vt_chunk // tm
        vt_block = (1, 1, v_cols, tm)
        vt_map = at_lag(lambda b, i: (b, i // per, 0, i % per))
    out_specs = [
        pl.BlockSpec((1, tm, d), at_cur(lambda b, i: (b, i, 0))),
        pl.BlockSpec((1, n_q, tm, LANES), at_lag(lambda b, i: (b, 0, i, 0))),
        pl.BlockSpec((1, n_k, tm, LANES), at_lag(lambda b, i: (b, 0, i, 0))),
        pl.BlockSpec(vt_block, vt_map),
    ]
    out_shape = [
        jax.ShapeDtypeStruct((bx, lx, d), F32),
        jax.ShapeDtypeStruct((bx, n_q, lx, LANES), BF16),
        jax.ShapeDtypeStruct((bx, n_k, lx, LANES), BF16),
        jax.ShapeDtypeStruct((bx, lx // vt_chunk, v_cols, vt_chunk), BF16),
    ]
    return pl.pallas_call(
        functools.partial(_pre_kernel, n_q=n_q, n_k=n_k, v_cols=v_cols, rope=rope),
        grid=grid, in_specs=in_specs, out_specs=out_specs, out_shape=out_shape,
        scratch_shapes=[pltpu.VMEM((tm, wqkv.shape[1]), F32)],
        compiler_params=pltpu.CompilerParams(
            dimension_semantics=("arbitrary",),
            vmem_limit_bytes=VMEM_LIMIT),
        name=name,
    )(*args)


def _post_call(x, o, mods, wout, wi, wo, *, layer, mod_row, tm, name):
    bx, lx, d = x.shape
    return pl.pallas_call(
        _post_kernel,
        grid=(bx, lx // tm),
        in_specs=[
            pl.BlockSpec((1, tm, d), lambda b, i: (b, i, 0)),
            pl.BlockSpec((1, tm, o.shape[2]), lambda b, i: (b, i, 0)),
            _mod_spec(mods, layer, mod_row, lambda f: f),
            _resident(wout.shape), _resident_layer(wi, layer), _resident_layer(wo, layer),
        ],
        out_specs=pl.BlockSpec((1, tm, d), lambda b, i: (b, i, 0)),
        out_shape=jax.ShapeDtypeStruct((bx, lx, d), F32),
        compiler_params=pltpu.CompilerParams(
            dimension_semantics=("arbitrary", "arbitrary"),
            vmem_limit_bytes=VMEM_LIMIT),
        name=name,
    )(x, o, mods, wout, wi, wo)


def _split_components(q_tile):
    qf = q_tile.astype(F32)
    lane = lax.broadcasted_iota(jnp.int32, qf.shape, 1)
    lo = jnp.where(lane < HEAD_DIM, qf, 0.0).astype(BF16)
    hi = jnp.where(lane >= HEAD_DIM, qf, 0.0).astype(BF16)
    return jnp.concatenate([lo, hi], axis=0)


def _attn_a_kernel(*refs, n_lat, n_cast, lam_init, stabilize):
    n_in = len(refs) - 1 - n_cast
    cast_in, cast_out = refs[n_in - n_cast:n_in], refs[n_in + 1:]
    refs = tuple(refs[:n_in - n_cast]) + (refs[n_in],)
    if n_lat:
        lam_ref, gain_ref, q_ref, k_ref, vt_ref, kc_ref, vtc_ref, o_ref = refs
    else:
        lam_ref, gain_ref, q_ref, kc_ref, vtc_ref, o_ref = refs
    tq = q_ref.shape[2]

    def finish(head, l_fin, acc):
        inv = 1.0 / l_fin
        lv = lam_ref[...]
        lam = (jnp.exp(jnp.sum(lv[0:1] * lv[1:2], axis=-1, keepdims=True))
               - jnp.exp(jnp.sum(lv[2:3] * lv[3:4], axis=-1, keepdims=True)) + lam_init)
        ot = acc[:, :tq] * inv[:, :tq] - lam * (acc[:, tq:] * inv[:, tq:])
        on = ot * lax.rsqrt(jnp.mean(ot * ot, axis=0, keepdims=True) + EPS)
        o = (on.T * (gain_ref[...] * (1.0 - lam_init))).astype(BF16)
        o_ref[0, :, head * LANES:(head + 1) * LANES] = o

    for w_in, w_out in zip(cast_in, cast_out):
        w_out[...] = w_in[...].astype(BF16)
    for head in range(q_ref.shape[1]):
        finish(head, *_attn_a_sums(head, q_ref, k_ref if n_lat else None,
                                   vt_ref if n_lat else None, kc_ref, vtc_ref, stabilize))


def _attn_a_sums(head, q_ref, k_ref, vt_ref, kc_ref, vtc_ref, stabilize):
    qq = _split_components(q_ref[0, head])
    v_rows = slice(head * LANES, (head + 1) * LANES)

    def scores(kb):
        return lax.dot_general(kb, qq, (((1,), (1,)), ((), ())), preferred_element_type=F32)

    def update_unshifted(state, st, vtb):
        p = jnp.exp2(st)
        l_new = jnp.sum(p, axis=0, keepdims=True)
        acc_new = jnp.dot(vtb, p.astype(BF16), preferred_element_type=F32)
        if state is not None:
            l_new = state[1] + l_new
            acc_new = state[2] + acc_new
        return None, l_new, acc_new

    def update(state, st, vtb):
        if not stabilize:
            return update_unshifted(state, st, vtb)
        m_blk = jnp.max(st, axis=0, keepdims=True)
        if state is None:
            m_new = m_blk
        else:
            m_old, l_old, acc_old = state
            m_new = jnp.maximum(m_old, m_blk)
            alpha = jnp.exp2(m_old - m_new)
        p = jnp.exp2(st - m_new)
        l_new = jnp.sum(p, axis=0, keepdims=True)
        acc_new = jnp.dot(vtb, p.astype(BF16), preferred_element_type=F32)
        if state is not None:
            l_new = alpha * l_old + l_new
            acc_new = alpha * acc_old + acc_new
        return m_new, l_new, acc_new

    blocks = []
    if k_ref is not None:
        tkv = vt_ref.shape[3]
        for j in range(vt_ref.shape[1]):
            blocks.append((lambda j=j: k_ref[0, head, j * tkv:(j + 1) * tkv, :],
                           lambda j=j: vt_ref[0, j, v_rows, :]))
    ckv = vtc_ref.shape[3]
    for c in range(vtc_ref.shape[1]):
        blocks.append((lambda c=c: kc_ref[0, head, c * ckv:(c + 1) * ckv, :],
                       lambda c=c: vtc_ref[0, c, v_rows, :]))
    state = None
    st = scores(blocks[0][0]())
    for j, (_, load_vt) in enumerate(blocks):
        st_next = scores(blocks[j + 1][0]()) if j + 1 < len(blocks) else None
        state = update(state, st, load_vt())
        st = st_next
    _, l_fin, acc = state
    return l_fin, acc


def _cast_views(stacks):
    return [w.reshape(-1, w.shape[-1]) for w in stacks]


def _cast_row_block(n_rows, n_steps):
    for n_blocks in range(n_steps, 0, -1):
        if n_steps % n_blocks == 0 and n_rows % (BF16_SUBLANES * n_blocks) == 0:
            return n_rows // n_blocks, n_steps // n_blocks
    raise ValueError("rows do not split into bf16 sublane tiles")


def _attn_a_call(lam_vec, gain, q, k, vt, kc, vtc, *cast_views, cast_rows=(), tq, heads_per_step,
                 lam_init, stabilize, name):
    b, h, lq, _ = q.shape
    n_lat = 0 if k is None else vt.shape[1]
    nq = lq // tq
    hps = heads_per_step
    nh = h // hps
    n_tiles = b * nh * nq

    def at_cur(f):
        return lambda g: f(g // (nh * nq), (g // nq) % nh, g % nq)

    in_specs = [
        pl.BlockSpec(lam_vec.shape, lambda g: (0, 0)),
        pl.BlockSpec(gain.shape, lambda g: (0, 0)),
        pl.BlockSpec((1, hps, tq, LANES), at_cur(lambda b_, h_, i: (b_, h_, i, 0))),
    ]
    args = [lam_vec, gain, q]
    if n_lat:
        in_specs += [
            pl.BlockSpec((1, hps, k.shape[2], LANES), at_cur(lambda b_, h_, i: (b_, h_, 0, 0))),
            pl.BlockSpec((1, vt.shape[1], hps * LANES, vt.shape[3]),
                         at_cur(lambda b_, h_, i: (b_, 0, h_, 0))),
        ]
        args += [k, vt]
    in_specs += [
        pl.BlockSpec((1, hps, kc.shape[2], LANES), at_cur(lambda b_, h_, i: (b_, h_, 0, 0))),
        pl.BlockSpec((1, vtc.shape[1], hps * LANES, vtc.shape[3]),
                     at_cur(lambda b_, h_, i: (b_, 0, h_, 0))),
    ]
    args += [kc, vtc]
    cast_in, cast_out, cast_shapes = [], [], []
    for w, (row0, n_rows) in zip(cast_views, cast_rows):
        rb, reps = _cast_row_block(n_rows, n_tiles)
        assert row0 % rb == 0

        def block(g, first=row0 // rb, reps=reps):
            return (first + g // reps, 0)

        cast_in.append(pl.BlockSpec((rb, w.shape[1]), block))
        cast_out.append(pl.BlockSpec((rb, w.shape[1]), functools.partial(block, first=0)))
        cast_shapes.append(jax.ShapeDtypeStruct((n_rows, w.shape[1]), BF16))
    outs = pl.pallas_call(
        functools.partial(_attn_a_kernel, n_lat=n_lat, n_cast=len(cast_views),
                          lam_init=lam_init, stabilize=stabilize),
        grid=(n_tiles,),
        in_specs=in_specs + cast_in,
        out_specs=[pl.BlockSpec((1, tq, hps * LANES),
                                at_cur(lambda b_, h_, i: (b_, i, h_)))] + cast_out,
        out_shape=[jax.ShapeDtypeStruct((b, lq, h * LANES), BF16)] + cast_shapes,
        compiler_params=pltpu.CompilerParams(
            dimension_semantics=("arbitrary",),
            vmem_limit_bytes=VMEM_LIMIT),
        name=name + ("_shifted" if stabilize else ""),
    )(*args, *cast_views)
    return outs if cast_views else outs[0]


def _attn_b_kernel(sink_ref, q_ref, kp_ref, km_ref, kn_ref, kc_ref,
                   vp_ref, vm_ref, vn_ref, vc_ref, o_ref, *, stabilize):
    n = pl.program_id(1)
    n_steps = pl.num_programs(1)
    bb = WINDOW_BLOCK
    n_pair = kp_ref.shape[1]
    qb = vm_ref.shape[1]
    ncol = 2 * B_GROUP * bb
    half = ncol // 2

    key_j = lax.broadcasted_iota(jnp.int32, (bb, ncol), 0)
    qry_i = lax.broadcasted_iota(jnp.int32, (bb, ncol), 1) & (bb - 1)
    tri_prev = key_j >= qry_i
    tri_next = key_j <= qry_i
    edge_prev = key_j >= qry_i + jnp.where(n > 0, 0, bb)
    edge_next = key_j <= qry_i - jnp.where(n < n_steps - 1, 0, bb)

    def chains(p):
        rows = slice(p * LANES, (p + 1) * LANES)
        k_blk = lambda j: km_ref[0, p, j * bb:(j + 1) * bb, :]
        v_blk = lambda j: vm_ref[0, j, rows, :]
        for j in range(qb):
            first, final = j == 0, j == qb - 1
            yield (slice(j * bb, (j + 1) * bb),
                   kp_ref[0, p] if first else k_blk(j - 1), k_blk(j),
                   kn_ref[0, p] if final else k_blk(j + 1),
                   vp_ref[0, 0, rows, :] if first else v_blk(j - 1), v_blk(j),
                   vn_ref[0, 0, rows, :] if final else v_blk(j + 1),
                   edge_prev if first else tri_prev, edge_next if final else tri_next)

    def scores(p, chain):
        q_rows, k_prev, k_own, k_next = chain[:4]
        slabs = [q_ref[0, p * B_GROUP + g, q_rows, :].astype(F32) for g in range(B_GROUP)]
        lane = lax.broadcasted_iota(jnp.int32, slabs[0].shape, 1)
        lo = [jnp.where(lane < HEAD_DIM, s, 0.0).astype(BF16) for s in slabs]
        hi = [jnp.where(lane >= HEAD_DIM, s, 0.0).astype(BF16) for s in slabs]
        qq = jnp.concatenate(lo + hi, axis=0)
        kcat = jnp.concatenate([k_prev, k_own, k_next, kc_ref[0, p]], axis=0)
        return lax.dot_general(kcat, qq, (((1,), (1,)), ((), ())), preferred_element_type=F32)

    work = [(p, chain) for p in range(n_pair) for chain in chains(p)]
    st_next = scores(*work[0])
    for idx, (p, chain) in enumerate(work):
        st = st_next
        st_next = scores(*work[idx + 1]) if idx + 1 < len(work) else None
        q_rows, _, _, _, v_prev, v_own, v_next, in_prev, in_next = chain
        sink = sink_ref[p]
        if stabilize:
            s_prev = jnp.where(in_prev, st[0:bb], NEG_INF)
            s_own = st[bb:2 * bb]
            s_next = jnp.where(in_next, st[2 * bb:3 * bb], NEG_INF)
            s_ctx = st[3 * bb:]
            m = jnp.maximum(jnp.maximum(jnp.max(s_prev, axis=0, keepdims=True),
                                        jnp.max(s_own, axis=0, keepdims=True)),
                            jnp.maximum(jnp.max(s_next, axis=0, keepdims=True),
                                        jnp.max(s_ctx, axis=0, keepdims=True)))
            m = jnp.maximum(m, sink)
            parts = [jnp.exp2(s - m) for s in (s_prev, s_own, s_next, s_ctx)]
            l = jnp.exp2(sink - m)
        else:
            parts = [jnp.where(in_prev, jnp.exp2(st[0:bb]), 0.0),
                     jnp.exp2(st[bb:2 * bb]),
                     jnp.where(in_next, jnp.exp2(st[2 * bb:3 * bb]), 0.0),
                     jnp.exp2(st[3 * bb:])]
            l = jnp.exp2(sink)
        for part in parts:
            l = l + jnp.sum(part, axis=0, keepdims=True)
        pt = jnp.concatenate([part.astype(BF16) for part in parts], axis=0)
        rows = slice(p * LANES, (p + 1) * LANES)
        vcat = jnp.concatenate([v_prev, v_own, v_next]
                               + [vc_ref[0, c, rows, :] for c in range(vc_ref.shape[1])], axis=1)
        inv = 1.0 / l
        for e in range(2):
            ot = jnp.dot(vcat[e * HEAD_DIM:(e + 1) * HEAD_DIM], pt[:, e * half:(e + 1) * half],
                         preferred_element_type=F32) * inv[:, e * half:(e + 1) * half]
            for gp in range(B_GROUP // 2):
                blk = jnp.concatenate([ot[:, (2 * gp) * bb:(2 * gp + 1) * bb],
                                       ot[:, (2 * gp + 1) * bb:(2 * gp + 2) * bb]], axis=0)
                c0 = ((p * 2 + e) * (B_GROUP // 2) + gp) * LANES
                o_ref[0, q_rows, c0:c0 + LANES] = blk.T.astype(BF16)


def _attn_b_call(sink_rows, q, k, vt, kc, vtc, *, stabilize, name):
    b, n_slab, l, _ = q.shape
    n_pair = k.shape[1]
    bb = WINDOW_BLOCK
    nb = l // bb
    qb = B_QUERY_BLOCKS
    assert nb % qb == 0 and vt.shape[3] == bb
    prev = lambda n: jnp.maximum(qb * n - 1, 0)
    nxt = lambda n: jnp.minimum(qb * n + qb, nb - 1)
    kspec = lambda f: pl.BlockSpec((1, n_pair, bb, LANES), lambda b_, n: (b_, 0, f(n), 0))
    vspec = lambda f: pl.BlockSpec((1, 1, vt.shape[2], bb), lambda b_, n: (b_, f(n), 0, 0))
    whole = lambda a: pl.BlockSpec((1,) + a.shape[1:], lambda b_, n: (b_,) + (0,) * (a.ndim - 1))
    return pl.pallas_call(
        functools.partial(_attn_b_kernel, stabilize=stabilize),
        grid=(b, nb // qb),
        in_specs=[
            pl.BlockSpec(sink_rows.shape, lambda b_, n: (0, 0, 0)),
            pl.BlockSpec((1, n_slab, qb * bb, LANES), lambda b_, n: (b_, 0, n, 0)),
            kspec(prev),
            pl.BlockSpec((1, n_pair, qb * bb, LANES), lambda b_, n: (b_, 0, n, 0)),
            kspec(nxt), whole(kc),
            vspec(prev),
            pl.BlockSpec((1, qb, vt.shape[2], bb), lambda b_, n: (b_, n, 0, 0)),
            vspec(nxt), whole(vtc),
        ],
        out_specs=pl.BlockSpec((1, qb * bb, n_slab * LANES), lambda b_, n: (b_, n, 0)),
        out_shape=jax.ShapeDtypeStruct((b, l, n_slab * LANES), BF16),
        compiler_params=pltpu.CompilerParams(
            dimension_semantics=("arbitrary", "arbitrary"),
            vmem_limit_bytes=VMEM_LIMIT),
        name=name + ("_shifted" if stabilize else ""),
    )(sink_rows, q, k, k, k, kc, vt, vt, vt, vtc)


def _rope_tables(n_tok):
    rows = n_tok // GRID_W
    nf = HEAD_DIM // 4
    inv = (np.float32(ROPE_BASE) ** (-np.arange(nf, dtype=np.float32) / np.float32(nf)))
    row = np.repeat(np.arange(rows, dtype=np.float32), GRID_W)
    col = np.tile(np.arange(GRID_W, dtype=np.float32), rows)
    ang = np.stack([row[:, None] * inv, col[:, None] * inv], axis=1)
    ang = np.stack([ang, ang], axis=2).reshape(n_tok, HEAD_DIM).astype(np.float32)
    cos = np.tile(np.cos(ang), (1, LANES // HEAD_DIM)).astype(np.float32)
    sin = np.tile(np.sin(ang), (1, LANES // HEAD_DIM)).astype(np.float32)
    first_half = (np.arange(LANES) % (2 * nf)) < nf
    zero = np.float32(0.0)
    return (jnp.asarray(cos), jnp.asarray(np.where(first_half, -sin, zero)),
            jnp.asarray(np.where(first_half, zero, sin)))


def _group_sum_matrix():
    idx = np.arange(MXU_COLS) // HEAD_DIM
    return jnp.asarray(idx[:, None] == idx[None, :], dtype=BF16)


def _score_bound(gq_row, gk_row):
    return 1.05 * HEAD_DIM * jnp.max(jnp.abs(gq_row)) * jnp.max(jnp.abs(gk_row))


def _pair_q_heads(w_qkv):
    order = []
    for p in range(B_KV_HEADS // 2):
        for g in range(B_GROUP):
            order += [(2 * p) * B_GROUP + g, (2 * p + 1) * B_GROUP + g]
    d = w_qkv.shape[0]
    wq = w_qkv[:, :B_Q_HEADS * HEAD_DIM].reshape(d, B_Q_HEADS, HEAD_DIM)
    wq = wq[:, jnp.array(order)].reshape(d, B_Q_HEADS * HEAD_DIM)
    return jnp.concatenate([wq, w_qkv[:, B_Q_HEADS * HEAD_DIM:]], axis=1)


def kernel(x, c, ctx, c_ctx, ada_w, ada_b, ffn_pre_wi, ffn_pre_wo, ffn_post_wi, ffn_post_wo,
           a_w_qkv, a_w_o, a_q_gain, a_k_gain, a_lambda, a_subln_gain,
           b_w_qkv, b_w_o, b_q_gain, b_k_gain, b_sink):
    bsz, seq, d = x.shape
    n_ctx = ctx.shape[1]
    depth = ada_w.shape[0]
    assert depth == 2 and seq % WINDOW_BLOCK == 0

    rows = 8
    cs = jnp.concatenate([c, c_ctx[None], jnp.zeros((rows - bsz - 1, d), F32)], axis=0)
    mods = _ada(cs, ada_w, ada_b).reshape(depth, rows, N_MOD, d)
    lat, ctx_row = None, bsz

    rope_tabs = _rope_tables(seq)
    bd = _group_sum_matrix()
    qk_scale = HEAD_DIM ** -0.5 * LOG2E
    tile2 = lambda g: jnp.tile(g.astype(F32), LANES // HEAD_DIM).reshape(1, LANES)

    pre_wi = _to_bf16(ffn_pre_wi, layer=0, name="cast_pre_wi")
    pre_wo = _to_bf16(ffn_pre_wo, layer=0, name="cast_pre_wo")

    tm = 512
    tm_post = 512
    tkv = 512
    tq = 2048

    lam_init = 0.8 - 0.6 * math.exp(-0.3 * 0)
    a_cfg = dict(n_q=A_HEADS, n_k=A_HEADS, v_cols=A_HEADS * LANES)
    wqkv_a = a_w_qkv[0].astype(BF16)
    gq_a, gk_a = tile2(a_q_gain[0]) * qk_scale, tile2(a_k_gain[0])
    x1, q, k, vt = _pre_call(x, mods, pre_wi, pre_wo, wqkv_a, bd, gq_a, gk_a, rope_tabs,
                             layer=0, w_layer=0, mod_row=lat, tm=tm, vt_chunk=tkv,
                             name="pre0_lat", **a_cfg)
    xc1, qc, kc, vtc = _pre_call(ctx, mods, pre_wi, pre_wo, wqkv_a, bd, gq_a, gk_a, None,
                                 layer=0, w_layer=0, mod_row=ctx_row, tm=n_ctx, vt_chunk=n_ctx,
                                 name="pre0_ctx", **a_cfg)
    sub_gain = a_subln_gain[0].astype(F32).reshape(1, LANES)
    lam_vec = a_lambda[0].astype(F32)
    unshifted_ok = _score_bound(gq_a, gk_a) <= UNSHIFTED_SCORE_LIMIT

    views = _cast_views([ffn_post_wi, ffn_post_wo, ffn_pre_wi, ffn_pre_wo, a_w_o, b_w_o])
    wi_rows, wo_rows = ffn_pre_wi.shape[1], ffn_pre_wo.shape[1]
    cast_rows = [(0, depth * wi_rows), (0, depth * wo_rows), (wi_rows, wi_rows), (wo_rows, wo_rows),
                 (0, a_w_o.shape[1]), (0, b_w_o.shape[1])]

    def attn_a(stabilize):
        def run(q, k, vt, qc, kc, vtc, *views):
            o, *cast = _attn_a_call(lam_vec, sub_gain, q, k, vt, kc, vtc, *views,
                                    cast_rows=cast_rows, tq=tq, heads_per_step=1,
                                    lam_init=lam_init, stabilize=stabilize, name="attn_a_lat")
            oc = _attn_a_call(lam_vec, sub_gain, qc, None, None, kc, vtc, tq=n_ctx,
                              heads_per_step=A_HEADS, lam_init=lam_init, stabilize=stabilize,
                              name="attn_a_ctx")
            return (o, oc, *cast)
        return run

    o, oc, post_wi, post_wo, pre_wi1, pre_wo1, wout_a, wout_b = lax.cond(
        unshifted_ok, attn_a(False), attn_a(True), q, k, vt, qc, kc, vtc, *views)
    post_wi = post_wi.reshape(ffn_post_wi.shape)
    post_wo = post_wo.reshape(ffn_post_wo.shape)
    pre_wi1 = pre_wi1.reshape((1,) + ffn_pre_wi.shape[1:])
    pre_wo1 = pre_wo1.reshape((1,) + ffn_pre_wo.shape[1:])
    x2 = _post_call(x1, o, mods, wout_a, post_wi, post_wo, layer=0, mod_row=lat, tm=tm_post,
                    name="post0_lat")
    xc2 = _post_call(xc1, oc, mods, wout_a, post_wi, post_wo, layer=0, mod_row=ctx_row,
                     tm=n_ctx, name="post0_ctx")

    b_cfg = dict(n_q=B_Q_HEADS // 2, n_k=B_KV_HEADS // 2, v_cols=B_KV_HEADS * HEAD_DIM)
    wqkv_b = _pair_q_heads(b_w_qkv[0]).astype(BF16)
    gq_b, gk_b = tile2(b_q_gain[0]) * qk_scale, tile2(b_k_gain[0])
    x3, q, k, vt = _pre_call(x2, mods, pre_wi1, pre_wo1, wqkv_b, bd, gq_b, gk_b, rope_tabs,
                             layer=1, w_layer=0, mod_row=lat, tm=tm, vt_chunk=WINDOW_BLOCK,
                             name="pre1_lat", **b_cfg)
    _, _, kc, vtc = _pre_call(xc2, mods, pre_wi1, pre_wo1, wqkv_b, bd, gq_b, gk_b, None,
                              layer=1, w_layer=0, mod_row=ctx_row, tm=n_ctx,
                              vt_chunk=WINDOW_BLOCK, name="pre1_ctx", **b_cfg)
    sink = (b_sink[0].astype(F32) * LOG2E).reshape(B_KV_HEADS // 2, 2 * B_GROUP, 1)
    sink_rows = jnp.broadcast_to(sink, (B_KV_HEADS // 2, 2 * B_GROUP, WINDOW_BLOCK))
    sink_rows = sink_rows.reshape(B_KV_HEADS // 2, 1, 2 * B_GROUP * WINDOW_BLOCK)
    logit_bound = jnp.maximum(_score_bound(gq_b, gk_b), jnp.max(jnp.abs(sink_rows)))
    o = lax.cond(
        logit_bound <= UNSHIFTED_SCORE_LIMIT,
        functools.partial(_attn_b_call, stabilize=False, name="attn_b"),
        functools.partial(_attn_b_call, stabilize=True, name="attn_b"),
        sink_rows, q, k, vt, kc, vtc)
    x4 = _post_call(x3, o, mods, wout_b, post_wi, post_wo, layer=1,
                    mod_row=lat, tm=tm_post, name="post1_lat")
    return x4
```

```python
import functools
import math

import jax
import jax.numpy as jnp
import numpy as np
from jax import lax
from jax.experimental import pallas as pl
from jax.experimental.pallas import tpu as pltpu

F32 = jnp.float32
BF16 = jnp.bfloat16

LANES = 128
BF16_SUBLANES = 16
MXU_COLS = 256
HEAD_DIM = 64
N_MOD = 9
EPS = 1e-6
ROPE_BASE = 10000.0
GRID_W = 64
WINDOW_BLOCK = 128
NEG_INF = -1e30
LOG2E = math.log2(math.e)
UNSHIFTED_SCORE_LIMIT = 96.0
VMEM_LIMIT = 56 * 1024 * 1024
CAST_BLOCK_BYTES = 6 * 1024 * 1024

A_HEADS = 8
B_Q_HEADS = 16
B_KV_HEADS = 4
B_GROUP = B_Q_HEADS // B_KV_HEADS
B_QUERY_BLOCKS = 8


def _resident(shape):
    nd = len(shape)
    return pl.BlockSpec(shape, lambda *_: (0,) * nd, pipeline_mode=pl.Buffered(1))


def _resident_layer(stack, layer):
    _, rows, cols = stack.shape
    return pl.BlockSpec((None, rows, cols), lambda *_: (layer, 0, 0),
                        pipeline_mode=pl.Buffered(1))


def _cast_kernel(w_ref, o_ref):
    o_ref[...] = w_ref[...].astype(BF16)


def _to_bf16(w, *, layer, name):
    _, rows, cols = w.shape
    rb = max(r for r in range(8, rows + 1, 8)
             if rows % r == 0 and r * cols * 4 <= CAST_BLOCK_BYTES)
    return pl.pallas_call(
        _cast_kernel, grid=(rows // rb,),
        in_specs=[pl.BlockSpec((1, rb, cols), lambda r: (layer, r, 0))],
        out_specs=pl.BlockSpec((1, rb, cols), lambda r: (0, r, 0)),
        out_shape=jax.ShapeDtypeStruct((1, rows, cols), BF16),
        compiler_params=pltpu.CompilerParams(
            dimension_semantics=("arbitrary",),
            vmem_limit_bytes=VMEM_LIMIT),
        name=name,
    )(w)


def _ada_kernel(cs_ref, w_ref, b_ref, out_ref):
    s = cs_ref[...]
    s = s * jax.nn.sigmoid(s)
    out_ref[0] = jnp.dot(s.astype(BF16), w_ref[0].astype(BF16),
                         preferred_element_type=F32) + b_ref[0]


def _ada(cs, ada_w, ada_b):
    depth, d, n = ada_w.shape
    rows = cs.shape[0]
    tn = n // 4
    return pl.pallas_call(
        _ada_kernel,
        grid=(depth, n // tn),
        in_specs=[
            pl.BlockSpec((rows, d), lambda i, j: (0, 0)),
            pl.BlockSpec((1, d, tn), lambda i, j: (i, 0, j)),
            pl.BlockSpec((1, 1, tn), lambda i, j: (i, 0, j)),
        ],
        out_specs=pl.BlockSpec((1, rows, tn), lambda i, j: (i, 0, j)),
        out_shape=jax.ShapeDtypeStruct((depth, rows, n), F32),
        compiler_params=pltpu.CompilerParams(
            dimension_semantics=("arbitrary", "arbitrary"),
            vmem_limit_bytes=VMEM_LIMIT),
        name="ada_mod",
    )(cs, ada_w, ada_b.reshape(depth, 1, n))


def _rms(x):
    return x * lax.rsqrt(jnp.mean(x * x, axis=-1, keepdims=True) + EPS)


def _modulated(x, shift, scale):
    return (_rms(x) * (1.0 + scale) + shift).astype(BF16)


def _ffn_half_step(x, shift, scale, gate, wi_ref, wo_ref):
    d_ff = wo_ref.shape[0]
    h = _modulated(x, shift, scale)
    gu = jnp.dot(h, wi_ref[...], preferred_element_type=F32)
    g = gu[:, :d_ff]
    u = gu[:, d_ff:]
    act = (g * jax.nn.sigmoid(g) * u).astype(BF16)
    ff = jnp.dot(act, wo_ref[...], preferred_element_type=F32)
    return x + (0.5 * gate) * ff


def _pre_kernel(*refs, n_q, n_k, v_cols, rope):
    if rope:
        (x_ref, mod_ref, wi_ref, wo_ref, wqkv_ref, bd_ref, gq_ref, gk_ref,
         cos_ref, sa_ref, sb_ref, x_out, q_out, k_out, vt_out, qkv_sc) = refs
    else:
        (x_ref, mod_ref, wi_ref, wo_ref, wqkv_ref, bd_ref, gq_ref, gk_ref,
         x_out, q_out, k_out, vt_out, qkv_sc) = refs

    step = pl.program_id(0)
    last = pl.num_programs(0) - 1

    @pl.when(step == 0)
    def _():
        qkv_sc[...] = jnp.zeros(qkv_sc.shape, F32)

    def finish_projection():
        bd = bd_ref[...]
        if rope:
            cos, sa, sb = cos_ref[...], sa_ref[...], sb_ref[...]

        def norm_rope_store(col0, n_slabs, gain, out):
            for s in range(0, n_slabs, 2):
                y = qkv_sc[:, col0 + s * LANES: col0 + (s + 2) * LANES]
                ss = jnp.dot((y * y).astype(BF16), bd, preferred_element_type=F32)
                y = y * lax.rsqrt(ss * (1.0 / HEAD_DIM) + EPS)
                for t in range(2):
                    z = y[:, t * LANES:(t + 1) * LANES] * gain
                    if rope:
                        z = (z * cos + pltpu.roll(z, LANES - 16, 1) * sa
                             + pltpu.roll(z, 16, 1) * sb)
                    out[0, s + t] = z.astype(BF16)

        norm_rope_store(0, n_q, gq_ref[...], q_out)
        norm_rope_store(n_q * LANES, n_k, gk_ref[...], k_out)

        v0 = (n_q + n_k) * LANES
        vt = qkv_sc[:, v0:v0 + v_cols].T.astype(BF16)
        chunk = vt_out.shape[3]
        for c in range(vt_out.shape[1]):
            vt_out[0, c] = vt[:, c * chunk:(c + 1) * chunk]

    def tile_matmuls():
        mod = mod_ref[0]
        x1 = _ffn_half_step(x_ref[0], mod[0:1], mod[1:2], mod[2:3], wi_ref, wo_ref)
        x_out[0] = x1
        h = _modulated(x1, mod[3:4], mod[4:5])
        qkv_sc[...] = jnp.dot(h, wqkv_ref[...], preferred_element_type=F32)

    @pl.when(step < last)
    def _():
        finish_projection()
        tile_matmuls()

    @pl.when(step == last)
    def _():
        finish_projection()


def _post_kernel(x_ref, o_ref, mod_ref, wout_ref, wi_ref, wo_ref, x_out):
    mod = mod_ref[0]
    attn = jnp.dot(o_ref[0], wout_ref[...], preferred_element_type=F32)
    x2 = x_ref[0] + mod[5:6] * attn
    x_out[0] = _ffn_half_step(x2, mod[6:7], mod[7:8], mod[8:9], wi_ref, wo_ref)


def _mod_spec(mods, layer, mod_row, wrap):
    _, _, n_mod, d = mods.shape
    return pl.BlockSpec((None, 1, n_mod, d),
                        wrap(lambda b, i: (layer, b if mod_row is None else mod_row, 0, 0)))


def _pre_call(x, mods, wi, wo, wqkv, bd, gq, gk, rope_tabs, *, layer, w_layer, mod_row,
              n_q, n_k, v_cols, tm, vt_chunk, name):
    bx, lx, d = x.shape
    rope = rope_tabs is not None
    nt = lx // tm
    n_tiles = bx * nt
    grid = (n_tiles + 1,)

    def cur(g):
        t = jnp.minimum(g, n_tiles - 1)
        return t // nt, t % nt

    def lag(g):
        t = jnp.maximum(g - 1, 0)
        return t // nt, t % nt

    def at_cur(f):
        return lambda g: f(*cur(g))

    def at_lag(f):
        return lambda g: f(*lag(g))

    in_specs = [
        pl.BlockSpec((1, tm, d), at_cur(lambda b, i: (b, i, 0))),
        _mod_spec(mods, layer, mod_row, at_cur),
        _resident_layer(wi, w_layer), _resident_layer(wo, w_layer), _resident(wqkv.shape),
        _resident(bd.shape), _resident(gq.shape), _resident(gk.shape),
    ]
    args = [x, mods, wi, wo, wqkv, bd, gq, gk]
    if rope:
        in_specs += [pl.BlockSpec((tm, LANES), at_lag(lambda b, i: (i, 0)))] * 3
        args += list(rope_tabs)
    if tm >= vt_chunk:
        vt_block = (1, tm // vt_chunk, v_cols, vt_chunk)
        vt_map = at_lag(lambda b, i: (b, i, 0, 0))
    else:
        per = vt_chunk // tm
        vt_block = (1, 1, v_cols, tm)
        vt_map = at_lag(lambda b, i: (b, i // per, 0, i % per))
    out_specs = [
        pl.BlockSpec((1, tm, d), at_cur(lambda b, i: (b, i, 0))),
        pl.BlockSpec((1, n_q, tm, LANES), at_lag(lambda b, i: (b, 0, i, 0))),
        pl.BlockSpec((1, n_k, tm, LANES), at_lag(lambda b, i: (b, 0, i, 0))),
        pl.BlockSpec(vt_block, vt_map),
    ]
    out_shape = [
        jax.ShapeDtypeStruct((bx, lx, d), F32),
        jax.ShapeDtypeStruct((bx, n_q, lx, LANES), BF16),
        jax.ShapeDtypeStruct((bx, n_k, lx, LANES), BF16),
        jax.ShapeDtypeStruct((bx, lx // vt_chunk, v_cols, vt_chunk), BF16),
    ]
    return pl.pallas_call(
        functools.partial(_pre_kernel, n_q=n_q, n_k=n_k, v_cols=v_cols, rope=rope),
        grid=grid, in_specs=in_specs, out_specs=out_specs, out_shape=out_shape,
        scratch_shapes=[pltpu.VMEM((tm, wqkv.shape[1]), F32)],
        compiler_params=pltpu.CompilerParams(
            dimension_semantics=("arbitrary",),
            vmem_limit_bytes=VMEM_LIMIT),
        name=name,
    )(*args)


def _post_call(x, o, mods, wout, wi, wo, *, layer, mod_row, tm, name):
    bx, lx, d = x.shape
    return pl.pallas_call(
        _post_kernel,
        grid=(bx, lx // tm),
        in_specs=[
            pl.BlockSpec((1, tm, d), lambda b, i: (b, i, 0)),
            pl.BlockSpec((1, tm, o.shape[2]), lambda b, i: (b, i, 0)),
            _mod_spec(mods, layer, mod_row, lambda f: f),
            _resident(wout.shape), _resident_layer(wi, layer), _resident_layer(wo, layer),
        ],
        out_specs=pl.BlockSpec((1, tm, d), lambda b, i: (b, i, 0)),
        out_shape=jax.ShapeDtypeStruct((bx, lx, d), F32),
        compiler_params=pltpu.CompilerParams(
            dimension_semantics=("arbitrary", "arbitrary"),
            vmem_limit_bytes=VMEM_LIMIT),
        name=name,
    )(x, o, mods, wout, wi, wo)


def _split_components(q_tile):
    qf = q_tile.astype(F32)
    lane = lax.broadcasted_iota(jnp.int32, qf.shape, 1)
    lo = jnp.where(lane < HEAD_DIM, qf, 0.0).astype(BF16)
    hi = jnp.where(lane >= HEAD_DIM, qf, 0.0).astype(BF16)
    return jnp.concatenate([lo, hi], axis=0)


def _attn_a_kernel(*refs, n_lat, n_cast, lam_init, stabilize):
    n_in = len(refs) - 1 - n_cast
    cast_in, cast_out = refs[n_in - n_cast:n_in], refs[n_in + 1:]
    refs = tuple(refs[:n_in - n_cast]) + (refs[n_in],)
    if n_lat:
        lam_ref, gain_ref, q_ref, k_ref, vt_ref, kc_ref, vtc_ref, o_ref = refs
    else:
        lam_ref, gain_ref, q_ref, kc_ref, vtc_ref, o_ref = refs
    tq = q_ref.shape[2]

    def finish(head, l_fin, acc):
        inv = 1.0 / l_fin
        lv = lam_ref[...]
        lam = (jnp.exp(jnp.sum(lv[0:1] * lv[1:2], axis=-1, keepdims=True))
               - jnp.exp(jnp.sum(lv[2:3] * lv[3:4], axis=-1, keepdims=True)) + lam_init)
        ot = acc[:, :tq] * inv[:, :tq] - lam * (acc[:, tq:] * inv[:, tq:])
        on = ot * lax.rsqrt(jnp.mean(ot * ot, axis=0, keepdims=True) + EPS)
        o = (on.T * (gain_ref[...] * (1.0 - lam_init))).astype(BF16)
        o_ref[0, :, head * LANES:(head + 1) * LANES] = o

    for w_in, w_out in zip(cast_in, cast_out):
        w_out[...] = w_in[...].astype(BF16)
    for head in range(q_ref.shape[1]):
        finish(head, *_attn_a_sums(head, q_ref, k_ref if n_lat else None,
                                   vt_ref if n_lat else None, kc_ref, vtc_ref, stabilize))


def _attn_a_sums(head, q_ref, k_ref, vt_ref, kc_ref, vtc_ref, stabilize):
    qq = _split_components(q_ref[0, head])
    v_rows = slice(head * LANES, (head + 1) * LANES)

    def scores(kb):
        return lax.dot_general(kb, qq, (((1,), (1,)), ((), ())), preferred_element_type=F32)

    def update_unshifted(state, st, vtb):
        p = jnp.exp2(st)
        l_new = jnp.sum(p, axis=0, keepdims=True)
        acc_new = jnp.dot(vtb, p.astype(BF16), preferred_element_type=F32)
        if state is not None:
            l_new = state[1] + l_new
            acc_new = state[2] + acc_new
        return None, l_new, acc_new

    def update(state, st, vtb):
        if not stabilize:
            return update_unshifted(state, st, vtb)
        m_blk = jnp.max(st, axis=0, keepdims=True)
        if state is None:
            m_new = m_blk
        else:
            m_old, l_old, acc_old = state
            m_new = jnp.maximum(m_old, m_blk)
            alpha = jnp.exp2(m_old - m_new)
        p = jnp.exp2(st - m_new)
        l_new = jnp.sum(p, axis=0, keepdims=True)
        acc_new = jnp.dot(vtb, p.astype(BF16), preferred_element_type=F32)
        if state is not None:
            l_new = alpha * l_old + l_new
            acc_new = alpha * acc_old + acc_new
        return m_new, l_new, acc_new

    blocks = []
    if k_ref is not None:
        tkv = vt_ref.shape[3]
        for j in range(vt_ref.shape[1]):
            blocks.append((lambda j=j: k_ref[0, head, j * tkv:(j + 1) * tkv, :],
                           lambda j=j: vt_ref[0, j, v_rows, :]))
    ckv = vtc_ref.shape[3]
    for c in range(vtc_ref.shape[1]):
        blocks.append((lambda c=c: kc_ref[0, head, c * ckv:(c + 1) * ckv, :],
                       lambda c=c: vtc_ref[0, c, v_rows, :]))
    state = None
    st = scores(blocks[0][0]())
    for j, (_, load_vt) in enumerate(blocks):
        st_next = scores(blocks[j + 1][0]()) if j + 1 < len(blocks) else None
        state = update(state, st, load_vt())
        st = st_next
    _, l_fin, acc = state
    return l_fin, acc


def _cast_views(stacks):
    return [w.reshape(-1, w.shape[-1]) for w in stacks]


def _cast_row_block(n_rows, n_steps):
    for n_blocks in range(n_steps, 0, -1):
        if n_steps % n_blocks == 0 and n_rows % (BF16_SUBLANES * n_blocks) == 0:
            return n_rows // n_blocks, n_steps // n_blocks
    raise ValueError("rows do not split into bf16 sublane tiles")


def _attn_a_call(lam_vec, gain, q, k, vt, kc, vtc, *cast_views, cast_rows=(), tq, heads_per_step,
                 lam_init, stabilize, name):
    b, h, lq, _ = q.shape
    n_lat = 0 if k is None else vt.shape[1]
    nq = lq // tq
    hps = heads_per_step
    nh = h // hps
    n_tiles = b * nh * nq

    def at_cur(f):
        return lambda g: f(g // (nh * nq), (g // nq) % nh, g % nq)

    in_specs = [
        pl.BlockSpec(lam_vec.shape, lambda g: (0, 0)),
        pl.BlockSpec(gain.shape, lambda g: (0, 0)),
        pl.BlockSpec((1, hps, tq, LANES), at_cur(lambda b_, h_, i: (b_, h_, i, 0))),
    ]
    args = [lam_vec, gain, q]
    if n_lat:
        in_specs += [
            pl.BlockSpec((1, hps, k.shape[2], LANES), at_cur(lambda b_, h_, i: (b_, h_, 0, 0))),
            pl.BlockSpec((1, vt.shape[1], hps * LANES, vt.shape[3]),
                         at_cur(lambda b_, h_, i: (b_, 0, h_, 0))),
        ]
        args += [k, vt]
    in_specs += [
        pl.BlockSpec((1, hps, kc.shape[2], LANES), at_cur(lambda b_, h_, i: (b_, h_, 0, 0))),
        pl.BlockSpec((1, vtc.shape[1], hps * LANES, vtc.shape[3]),
                     at_cur(lambda b_, h_, i: (b_, 0, h_, 0))),
    ]
    args += [kc, vtc]
    cast_in, cast_out, cast_shapes = [], [], []
    for w, (row0, n_rows) in zip(cast_views, cast_rows):
        rb, reps = _cast_row_block(n_rows, n_tiles)
        assert row0 % rb == 0

        def block(g, first=row0 // rb, reps=reps):
            return (first + g // reps, 0)

        cast_in.append(pl.BlockSpec((rb, w.shape[1]), block))
        cast_out.append(pl.BlockSpec((rb, w.shape[1]), functools.partial(block, first=0)))
        cast_shapes.append(jax.ShapeDtypeStruct((n_rows, w.shape[1]), BF16))
    outs = pl.pallas_call(
        functools.partial(_attn_a_kernel, n_lat=n_lat, n_cast=len(cast_views),
                          lam_init=lam_init, stabilize=stabilize),
        grid=(n_tiles,),
        in_specs=in_specs + cast_in,
        out_specs=[pl.BlockSpec((1, tq, hps * LANES),
                                at_cur(lambda b_, h_, i: (b_, i, h_)))] + cast_out,
        out_shape=[jax.ShapeDtypeStruct((b, lq, h * LANES), BF16)] + cast_shapes,
        compiler_params=pltpu.CompilerParams(
            dimension_semantics=("arbitrary",),
            vmem_limit_bytes=VMEM_LIMIT),
        name=name + ("_shifted" if stabilize else ""),
    )(*args, *cast_views)
    return outs if cast_views else outs[0]


def _attn_b_kernel(sink_ref, q_ref, kp_ref, km_ref, kn_ref, kc_ref,
                   vp_ref, vm_ref, vn_ref, vc_ref, o_ref, *, stabilize):
    n = pl.program_id(1)
    n_steps = pl.num_programs(1)
    bb = WINDOW_BLOCK
    n_pair = kp_ref.shape[1]
    qb = vm_ref.shape[1]
    ncol = 2 * B_GROUP * bb
    half = ncol // 2

    key_j = lax.broadcasted_iota(jnp.int32, (bb, ncol), 0)
    qry_i = lax.broadcasted_iota(jnp.int32, (bb, ncol), 1) & (bb - 1)
    tri_prev = key_j >= qry_i
    tri_next = key_j <= qry_i
    edge_prev = key_j >= qry_i + jnp.where(n > 0, 0, bb)
    edge_next = key_j <= qry_i - jnp.where(n < n_steps - 1, 0, bb)

    def chains(p):
        rows = slice(p * LANES, (p + 1) * LANES)
        k_blk = lambda j: km_ref[0, p, j * bb:(j + 1) * bb, :]
        v_blk = lambda j: vm_ref[0, j, rows, :]
        for j in range(qb):
            first, final = j == 0, j == qb - 1
            yield (slice(j * bb, (j + 1) * bb),
                   kp_ref[0, p] if first else k_blk(j - 1), k_blk(j),
                   kn_ref[0, p] if final else k_blk(j + 1),
                   vp_ref[0, 0, rows, :] if first else v_blk(j - 1), v_blk(j),
                   vn_ref[0, 0, rows, :] if final else v_blk(j + 1),
                   edge_prev if first else tri_prev, edge_next if final else tri_next)

    def scores(p, chain):
        q_rows, k_prev, k_own, k_next = chain[:4]
        slabs = [q_ref[0, p * B_GROUP + g, q_rows, :].astype(F32) for g in range(B_GROUP)]
        lane = lax.broadcasted_iota(jnp.int32, slabs[0].shape, 1)
        lo = [jnp.where(lane < HEAD_DIM, s, 0.0).astype(BF16) for s in slabs]
        hi = [jnp.where(lane >= HEAD_DIM, s, 0.0).astype(BF16) for s in slabs]
        qq = jnp.concatenate(lo + hi, axis=0)
        kcat = jnp.concatenate([k_prev, k_own, k_next, kc_ref[0, p]], axis=0)
        return lax.dot_general(kcat, qq, (((1,), (1,)), ((), ())), preferred_element_type=F32)

    work = [(p, chain) for p in range(n_pair) for chain in chains(p)]
    st_next = scores(*work[0])
    for idx, (p, chain) in enumerate(work):
        st = st_next
        st_next = scores(*work[idx + 1]) if idx + 1 < len(work) else None
        q_rows, _, _, _, v_prev, v_own, v_next, in_prev, in_next = chain
        sink = sink_ref[p]
        if stabilize:
            s_prev = jnp.where(in_prev, st[0:bb], NEG_INF)
            s_own = st[bb:2 * bb]
            s_next = jnp.where(in_next, st[2 * bb:3 * bb], NEG_INF)
            s_ctx = st[3 * bb:]
            m = jnp.maximum(jnp.maximum(jnp.max(s_prev, axis=0, keepdims=True),
                                        jnp.max(s_own, axis=0, keepdims=True)),
                            jnp.maximum(jnp.max(s_next, axis=0, keepdims=True),
                                        jnp.max(s_ctx, axis=0, keepdims=True)))
            m = jnp.maximum(m, sink)
            parts = [jnp.exp2(s - m) for s in (s_prev, s_own, s_next, s_ctx)]
            l = jnp.exp2(sink - m)
        else:
            parts = [jnp.where(in_prev, jnp.exp2(st[0:bb]), 0.0),
                     jnp.exp2(st[bb:2 * bb]),
                     jnp.where(in_next, jnp.exp2(st[2 * bb:3 * bb]), 0.0),
                     jnp.exp2(st[3 * bb:])]
            l = jnp.exp2(sink)
        for part in parts:
            l = l + jnp.sum(part, axis=0, keepdims=True)
        pt = jnp.concatenate([part.astype(BF16) for part in parts], axis=0)
        rows = slice(p * LANES, (p + 1) * LANES)
        vcat = jnp.concatenate([v_prev, v_own, v_next]
                               + [vc_ref[0, c, rows, :] for c in range(vc_ref.shape[1])], axis=1)
        inv = 1.0 / l
        for e in range(2):
            ot = jnp.dot(vcat[e * HEAD_DIM:(e + 1) * HEAD_DIM], pt[:, e * half:(e + 1) * half],
                         preferred_element_type=F32) * inv[:, e * half:(e + 1) * half]
            for gp in range(B_GROUP // 2):
                blk = jnp.concatenate([ot[:, (2 * gp) * bb:(2 * gp + 1) * bb],
                                       ot[:, (2 * gp + 1) * bb:(2 * gp + 2) * bb]], axis=0)
                c0 = ((p * 2 + e) * (B_GROUP // 2) + gp) * LANES
                o_ref[0, q_rows, c0:c0 + LANES] = blk.T.astype(BF16)


def _attn_b_call(sink_rows, q, k, vt, kc, vtc, *, stabilize, name):
    b, n_slab, l, _ = q.shape
    n_pair = k.shape[1]
    bb = WINDOW_BLOCK
    nb = l // bb
    qb = B_QUERY_BLOCKS
    assert nb % qb == 0 and vt.shape[3] == bb
    prev = lambda n: jnp.maximum(qb * n - 1, 0)
    nxt = lambda n: jnp.minimum(qb * n + qb, nb - 1)
    kspec = lambda f: pl.BlockSpec((1, n_pair, bb, LANES), lambda b_, n: (b_, 0, f(n), 0))
    vspec = lambda f: pl.BlockSpec((1, 1, vt.shape[2], bb), lambda b_, n: (b_, f(n), 0, 0))
    whole = lambda a: pl.BlockSpec((1,) + a.shape[1:], lambda b_, n: (b_,) + (0,) * (a.ndim - 1))
    return pl.pallas_call(
        functools.partial(_attn_b_kernel, stabilize=stabilize),
        grid=(b, nb // qb),
        in_specs=[
            pl.BlockSpec(sink_rows.shape, lambda b_, n: (0, 0, 0)),
            pl.BlockSpec((1, n_slab, qb * bb, LANES), lambda b_, n: (b_, 0, n, 0)),
            kspec(prev),
            pl.BlockSpec((1, n_pair, qb * bb, LANES), lambda b_, n: (b_, 0, n, 0)),
            kspec(nxt), whole(kc),
            vspec(prev),
            pl.BlockSpec((1, qb, vt.shape[2], bb), lambda b_, n: (b_, n, 0, 0)),
            vspec(nxt), whole(vtc),
        ],
        out_specs=pl.BlockSpec((1, qb * bb, n_slab * LANES), lambda b_, n: (b_, n, 0)),
        out_shape=jax.ShapeDtypeStruct((b, l, n_slab * LANES), BF16),
        compiler_params=pltpu.CompilerParams(
            dimension_semantics=("arbitrary", "arbitrary"),
            vmem_limit_bytes=VMEM_LIMIT),
        name=name + ("_shifted" if stabilize else ""),
    )(sink_rows, q, k, k, k, kc, vt, vt, vt, vtc)


def _rope_tables(n_tok):
    rows = n_tok // GRID_W
    nf = HEAD_DIM // 4
    inv = (np.float32(ROPE_BASE) ** (-np.arange(nf, dtype=np.float32) / np.float32(nf)))
    row = np.repeat(np.arange(rows, dtype=np.float32), GRID_W)
    col = np.tile(np.arange(GRID_W, dtype=np.float32), rows)
    ang = np.stack([row[:, None] * inv, col[:, None] * inv], axis=1)
    ang = np.stack([ang, ang], axis=2).reshape(n_tok, HEAD_DIM).astype(np.float32)
    cos = np.tile(np.cos(ang), (1, LANES // HEAD_DIM)).astype(np.float32)
    sin = np.tile(np.sin(ang), (1, LANES // HEAD_DIM)).astype(np.float32)
    first_half = (np.arange(LANES) % (2 * nf)) < nf
    zero = np.float32(0.0)
    return (jnp.asarray(cos), jnp.asarray(np.where(first_half, -sin, zero)),
            jnp.asarray(np.where(first_half, zero, sin)))


def _group_sum_matrix():
    idx = np.arange(MXU_COLS) // HEAD_DIM
    return jnp.asarray(idx[:, None] == idx[None, :], dtype=BF16)


def _score_bound(gq_row, gk_row):
    return 1.05 * HEAD_DIM * jnp.max(jnp.abs(gq_row)) * jnp.max(jnp.abs(gk_row))


def _pair_q_heads(w_qkv):
    order = []
    for p in range(B_KV_HEADS // 2):
        for g in range(B_GROUP):
            order += [(2 * p) * B_GROUP + g, (2 * p + 1) * B_GROUP + g]
    d = w_qkv.shape[0]
    wq = w_qkv[:, :B_Q_HEADS * HEAD_DIM].reshape(d, B_Q_HEADS, HEAD_DIM)
    wq = wq[:, jnp.array(order)].reshape(d, B_Q_HEADS * HEAD_DIM)
    return jnp.concatenate([wq, w_qkv[:, B_Q_HEADS * HEAD_DIM:]], axis=1)


def kernel(x, c, ctx, c_ctx, ada_w, ada_b, ffn_pre_wi, ffn_pre_wo, ffn_post_wi, ffn_post_wo,
           a_w_qkv, a_w_o, a_q_gain, a_k_gain, a_lambda, a_subln_gain,
           b_w_qkv, b_w_o, b_q_gain, b_k_gain, b_sink):
    bsz, seq, d = x.shape
    n_ctx = ctx.shape[1]
    depth = ada_w.shape[0]
    assert depth == 2 and seq % WINDOW_BLOCK == 0

    rows = 8
    cs = jnp.concatenate([c, c_ctx[None], jnp.zeros((rows - bsz - 1, d), F32)], axis=0)
    mods = _ada(cs, ada_w, ada_b).reshape(depth, rows, N_MOD, d)
    lat, ctx_row = None, bsz

    rope_tabs = _rope_tables(seq)
    bd = _group_sum_matrix()
    qk_scale = HEAD_DIM ** -0.5 * LOG2E
    tile2 = lambda g: jnp.tile(g.astype(F32), LANES // HEAD_DIM).reshape(1, LANES)

    pre_wi = _to_bf16(ffn_pre_wi, layer=0, name="cast_pre_wi")
    pre_wo = _to_bf16(ffn_pre_wo, layer=0, name="cast_pre_wo")

    tm = 512
    tm_post = 512
    tkv = 512
    tq = 2048

    lam_init = 0.8 - 0.6 * math.exp(-0.3 * 0)
    a_cfg = dict(n_q=A_HEADS, n_k=A_HEADS, v_cols=A_HEADS * LANES)
    wqkv_a = _to_bf16(a_w_qkv, layer=0, name="cast_wqkv_a")[0]
    gq_a, gk_a = tile2(a_q_gain[0]) * qk_scale, tile2(a_k_gain[0])
    x1, q, k, vt = _pre_call(x, mods, pre_wi, pre_wo, wqkv_a, bd, gq_a, gk_a, rope_tabs,
                             layer=0, w_layer=0, mod_row=lat, tm=tm, vt_chunk=tkv,
                             name="pre0_lat", **a_cfg)
    xc1, qc, kc, vtc = _pre_call(ctx, mods, pre_wi, pre_wo, wqkv_a, bd, gq_a, gk_a, None,
                                 layer=0, w_layer=0, mod_row=ctx_row, tm=n_ctx, vt_chunk=n_ctx,
                                 name="pre0_ctx", **a_cfg)
    sub_gain = a_subln_gain[0].astype(F32).reshape(1, LANES)
    lam_vec = a_lambda[0].astype(F32)
    unshifted_ok = _score_bound(gq_a, gk_a) <= UNSHIFTED_SCORE_LIMIT

    views = _cast_views([ffn_post_wi, ffn_post_wo, ffn_pre_wi, ffn_pre_wo, a_w_o, b_w_o])
    wi_rows, wo_rows = ffn_pre_wi.shape[1], ffn_pre_wo.shape[1]
    cast_rows = [(0, depth * wi_rows), (0, depth * wo_rows), (wi_rows, wi_rows), (wo_rows, wo_rows),
                 (0, a_w_o.shape[1]), (0, b_w_o.shape[1])]

    def attn_a(stabilize):
        def run(q, k, vt, qc, kc, vtc, *views):
            o, *cast = _attn_a_call(lam_vec, sub_gain, q, k, vt, kc, vtc, *views,
                                    cast_rows=cast_rows, tq=tq, heads_per_step=1,
                                    lam_init=lam_init, stabilize=stabilize, name="attn_a_lat")
            oc = _attn_a_call(lam_vec, sub_gain, qc, None, None, kc, vtc, tq=n_ctx,
                              heads_per_step=A_HEADS, lam_init=lam_init, stabilize=stabilize,
                              name="attn_a_ctx")
            return (o, oc, *cast)
        return run

    o, oc, post_wi, post_wo, pre_wi1, pre_wo1, wout_a, wout_b = lax.cond(
        unshifted_ok, attn_a(False), attn_a(True), q, k, vt, qc, kc, vtc, *views)
    post_wi = post_wi.reshape(ffn_post_wi.shape)
    post_wo = post_wo.reshape(ffn_post_wo.shape)
    pre_wi1 = pre_wi1.reshape((1,) + ffn_pre_wi.shape[1:])
    pre_wo1 = pre_wo1.reshape((1,) + ffn_pre_wo.shape[1:])
    x2 = _post_call(x1, o, mods, wout_a, post_wi, post_wo, layer=0, mod_row=lat, tm=tm_post,
                    name="post0_lat")
    xc2 = _post_call(xc1, oc, mods, wout_a, post_wi, post_wo, layer=0, mod_row=ctx_row,
                     tm=n_ctx, name="post0_ctx")

    b_cfg = dict(n_q=B_Q_HEADS // 2, n_k=B_KV_HEADS // 2, v_cols=B_KV_HEADS * HEAD_DIM)
    wqkv_b = _pair_q_heads(b_w_qkv[0]).astype(BF16)
    gq_b, gk_b = tile2(b_q_gain[0]) * qk_scale, tile2(b_k_gain[0])
    x3, q, k, vt = _pre_call(x2, mods, pre_wi1, pre_wo1, wqkv_b, bd, gq_b, gk_b, rope_tabs,
                             layer=1, w_layer=0, mod_row=lat, tm=tm, vt_chunk=WINDOW_BLOCK,
                             name="pre1_lat", **b_cfg)
    _, _, kc, vtc = _pre_call(xc2, mods, pre_wi1, pre_wo1, wqkv_b, bd, gq_b, gk_b, None,
                              layer=1, w_layer=0, mod_row=ctx_row, tm=n_ctx,
                              vt_chunk=WINDOW_BLOCK, name="pre1_ctx", **b_cfg)
    sink = (b_sink[0].astype(F32) * LOG2E).reshape(B_KV_HEADS // 2, 2 * B_GROUP, 1)
    sink_rows = jnp.broadcast_to(sink, (B_KV_HEADS // 2, 2 * B_GROUP, WINDOW_BLOCK))
    sink_rows = sink_rows.reshape(B_KV_HEADS // 2, 1, 2 * B_GROUP * WINDOW_BLOCK)
    logit_bound = jnp.maximum(_score_bound(gq_b, gk_b), jnp.max(jnp.abs(sink_rows)))
    o = lax.cond(
        logit_bound <= UNSHIFTED_SCORE_LIMIT,
        functools.partial(_attn_b_call, stabilize=False, name="attn_b"),
        functools.partial(_attn_b_call, stabilize=True, name="attn_b"),
        sink_rows, q, k, vt, kc, vtc)
    x4 = _post_call(x3, o, mods, wout_b, post_wi, post_wo, layer=1,
                    mod_row=lat, tm=tm_post, name="post1_lat")
    return x4
```

```python
import functools
import math

import jax
import jax.numpy as jnp
import numpy as np
from jax import lax
from jax.experimental import pallas as pl
from jax.experimental.pallas import tpu as pltpu

F32 = jnp.float32
BF16 = jnp.bfloat16

LANES = 128
BF16_SUBLANES = 16
MXU_COLS = 256
HEAD_DIM = 64
N_MOD = 9
EPS = 1e-6
ROPE_BASE = 10000.0
GRID_W = 64
WINDOW_BLOCK = 128
NEG_INF = -1e30
LOG2E = math.log2(math.e)
UNSHIFTED_SCORE_LIMIT = 96.0
VMEM_LIMIT = 56 * 1024 * 1024
CAST_BLOCK_BYTES = 6 * 1024 * 1024

A_HEADS = 8
B_Q_HEADS = 16
B_KV_HEADS = 4
B_GROUP = B_Q_HEADS // B_KV_HEADS
B_QUERY_BLOCKS = 8


def _resident(shape):
    nd = len(shape)
    return pl.BlockSpec(shape, lambda *_: (0,) * nd, pipeline_mode=pl.Buffered(1))


def _resident_layer(stack, layer):
    _, rows, cols = stack.shape
    return pl.BlockSpec((None, rows, cols), lambda *_: (layer, 0, 0),
                        pipeline_mode=pl.Buffered(1))


def _cast_kernel(w_ref, o_ref):
    o_ref[...] = w_ref[...].astype(BF16)


def _to_bf16(w, *, layer, name):
    _, rows, cols = w.shape
    rb = max(r for r in range(8, rows + 1, 8)
             if rows % r == 0 and r * cols * 4 <= CAST_BLOCK_BYTES)
    return pl.pallas_call(
        _cast_kernel, grid=(rows // rb,),
        in_specs=[pl.BlockSpec((1, rb, cols), lambda r: (layer, r, 0))],
        out_specs=pl.BlockSpec((1, rb, cols), lambda r: (0, r, 0)),
        out_shape=jax.ShapeDtypeStruct((1, rows, cols), BF16),
        compiler_params=pltpu.CompilerParams(
            dimension_semantics=("arbitrary",),
            vmem_limit_bytes=VMEM_LIMIT),
        name=name,
    )(w)


def _ada_kernel(cs_ref, w_ref, b_ref, out_ref):
    s = cs_ref[...]
    s = s * jax.nn.sigmoid(s)
    out_ref[0] = jnp.dot(s.astype(BF16), w_ref[0].astype(BF16),
                         preferred_element_type=F32) + b_ref[0]


def _ada(cs, ada_w, ada_b):
    depth, d, n = ada_w.shape
    rows = cs.shape[0]
    tn = n // 4
    return pl.pallas_call(
        _ada_kernel,
        grid=(depth, n // tn),
        in_specs=[
            pl.BlockSpec((rows, d), lambda i, j: (0, 0)),
            pl.BlockSpec((1, d, tn), lambda i, j: (i, 0, j)),
            pl.BlockSpec((1, 1, tn), lambda i, j: (i, 0, j)),
        ],
        out_specs=pl.BlockSpec((1, rows, tn), lambda i, j: (i, 0, j)),
        out_shape=jax.ShapeDtypeStruct((depth, rows, n), F32),
        compiler_params=pltpu.CompilerParams(
            dimension_semantics=("arbitrary", "arbitrary"),
            vmem_limit_bytes=VMEM_LIMIT),
        name="ada_mod",
    )(cs, ada_w, ada_b.reshape(depth, 1, n))


def _rms(x):
    return x * lax.rsqrt(jnp.mean(x * x, axis=-1, keepdims=True) + EPS)


def _modulated(x, shift, scale):
    return (_rms(x) * (1.0 + scale) + shift).astype(BF16)


def _ffn_half_step(x, shift, scale, gate, wi_ref, wo_ref):
    d_ff = wo_ref.shape[0]
    h = _modulated(x, shift, scale)
    gu = jnp.dot(h, wi_ref[...], preferred_element_type=F32)
    g = gu[:, :d_ff]
    u = gu[:, d_ff:]
    act = (g * jax.nn.sigmoid(g) * u).astype(BF16)
    ff = jnp.dot(act, wo_ref[...], preferred_element_type=F32)
    return x + (0.5 * gate) * ff


def _pre_kernel(*refs, n_q, n_k, v_cols, rope):
    if rope:
        (x_ref, mod_ref, wi_ref, wo_ref, wqkv_ref, bd_ref, gq_ref, gk_ref,
         cos_ref, sa_ref, sb_ref, x_out, q_out, k_out, vt_out, qkv_sc) = refs
    else:
        (x_ref, mod_ref, wi_ref, wo_ref, wqkv_ref, bd_ref, gq_ref, gk_ref,
         x_out, q_out, k_out, vt_out, qkv_sc) = refs

    step = pl.program_id(0)
    last = pl.num_programs(0) - 1

    @pl.when(step == 0)
    def _():
        qkv_sc[...] = jnp.zeros(qkv_sc.shape, F32)

    def finish_projection():
        bd = bd_ref[...]
        if rope:
            cos, sa, sb = cos_ref[...], sa_ref[...], sb_ref[...]

        def norm_rope_store(col0, n_slabs, gain, out):
            for s in range(0, n_slabs, 2):
                y = qkv_sc[:, col0 + s * LANES: col0 + (s + 2) * LANES]
                ss = jnp.dot((y * y).astype(BF16), bd, preferred_element_type=F32)
                y = y * lax.rsqrt(ss * (1.0 / HEAD_DIM) + EPS)
                for t in range(2):
                    z = y[:, t * LANES:(t + 1) * LANES] * gain
                    if rope:
                        z = (z * cos + pltpu.roll(z, LANES - 16, 1) * sa
                             + pltpu.roll(z, 16, 1) * sb)
                    out[0, s + t] = z.astype(BF16)

        norm_rope_store(0, n_q, gq_ref[...], q_out)
        norm_rope_store(n_q * LANES, n_k, gk_ref[...], k_out)

        v0 = (n_q + n_k) * LANES
        vt = qkv_sc[:, v0:v0 + v_cols].T.astype(BF16)
        chunk = vt_out.shape[3]
        for c in range(vt_out.shape[1]):
            vt_out[0, c] = vt[:, c * chunk:(c + 1) * chunk]

    def tile_matmuls():
        mod = mod_ref[0]
        x1 = _ffn_half_step(x_ref[0], mod[0:1], mod[1:2], mod[2:3], wi_ref, wo_ref)
        x_out[0] = x1
        h = _modulated(x1, mod[3:4], mod[4:5])
        qkv_sc[...] = jnp.dot(h, wqkv_ref[...], preferred_element_type=F32)

    @pl.when(step < last)
    def _():
        finish_projection()
        tile_matmuls()

    @pl.when(step == last)
    def _():
        finish_projection()


def _post_kernel(x_ref, o_ref, mod_ref, wout_ref, wi_ref, wo_ref, x_out):
    mod = mod_ref[0]
    attn = jnp.dot(o_ref[0], wout_ref[...], preferred_element_type=F32)
    x2 = x_ref[0] + mod[5:6] * attn
    x_out[0] = _ffn_half_step(x2, mod[6:7], mod[7:8], mod[8:9], wi_ref, wo_ref)


def _mod_spec(mods, layer, mod_row, wrap):
    _, _, n_mod, d = mods.shape
    return pl.BlockSpec((None, 1, n_mod, d),
                        wrap(lambda b, i: (layer, b if mod_row is None else mod_row, 0, 0)))


def _pre_call(x, mods, wi, wo, wqkv, bd, gq, gk, rope_tabs, *, layer, w_layer, mod_row,
              n_q, n_k, v_cols, tm, vt_chunk, name):
    bx, lx, d = x.shape
    rope = rope_tabs is not None
    nt = lx // tm
    n_tiles = bx * nt
    grid = (n_tiles + 1,)

    def cur(g):
        t = jnp.minimum(g, n_tiles - 1)
        return t // nt, t % nt

    def lag(g):
        t = jnp.maximum(g - 1, 0)
        return t // nt, t % nt

    def at_cur(f):
        return lambda g: f(*cur(g))

    def at_lag(f):
        return lambda g: f(*lag(g))

    in_specs = [
        pl.BlockSpec((1, tm, d), at_cur(lambda b, i: (b, i, 0))),
        _mod_spec(mods, layer, mod_row, at_cur),
        _resident_layer(wi, w_layer), _resident_layer(wo, w_layer), _resident(wqkv.shape),
        _resident(bd.shape), _resident(gq.shape), _resident(gk.shape),
    ]
    args = [x, mods, wi, wo, wqkv, bd, gq, gk]
    if rope:
        in_specs += [pl.BlockSpec((tm, LANES), at_lag(lambda b, i: (i, 0)))] * 3
        args += list(rope_tabs)
    if tm >= vt_chunk:
        vt_block = (1, tm // vt_chunk, v_cols, vt_chunk)
        vt_map = at_lag(lambda b, i: (b, i, 0, 0))
    else:
        per = vt_chunk // tm
        vt_block = (1, 1, v_cols, tm)
        vt_map = at_lag(lambda b, i: (b, i // per, 0, i % per))
    out_specs = [
        pl.BlockSpec((1, tm, d), at_cur(lambda b, i: (b, i, 0))),
        pl.BlockSpec((1, n_q, tm, LANES), at_lag(lambda b, i: (b, 0, i, 0))),
        pl.BlockSpec((1, n_k, tm, LANES), at_lag(lambda b, i: (b, 0, i, 0))),
        pl.BlockSpec(vt_block, vt_map),
    ]
    out_shape = [
        jax.ShapeDtypeStruct((bx, lx, d), F32),
        jax.ShapeDtypeStruct((bx, n_q, lx, LANES), BF16),
        jax.ShapeDtypeStruct((bx, n_k, lx, LANES), BF16),
        jax.ShapeDtypeStruct((bx, lx // vt_chunk, v_cols, vt_chunk), BF16),
    ]
    return pl.pallas_call(
        functools.partial(_pre_kernel, n_q=n_q, n_k=n_k, v_cols=v_cols, rope=rope),
        grid=grid, in_specs=in_specs, out_specs=out_specs, out_shape=out_shape,
        scratch_shapes=[pltpu.VMEM((tm, wqkv.shape[1]), F32)],
        compiler_params=pltpu.CompilerParams(
            dimension_semantics=("arbitrary",),
            vmem_limit_bytes=VMEM_LIMIT),
        name=name,
    )(*args)


def _post_call(x, o, mods, wout, wi, wo, *, layer, mod_row, tm, name):
    bx, lx, d = x.shape
    return pl.pallas_call(
        _post_kernel,
        grid=(bx, lx // tm),
        in_specs=[
            pl.BlockSpec((1, tm, d), lambda b, i: (b, i, 0)),
            pl.BlockSpec((1, tm, o.shape[2]), lambda b, i: (b, i, 0)),
            _mod_spec(mods, layer, mod_row, lambda f: f),
            _resident(wout.shape), _resident_layer(wi, layer), _resident_layer(wo, layer),
        ],
        out_specs=pl.BlockSpec((1, tm, d), lambda b, i: (b, i, 0)),
        out_shape=jax.ShapeDtypeStruct((bx, lx, d), F32),
        compiler_params=pltpu.CompilerParams(
            dimension_semantics=("arbitrary", "arbitrary"),
            vmem_limit_bytes=VMEM_LIMIT),
        name=name,
    )(x, o, mods, wout, wi, wo)


def _split_components(q_tile):
    qf = q_tile.astype(F32)
    lane = lax.broadcasted_iota(jnp.int32, qf.shape, 1)
    lo = jnp.where(lane < HEAD_DIM, qf, 0.0).astype(BF16)
    hi = jnp.where(lane >= HEAD_DIM, qf, 0.0).astype(BF16)
    return jnp.concatenate([lo, hi], axis=0)


def _attn_a_kernel(*refs, n_lat, n_cast, lam_init, stabilize):
    n_in = len(refs) - 1 - n_cast
    cast_in, cast_out = refs[n_in - n_cast:n_in], refs[n_in + 1:]
    refs = tuple(refs[:n_in - n_cast]) + (refs[n_in],)
    if n_lat:
        lam_ref, gain_ref, q_ref, k_ref, vt_ref, kc_ref, vtc_ref, o_ref = refs
    else:
        lam_ref, gain_ref, q_ref, kc_ref, vtc_ref, o_ref = refs
    tq = q_ref.shape[2]

    def finish(head, l_fin, acc):
        inv = 1.0 / l_fin
        lv = lam_ref[...]
        lam = (jnp.exp(jnp.sum(lv[0:1] * lv[1:2], axis=-1, keepdims=True))
               - jnp.exp(jnp.sum(lv[2:3] * lv[3:4], axis=-1, keepdims=True)) + lam_init)
        ot = acc[:, :tq] * inv[:, :tq] - lam * (acc[:, tq:] * inv[:, tq:])
        on = ot * lax.rsqrt(jnp.mean(ot * ot, axis=0, keepdims=True) + EPS)
        o = (on.T * (gain_ref[...] * (1.0 - lam_init))).astype(BF16)
        o_ref[0, :, head * LANES:(head + 1) * LANES] = o

    for w_in, w_out in zip(cast_in, cast_out):
        w_out[...] = w_in[...].astype(BF16)
    for head in range(q_ref.shape[1]):
        finish(head, *_attn_a_sums(head, q_ref, k_ref if n_lat else None,
                                   vt_ref if n_lat else None, kc_ref, vtc_ref, stabilize))


def _attn_a_sums(head, q_ref, k_ref, vt_ref, kc_ref, vtc_ref, stabilize):
    qq = _split_components(q_ref[0, head])
    v_rows = slice(head * LANES, (head + 1) * LANES)

    def scores(kb):
        return lax.dot_general(kb, qq, (((1,), (1,)), ((), ())), preferred_element_type=F32)

    def update_unshifted(state, st, vtb):
        p = jnp.exp2(st)
        l_new = jnp.sum(p, axis=0, keepdims=True)
        acc_new = jnp.dot(vtb, p.astype(BF16), preferred_element_type=F32)
        if state is not None:
            l_new = state[1] + l_new
            acc_new = state[2] + acc_new
        return None, l_new, acc_new

    def update(state, st, vtb):
        if not stabilize:
            return update_unshifted(state, st, vtb)
        m_blk = jnp.max(st, axis=0, keepdims=True)
        if state is None:
            m_new = m_blk
        else:
            m_old, l_old, acc_old = state
            m_new = jnp.maximum(m_old, m_blk)
            alpha = jnp.exp2(m_old - m_new)
        p = jnp.exp2(st - m_new)
        l_new = jnp.sum(p, axis=0, keepdims=True)
        acc_new = jnp.dot(vtb, p.astype(BF16), preferred_element_type=F32)
        if state is not None:
            l_new = alpha * l_old + l_new
            acc_new = alpha * acc_old + acc_new
        return m_new, l_new, acc_new

    blocks = []
    if k_ref is not None:
        tkv = vt_ref.shape[3]
        for j in range(vt_ref.shape[1]):
            blocks.append((lambda j=j: k_ref[0, head, j * tkv:(j + 1) * tkv, :],
                           lambda j=j: vt_ref[0, j, v_rows, :]))
    ckv = vtc_ref.shape[3]
    for c in range(vtc_ref.shape[1]):
        blocks.append((lambda c=c: kc_ref[0, head, c * ckv:(c + 1) * ckv, :],
                       lambda c=c: vtc_ref[0, c, v_rows, :]))
    state = None
    st = scores(blocks[0][0]())
    for j, (_, load_vt) in enumerate(blocks):
        st_next = scores(blocks[j + 1][0]()) if j + 1 < len(blocks) else None
        state = update(state, st, load_vt())
        st = st_next
    _, l_fin, acc = state
    return l_fin, acc


def _cast_views(stacks):
    return [w.reshape(-1, w.shape[-1]) for w in stacks]


def _cast_row_block(n_rows, n_steps):
    for n_blocks in range(n_steps, 0, -1):
        if n_steps % n_blocks == 0 and n_rows % (BF16_SUBLANES * n_blocks) == 0:
            return n_rows // n_blocks, n_steps // n_blocks
    raise ValueError("rows do not split into bf16 sublane tiles")


def _attn_a_call(lam_vec, gain, q, k, vt, kc, vtc, *cast_views, cast_rows=(), tq, heads_per_step,
                 lam_init, stabilize, name):
    b, h, lq, _ = q.shape
    n_lat = 0 if k is None else vt.shape[1]
    nq = lq // tq
    hps = heads_per_step
    nh = h // hps
    n_tiles = b * nh * nq

    def at_cur(f):
        return lambda g: f(g // (nh * nq), (g // nq) % nh, g % nq)

    in_specs = [
        pl.BlockSpec(lam_vec.shape, lambda g: (0, 0)),
        pl.BlockSpec(gain.shape, lambda g: (0, 0)),
        pl.BlockSpec((1, hps, tq, LANES), at_cur(lambda b_, h_, i: (b_, h_, i, 0))),
    ]
    args = [lam_vec, gain, q]
    if n_lat:
        in_specs += [
            pl.BlockSpec((1, hps, k.shape[2], LANES), at_cur(lambda b_, h_, i: (b_, h_, 0, 0))),
            pl.BlockSpec((1, vt.shape[1], hps * LANES, vt.shape[3]),
                         at_cur(lambda b_, h_, i: (b_, 0, h_, 0))),
        ]
        args += [k, vt]
    in_specs += [
        pl.BlockSpec((1, hps, kc.shape[2], LANES), at_cur(lambda b_, h_, i: (b_, h_, 0, 0))),
        pl.BlockSpec((1, vtc.shape[1], hps * LANES, vtc.shape[3]),
                     at_cur(lambda b_, h_, i: (b_, 0, h_, 0))),
    ]
    args += [kc, vtc]
    cast_in, cast_out, cast_shapes = [], [], []
    for w, (row0, n_rows) in zip(cast_views, cast_rows):
        rb, reps = _cast_row_block(n_rows, n_tiles)
        assert row0 % rb == 0

        def block(g, first=row0 // rb, reps=reps):
            return (first + g // reps, 0)

        cast_in.append(pl.BlockSpec((rb, w.shape[1]), block))
        cast_out.append(pl.BlockSpec((rb, w.shape[1]), functools.partial(block, first=0)))
        cast_shapes.append(jax.ShapeDtypeStruct((n_rows, w.shape[1]), BF16))
    outs = pl.pallas_call(
        functools.partial(_attn_a_kernel, n_lat=n_lat, n_cast=len(cast_views),
                          lam_init=lam_init, stabilize=stabilize),
        grid=(n_tiles,),
        in_specs=in_specs + cast_in,
        out_specs=[pl.BlockSpec((1, tq, hps * LANES),
                                at_cur(lambda b_, h_, i: (b_, i, h_)))] + cast_out,
        out_shape=[jax.ShapeDtypeStruct((b, lq, h * LANES), BF16)] + cast_shapes,
        compiler_params=pltpu.CompilerParams(
            dimension_semantics=("arbitrary",),
            vmem_limit_bytes=VMEM_LIMIT),
        name=name + ("_shifted" if stabilize else ""),
    )(*args, *cast_views)
    return outs if cast_views else outs[0]


def _attn_b_kernel(sink_ref, q_ref, kp_ref, km_ref, kn_ref, kc_ref,
                   vp_ref, vm_ref, vn_ref, vc_ref, o_ref, *, stabilize):
    n = pl.program_id(1)
    n_steps = pl.num_programs(1)
    bb = WINDOW_BLOCK
    n_pair = kp_ref.shape[1]
    qb = vm_ref.shape[1]
    ncol = 2 * B_GROUP * bb
    half = ncol // 2

    key_j = lax.broadcasted_iota(jnp.int32, (bb, ncol), 0)
    qry_i = lax.broadcasted_iota(jnp.int32, (bb, ncol), 1) & (bb - 1)
    tri_prev = key_j >= qry_i
    tri_next = key_j <= qry_i
    edge_prev = key_j >= qry_i + jnp.where(n > 0, 0, bb)
    edge_next = key_j <= qry_i - jnp.where(n < n_steps - 1, 0, bb)

    def chains(p):
        rows = slice(p * LANES, (p + 1) * LANES)
        k_blk = lambda j: km_ref[0, p, j * bb:(j + 1) * bb, :]
        v_blk = lambda j: vm_ref[0, j, rows, :]
        for j in range(qb):
            first, final = j == 0, j == qb - 1
            yield (slice(j * bb, (j + 1) * bb),
                   kp_ref[0, p] if first else k_blk(j - 1), k_blk(j),
                   kn_ref[0, p] if final else k_blk(j + 1),
                   vp_ref[0, 0, rows, :] if first else v_blk(j - 1), v_blk(j),
                   vn_ref[0, 0, rows, :] if final else v_blk(j + 1),
                   edge_prev if first else tri_prev, edge_next if final else tri_next)

    def scores(p, chain):
        q_rows, k_prev, k_own, k_next = chain[:4]
        slabs = [q_ref[0, p * B_GROUP + g, q_rows, :].astype(F32) for g in range(B_GROUP)]
        lane = lax.broadcasted_iota(jnp.int32, slabs[0].shape, 1)
        lo = [jnp.where(lane < HEAD_DIM, s, 0.0).astype(BF16) for s in slabs]
        hi = [jnp.where(lane >= HEAD_DIM, s, 0.0).astype(BF16) for s in slabs]
        qq = jnp.concatenate(lo + hi, axis=0)
        kcat = jnp.concatenate([k_prev, k_own, k_next, kc_ref[0, p]], axis=0)
        return lax.dot_general(kcat, qq, (((1,), (1,)), ((), ())), preferred_element_type=F32)

    work = [(p, chain) for p in range(n_pair) for chain in chains(p)]
    st_next = scores(*work[0])
    for idx, (p, chain) in enumerate(work):
        st = st_next
        st_next = scores(*work[idx + 1]) if idx + 1 < len(work) else None
        q_rows, _, _, _, v_prev, v_own, v_next, in_prev, in_next = chain
        sink = sink_ref[p]
        if stabilize:
            s_prev = jnp.where(in_prev, st[0:bb], NEG_INF)
            s_own = st[bb:2 * bb]
            s_next = jnp.where(in_next, st[2 * bb:3 * bb], NEG_INF)
            s_ctx = st[3 * bb:]
            m = jnp.maximum(jnp.maximum(jnp.max(s_prev, axis=0, keepdims=True),
                                        jnp.max(s_own, axis=0, keepdims=True)),
                            jnp.maximum(jnp.max(s_next, axis=0, keepdims=True),
                                        jnp.max(s_ctx, axis=0, keepdims=True)))
            m = jnp.maximum(m, sink)
            parts = [jnp.exp2(s - m) for s in (s_prev, s_own, s_next, s_ctx)]
            l = jnp.exp2(sink - m)
        else:
            parts = [jnp.where(in_prev, jnp.exp2(st[0:bb]), 0.0),
                     jnp.exp2(st[bb:2 * bb]),
                     jnp.where(in_next, jnp.exp2(st[2 * bb:3 * bb]), 0.0),
                     jnp.exp2(st[3 * bb:])]
            l = jnp.exp2(sink)
        for part in parts:
            l = l + jnp.sum(part, axis=0, keepdims=True)
        pt = jnp.concatenate([part.astype(BF16) for part in parts], axis=0)
        rows = slice(p * LANES, (p + 1) * LANES)
        vcat = jnp.concatenate([v_prev, v_own, v_next]
                               + [vc_ref[0, c, rows, :] for c in range(vc_ref.shape[1])], axis=1)
        inv = 1.0 / l
        for e in range(2):
            ot = jnp.dot(vcat[e * HEAD_DIM:(e + 1) * HEAD_DIM], pt[:, e * half:(e + 1) * half],
                         preferred_element_type=F32) * inv[:, e * half:(e + 1) * half]
            for gp in range(B_GROUP // 2):
                blk = jnp.concatenate([ot[:, (2 * gp) * bb:(2 * gp + 1) * bb],
                                       ot[:, (2 * gp + 1) * bb:(2 * gp + 2) * bb]], axis=0)
                c0 = ((p * 2 + e) * (B_GROUP // 2) + gp) * LANES
                o_ref[0, q_rows, c0:c0 + LANES] = blk.T.astype(BF16)


def _attn_b_call(sink_rows, q, k, vt, kc, vtc, *, stabilize, name):
    b, n_slab, l, _ = q.shape
    n_pair = k.shape[1]
    bb = WINDOW_BLOCK
    nb = l // bb
    qb = B_QUERY_BLOCKS
    assert nb % qb == 0 and vt.shape[3] == bb
    prev = lambda n: jnp.maximum(qb * n - 1, 0)
    nxt = lambda n: jnp.minimum(qb * n + qb, nb - 1)
    kspec = lambda f: pl.BlockSpec((1, n_pair, bb, LANES), lambda b_, n: (b_, 0, f(n), 0))
    vspec = lambda f: pl.BlockSpec((1, 1, vt.shape[2], bb), lambda b_, n: (b_, f(n), 0, 0))
    whole = lambda a: pl.BlockSpec((1,) + a.shape[1:], lambda b_, n: (b_,) + (0,) * (a.ndim - 1))
    return pl.pallas_call(
        functools.partial(_attn_b_kernel, stabilize=stabilize),
        grid=(b, nb // qb),
        in_specs=[
            pl.BlockSpec(sink_rows.shape, lambda b_, n: (0, 0, 0)),
            pl.BlockSpec((1, n_slab, qb * bb, LANES), lambda b_, n: (b_, 0, n, 0)),
            kspec(prev),
            pl.BlockSpec((1, n_pair, qb * bb, LANES), lambda b_, n: (b_, 0, n, 0)),
            kspec(nxt), whole(kc),
            vspec(prev),
            pl.BlockSpec((1, qb, vt.shape[2], bb), lambda b_, n: (b_, n, 0, 0)),
            vspec(nxt), whole(vtc),
        ],
        out_specs=pl.BlockSpec((1, qb * bb, n_slab * LANES), lambda b_, n: (b_, n, 0)),
        out_shape=jax.ShapeDtypeStruct((b, l, n_slab * LANES), BF16),
        compiler_params=pltpu.CompilerParams(
            dimension_semantics=("arbitrary", "arbitrary"),
            vmem_limit_bytes=VMEM_LIMIT),
        name=name + ("_shifted" if stabilize else ""),
    )(sink_rows, q, k, k, k, kc, vt, vt, vt, vtc)


def _rope_tables(n_tok):
    rows = n_tok // GRID_W
    nf = HEAD_DIM // 4
    inv = (np.float32(ROPE_BASE) ** (-np.arange(nf, dtype=np.float32) / np.float32(nf)))
    row = np.repeat(np.arange(rows, dtype=np.float32), GRID_W)
    col = np.tile(np.arange(GRID_W, dtype=np.float32), rows)
    ang = np.stack([row[:, None] * inv, col[:, None] * inv], axis=1)
    ang = np.stack([ang, ang], axis=2).reshape(n_tok, HEAD_DIM).astype(np.float32)
    cos = np.tile(np.cos(ang), (1, LANES // HEAD_DIM)).astype(np.float32)
    sin = np.tile(np.sin(ang), (1, LANES // HEAD_DIM)).astype(np.float32)
    first_half = (np.arange(LANES) % (2 * nf)) < nf
    zero = np.float32(0.0)
    return (jnp.asarray(cos), jnp.asarray(np.where(first_half, -sin, zero)),
            jnp.asarray(np.where(first_half, zero, sin)))


def _group_sum_matrix():
    idx = np.arange(MXU_COLS) // HEAD_DIM
    return jnp.asarray(idx[:, None] == idx[None, :], dtype=BF16)


def _score_bound(gq_row, gk_row):
    return 1.05 * HEAD_DIM * jnp.max(jnp.abs(gq_row)) * jnp.max(jnp.abs(gk_row))


def _pair_q_heads(w_qkv):
    order = []
    for p in range(B_KV_HEADS // 2):
        for g in range(B_GROUP):
            order += [(2 * p) * B_GROUP + g, (2 * p + 1) * B_GROUP + g]
    d = w_qkv.shape[0]
    wq = w_qkv[:, :B_Q_HEADS * HEAD_DIM].reshape(d, B_Q_HEADS, HEAD_DIM)
    wq = wq[:, jnp.array(order)].reshape(d, B_Q_HEADS * HEAD_DIM)
    return jnp.concatenate([wq, w_qkv[:, B_Q_HEADS * HEAD_DIM:]], axis=1)


def kernel(x, c, ctx, c_ctx, ada_w, ada_b, ffn_pre_wi, ffn_pre_wo, ffn_post_wi, ffn_post_wo,
           a_w_qkv, a_w_o, a_q_gain, a_k_gain, a_lambda, a_subln_gain,
           b_w_qkv, b_w_o, b_q_gain, b_k_gain, b_sink):
    bsz, seq, d = x.shape
    n_ctx = ctx.shape[1]
    depth = ada_w.shape[0]
    assert depth == 2 and seq % WINDOW_BLOCK == 0

    rows = 8
    cs = jnp.concatenate([c, c_ctx[None], jnp.zeros((rows - bsz - 1, d), F32)], axis=0)
    mods = _ada(cs, ada_w, ada_b).reshape(depth, rows, N_MOD, d)
    lat, ctx_row = None, bsz

    rope_tabs = _rope_tables(seq)
    bd = _group_sum_matrix()
    qk_scale = HEAD_DIM ** -0.5 * LOG2E
    tile2 = lambda g: jnp.tile(g.astype(F32), LANES // HEAD_DIM).reshape(1, LANES)

    pre_wi = _to_bf16(ffn_pre_wi, layer=0, name="cast_pre_wi")
    pre_wo = _to_bf16(ffn_pre_wo, layer=0, name="cast_pre_wo")

    tm = 512
    tm_post = 1024
    tkv = 512
    tq = 2048

    lam_init = 0.8 - 0.6 * math.exp(-0.3 * 0)
    a_cfg = dict(n_q=A_HEADS, n_k=A_HEADS, v_cols=A_HEADS * LANES)
    wqkv_a = _to_bf16(a_w_qkv, layer=0, name="cast_wqkv_a")[0]
    gq_a, gk_a = tile2(a_q_gain[0]) * qk_scale, tile2(a_k_gain[0])
    x1, q, k, vt = _pre_call(x, mods, pre_wi, pre_wo, wqkv_a, bd, gq_a, gk_a, rope_tabs,
                             layer=0, w_layer=0, mod_row=lat, tm=tm, vt_chunk=tkv,
                             name="pre0_lat", **a_cfg)
    xc1, qc, kc, vtc = _pre_call(ctx, mods, pre_wi, pre_wo, wqkv_a, bd, gq_a, gk_a, None,
                                 layer=0, w_layer=0, mod_row=ctx_row, tm=n_ctx, vt_chunk=n_ctx,
                                 name="pre0_ctx", **a_cfg)
    sub_gain = a_subln_gain[0].astype(F32).reshape(1, LANES)
    lam_vec = a_lambda[0].astype(F32)
    unshifted_ok = _score_bound(gq_a, gk_a) <= UNSHIFTED_SCORE_LIMIT

    views = _cast_views([ffn_post_wi, ffn_post_wo, ffn_pre_wi, ffn_pre_wo, a_w_o, b_w_o])
    wi_rows, wo_rows = ffn_pre_wi.shape[1], ffn_pre_wo.shape[1]
    cast_rows = [(0, depth * wi_rows), (0, depth * wo_rows), (wi_rows, wi_rows), (wo_rows, wo_rows),
                 (0, a_w_o.shape[1]), (0, b_w_o.shape[1])]

    def attn_a(stabilize):
        def run(q, k, vt, qc, kc, vtc, *views):
            o, *cast = _attn_a_call(lam_vec, sub_gain, q, k, vt, kc, vtc, *views,
                                    cast_rows=cast_rows, tq=tq, heads_per_step=1,
                                    lam_init=lam_init, stabilize=stabilize, name="attn_a_lat")
            oc = _attn_a_call(lam_vec, sub_gain, qc, None, None, kc, vtc, tq=n_ctx,
                              heads_per_step=A_HEADS, lam_init=lam_init, stabilize=stabilize,
                              name="attn_a_ctx")
            return (o, oc, *cast)
        return run

    o, oc, post_wi, post_wo, pre_wi1, pre_wo1, wout_a, wout_b = lax.cond(
        unshifted_ok, attn_a(False), attn_a(True), q, k, vt, qc, kc, vtc, *views)
    post_wi = post_wi.reshape(ffn_post_wi.shape)
    post_wo = post_wo.reshape(ffn_post_wo.shape)
    pre_wi1 = pre_wi1.reshape((1,) + ffn_pre_wi.shape[1:])
    pre_wo1 = pre_wo1.reshape((1,) + ffn_pre_wo.shape[1:])
    x2 = _post_call(x1, o, mods, wout_a, post_wi, post_wo, layer=0, mod_row=lat, tm=tm_post,
                    name="post0_lat")
    xc2 = _post_call(xc1, oc, mods, wout_a, post_wi, post_wo, layer=0, mod_row=ctx_row,
                     tm=n_ctx, name="post0_ctx")

    b_cfg = dict(n_q=B_Q_HEADS // 2, n_k=B_KV_HEADS // 2, v_cols=B_KV_HEADS * HEAD_DIM)
    wqkv_b = _pair_q_heads(b_w_qkv[0]).astype(BF16)
    gq_b, gk_b = tile2(b_q_gain[0]) * qk_scale, tile2(b_k_gain[0])
    x3, q, k, vt = _pre_call(x2, mods, pre_wi1, pre_wo1, wqkv_b, bd, gq_b, gk_b, rope_tabs,
                             layer=1, w_layer=0, mod_row=lat, tm=tm, vt_chunk=WINDOW_BLOCK,
                             name="pre1_lat", **b_cfg)
    _, _, kc, vtc = _pre_call(xc2, mods, pre_wi1, pre_wo1, wqkv_b, bd, gq_b, gk_b, None,
                              layer=1, w_layer=0, mod_row=ctx_row, tm=n_ctx,
                              vt_chunk=WINDOW_BLOCK, name="pre1_ctx", **b_cfg)
    sink = (b_sink[0].astype(F32) * LOG2E).reshape(B_KV_HEADS // 2, 2 * B_GROUP, 1)
    sink_rows = jnp.broadcast_to(sink, (B_KV_HEADS // 2, 2 * B_GROUP, WINDOW_BLOCK))
    sink_rows = sink_rows.reshape(B_KV_HEADS // 2, 1, 2 * B_GROUP * WINDOW_BLOCK)
    logit_bound = jnp.maximum(_score_bound(gq_b, gk_b), jnp.max(jnp.abs(sink_rows)))
    o = lax.cond(
        logit_bound <= UNSHIFTED_SCORE_LIMIT,
        functools.partial(_attn_b_call, stabilize=False, name="attn_b"),
        functools.partial(_attn_b_call, stabilize=True, name="attn_b"),
        sink_rows, q, k, vt, kc, vtc)
    x4 = _post_call(x3, o, mods, wout_b, post_wi, post_wo, layer=1,
                    mod_row=lat, tm=tm_post, name="post1_lat")
    return x4
```

```python
import functools
import math

import jax
import jax.numpy as jnp
import numpy as np
from jax import lax
from jax.experimental import pallas as pl
from jax.experimental.pallas import tpu as pltpu

F32 = jnp.float32
BF16 = jnp.bfloat16

LANES = 128
BF16_SUBLANES = 16
MXU_COLS = 256
HEAD_DIM = 64
N_MOD = 9
EPS = 1e-6
ROPE_BASE = 10000.0
GRID_W = 64
WINDOW_BLOCK = 128
NEG_INF = -1e30
LOG2E = math.log2(math.e)
UNSHIFTED_SCORE_LIMIT = 96.0
MIB = 1024 * 1024
VMEM_CAP = 56 * MIB
CAST_BLOCK_BYTES = 6 * MIB
PRE_ROW_VMEM = 66 * 1024
POST_ROW_VMEM = 40 * 1024
ATTN_A_QUERY_VMEM = {False: 20 * 1024, True: 28 * 1024}
ATTN_B_BLOCK_VMEM = 3 * MIB


def _vmem_limit(estimate_bytes):
    return int(min(VMEM_CAP, max(estimate_bytes, 16 * MIB)))


def _nbytes(*arrays):
    return sum(a.size * a.dtype.itemsize for a in arrays)

A_HEADS = 8
B_Q_HEADS = 16
B_KV_HEADS = 4
B_GROUP = B_Q_HEADS // B_KV_HEADS
B_QUERY_BLOCKS = 8


def _resident(shape):
    nd = len(shape)
    return pl.BlockSpec(shape, lambda *_: (0,) * nd, pipeline_mode=pl.Buffered(1))


def _resident_layer(stack, layer):
    _, rows, cols = stack.shape
    return pl.BlockSpec((None, rows, cols), lambda *_: (layer, 0, 0),
                        pipeline_mode=pl.Buffered(1))


def _cast_kernel(w_ref, o_ref):
    o_ref[...] = w_ref[...].astype(BF16)


def _to_bf16(w, *, layer, name):
    _, rows, cols = w.shape
    rb = max(r for r in range(8, rows + 1, 8)
             if rows % r == 0 and r * cols * 4 <= CAST_BLOCK_BYTES)
    return pl.pallas_call(
        _cast_kernel, grid=(rows // rb,),
        in_specs=[pl.BlockSpec((1, rb, cols), lambda r: (layer, r, 0))],
        out_specs=pl.BlockSpec((1, rb, cols), lambda r: (0, r, 0)),
        out_shape=jax.ShapeDtypeStruct((1, rows, cols), BF16),
        compiler_params=pltpu.CompilerParams(
            dimension_semantics=("arbitrary",),
            vmem_limit_bytes=_vmem_limit(2 * (rb * cols * 4 + rb * cols * 2) + 2 * MIB)),
        name=name,
    )(w)


def _ada_kernel(cs_ref, w_ref, b_ref, out_ref):
    s = cs_ref[...]
    s = s * jax.nn.sigmoid(s)
    out_ref[0] = jnp.dot(s.astype(BF16), w_ref[0].astype(BF16),
                         preferred_element_type=F32) + b_ref[0]


def _ada(cs, ada_w, ada_b):
    depth, d, n = ada_w.shape
    rows = cs.shape[0]
    tn = n // 4
    return pl.pallas_call(
        _ada_kernel,
        grid=(depth, n // tn),
        in_specs=[
            pl.BlockSpec((rows, d), lambda i, j: (0, 0)),
            pl.BlockSpec((1, d, tn), lambda i, j: (i, 0, j)),
            pl.BlockSpec((1, 1, tn), lambda i, j: (i, 0, j)),
        ],
        out_specs=pl.BlockSpec((1, rows, tn), lambda i, j: (i, 0, j)),
        out_shape=jax.ShapeDtypeStruct((depth, rows, n), F32),
        compiler_params=pltpu.CompilerParams(
            dimension_semantics=("arbitrary", "arbitrary"),
            vmem_limit_bytes=_vmem_limit(d * tn * (2 * 4 + 2) + 2 * MIB)),
        name="ada_mod",
    )(cs, ada_w, ada_b.reshape(depth, 1, n))


def _rms(x):
    return x * lax.rsqrt(jnp.mean(x * x, axis=-1, keepdims=True) + EPS)


def _modulated(x, shift, scale):
    return (_rms(x) * (1.0 + scale) + shift).astype(BF16)


def _ffn_half_step(x, shift, scale, gate, wi_ref, wo_ref):
    d_ff = wo_ref.shape[0]
    h = _modulated(x, shift, scale)
    gu = jnp.dot(h, wi_ref[...], preferred_element_type=F32)
    g = gu[:, :d_ff]
    u = gu[:, d_ff:]
    act = (g * jax.nn.sigmoid(g) * u).astype(BF16)
    ff = jnp.dot(act, wo_ref[...], preferred_element_type=F32)
    return x + (0.5 * gate) * ff


def _pre_kernel(*refs, n_q, n_k, v_cols, rope):
    if rope:
        (x_ref, mod_ref, wi_ref, wo_ref, wqkv_ref, bd_ref, gq_ref, gk_ref,
         cos_ref, sa_ref, sb_ref, x_out, q_out, k_out, vt_out, qkv_sc) = refs
    else:
        (x_ref, mod_ref, wi_ref, wo_ref, wqkv_ref, bd_ref, gq_ref, gk_ref,
         x_out, q_out, k_out, vt_out, qkv_sc) = refs

    step = pl.program_id(0)
    last = pl.num_programs(0) - 1

    @pl.when(step == 0)
    def _():
        qkv_sc[...] = jnp.zeros(qkv_sc.shape, F32)

    def finish_projection():
        bd = bd_ref[...]
        if rope:
            cos, sa, sb = cos_ref[...], sa_ref[...], sb_ref[...]

        def norm_rope_store(col0, n_slabs, gain, out):
            for s in range(0, n_slabs, 2):
                y = qkv_sc[:, col0 + s * LANES: col0 + (s + 2) * LANES]
                ss = jnp.dot((y * y).astype(BF16), bd, preferred_element_type=F32)
                y = y * lax.rsqrt(ss * (1.0 / HEAD_DIM) + EPS)
                for t in range(2):
                    z = y[:, t * LANES:(t + 1) * LANES] * gain
                    if rope:
                        z = (z * cos + pltpu.roll(z, LANES - 16, 1) * sa
                             + pltpu.roll(z, 16, 1) * sb)
                    out[0, s + t] = z.astype(BF16)

        norm_rope_store(0, n_q, gq_ref[...], q_out)
        norm_rope_store(n_q * LANES, n_k, gk_ref[...], k_out)

        v0 = (n_q + n_k) * LANES
        vt = qkv_sc[:, v0:v0 + v_cols].T.astype(BF16)
        chunk = vt_out.shape[3]
        for c in range(vt_out.shape[1]):
            vt_out[0, c] = vt[:, c * chunk:(c + 1) * chunk]

    def tile_matmuls():
        mod = mod_ref[0]
        x1 = _ffn_half_step(x_ref[0], mod[0:1], mod[1:2], mod[2:3], wi_ref, wo_ref)
        x_out[0] = x1
        h = _modulated(x1, mod[3:4], mod[4:5])
        qkv_sc[...] = jnp.dot(h, wqkv_ref[...], preferred_element_type=F32)

    @pl.when(step < last)
    def _():
        finish_projection()
        tile_matmuls()

    @pl.when(step == last)
    def _():
        finish_projection()


def _post_kernel(x_ref, o_ref, mod_ref, wout_ref, wi_ref, wo_ref, x_out):
    mod = mod_ref[0]
    attn = jnp.dot(o_ref[0], wout_ref[...], preferred_element_type=F32)
    x2 = x_ref[0] + mod[5:6] * attn
    x_out[0] = _ffn_half_step(x2, mod[6:7], mod[7:8], mod[8:9], wi_ref, wo_ref)


def _mod_spec(mods, layer, mod_row, wrap):
    _, _, n_mod, d = mods.shape
    return pl.BlockSpec((None, 1, n_mod, d),
                        wrap(lambda b, i: (layer, b if mod_row is None else mod_row, 0, 0)))


def _pre_call(x, mods, wi, wo, wqkv, bd, gq, gk, rope_tabs, *, layer, w_layer, mod_row,
              n_q, n_k, v_cols, tm, vt_chunk, name):
    bx, lx, d = x.shape
    rope = rope_tabs is not None
    nt = lx // tm
    n_tiles = bx * nt
    grid = (n_tiles + 1,)

    def cur(g):
        t = jnp.minimum(g, n_tiles - 1)
        return t // nt, t % nt

    def lag(g):
        t = jnp.maximum(g - 1, 0)
        return t // nt, t % nt

    def at_cur(f):
        return lambda g: f(*cur(g))

    def at_lag(f):
        return lambda g: f(*lag(g))

    in_specs = [
        pl.BlockSpec((1, tm, d), at_cur(lambda b, i: (b, i, 0))),
        _mod_spec(mods, layer, mod_row, at_cur),
        _resident_layer(wi, w_layer), _resident_layer(wo, w_layer), _resident(wqkv.shape),
        _resident(bd.shape), _resident(gq.shape), _resident(gk.shape),
    ]
    args = [x, mods, wi, wo, wqkv, bd, gq, gk]
    if rope:
        in_specs += [pl.BlockSpec((tm, LANES), at_lag(lambda b, i: (i, 0)))] * 3
        args += list(rope_tabs)
    if tm >= vt_chunk:
        vt_block = (1, tm // vt_chunk, v_cols, vt_chunk)
        vt_map = at_lag(lambda b, i: (b, i, 0, 0))
    else:
        per = vt_chunk // tm
        vt_block = (1, 1, v_cols, tm)
        vt_map = at_lag(lambda b, i: (b, i // per, 0, i % per))
    out_specs = [
        pl.BlockSpec((1, tm, d), at_cur(lambda b, i: (b, i, 0))),
        pl.BlockSpec((1, n_q, tm, LANES), at_lag(lambda b, i: (b, 0, i, 0))),
        pl.BlockSpec((1, n_k, tm, LANES), at_lag(lambda b, i: (b, 0, i, 0))),
        pl.BlockSpec(vt_block, vt_map),
    ]
    out_shape = [
        jax.ShapeDtypeStruct((bx, lx, d), F32),
        jax.ShapeDtypeStruct((bx, n_q, lx, LANES), BF16),
        jax.ShapeDtypeStruct((bx, n_k, lx, LANES), BF16),
        jax.ShapeDtypeStruct((bx, lx // vt_chunk, v_cols, vt_chunk), BF16),
    ]
    return pl.pallas_call(
        functools.partial(_pre_kernel, n_q=n_q, n_k=n_k, v_cols=v_cols, rope=rope),
        grid=grid, in_specs=in_specs, out_specs=out_specs, out_shape=out_shape,
        scratch_shapes=[pltpu.VMEM((tm, wqkv.shape[1]), F32)],
        compiler_params=pltpu.CompilerParams(
            dimension_semantics=("arbitrary",),
            vmem_limit_bytes=_vmem_limit(_nbytes(wi[0], wo[0], wqkv) + tm * PRE_ROW_VMEM)),
        name=name,
    )(*args)


def _post_call(x, o, mods, wout, wi, wo, *, layer, mod_row, tm, name):
    bx, lx, d = x.shape
    return pl.pallas_call(
        _post_kernel,
        grid=(bx, lx // tm),
        in_specs=[
            pl.BlockSpec((1, tm, d), lambda b, i: (b, i, 0)),
            pl.BlockSpec((1, tm, o.shape[2]), lambda b, i: (b, i, 0)),
            _mod_spec(mods, layer, mod_row, lambda f: f),
            _resident(wout.shape), _resident_layer(wi, layer), _resident_layer(wo, layer),
        ],
        out_specs=pl.BlockSpec((1, tm, d), lambda b, i: (b, i, 0)),
        out_shape=jax.ShapeDtypeStruct((bx, lx, d), F32),
        compiler_params=pltpu.CompilerParams(
            dimension_semantics=("arbitrary", "arbitrary"),
            vmem_limit_bytes=_vmem_limit(_nbytes(wout, wi[0], wo[0]) + tm * POST_ROW_VMEM)),
        name=name,
    )(x, o, mods, wout, wi, wo)


def _split_components(q_tile):
    qf = q_tile.astype(F32)
    lane = lax.broadcasted_iota(jnp.int32, qf.shape, 1)
    lo = jnp.where(lane < HEAD_DIM, qf, 0.0).astype(BF16)
    hi = jnp.where(lane >= HEAD_DIM, qf, 0.0).astype(BF16)
    return jnp.concatenate([lo, hi], axis=0)


def _attn_a_kernel(*refs, n_lat, n_cast, lam_init, stabilize):
    n_in = len(refs) - 1 - n_cast
    cast_in, cast_out = refs[n_in - n_cast:n_in], refs[n_in + 1:]
    refs = tuple(refs[:n_in - n_cast]) + (refs[n_in],)
    if n_lat:
        lam_ref, gain_ref, q_ref, k_ref, vt_ref, kc_ref, vtc_ref, o_ref = refs
    else:
        lam_ref, gain_ref, q_ref, kc_ref, vtc_ref, o_ref = refs
    tq = q_ref.shape[2]

    def finish(head, l_fin, acc):
        inv = 1.0 / l_fin
        lv = lam_ref[...]
        lam = (jnp.exp(jnp.sum(lv[0:1] * lv[1:2], axis=-1, keepdims=True))
               - jnp.exp(jnp.sum(lv[2:3] * lv[3:4], axis=-1, keepdims=True)) + lam_init)
        ot = acc[:, :tq] * inv[:, :tq] - lam * (acc[:, tq:] * inv[:, tq:])
        on = ot * lax.rsqrt(jnp.mean(ot * ot, axis=0, keepdims=True) + EPS)
        o = (on.T * (gain_ref[...] * (1.0 - lam_init))).astype(BF16)
        o_ref[0, :, head * LANES:(head + 1) * LANES] = o

    for w_in, w_out in zip(cast_in, cast_out):
        w_out[...] = w_in[...].astype(BF16)
    for head in range(q_ref.shape[1]):
        finish(head, *_attn_a_sums(head, q_ref, k_ref if n_lat else None,
                                   vt_ref if n_lat else None, kc_ref, vtc_ref, stabilize))


def _attn_a_sums(head, q_ref, k_ref, vt_ref, kc_ref, vtc_ref, stabilize):
    qq = _split_components(q_ref[0, head])
    v_rows = slice(head * LANES, (head + 1) * LANES)

    def scores(kb):
        return lax.dot_general(kb, qq, (((1,), (1,)), ((), ())), preferred_element_type=F32)

    def update_unshifted(state, st, vtb):
        p = jnp.exp2(st)
        l_new = jnp.sum(p, axis=0, keepdims=True)
        acc_new = jnp.dot(vtb, p.astype(BF16), preferred_element_type=F32)
        if state is not None:
            l_new = state[1] + l_new
            acc_new = state[2] + acc_new
        return None, l_new, acc_new

    def update(state, st, vtb):
        if not stabilize:
            return update_unshifted(state, st, vtb)
        m_blk = jnp.max(st, axis=0, keepdims=True)
        if state is None:
            m_new = m_blk
        else:
            m_old, l_old, acc_old = state
            m_new = jnp.maximum(m_old, m_blk)
            alpha = jnp.exp2(m_old - m_new)
        p = jnp.exp2(st - m_new)
        l_new = jnp.sum(p, axis=0, keepdims=True)
        acc_new = jnp.dot(vtb, p.astype(BF16), preferred_element_type=F32)
        if state is not None:
            l_new = alpha * l_old + l_new
            acc_new = alpha * acc_old + acc_new
        return m_new, l_new, acc_new

    blocks = []
    if k_ref is not None:
        tkv = vt_ref.shape[3]
        for j in range(vt_ref.shape[1]):
            blocks.append((lambda j=j: k_ref[0, head, j * tkv:(j + 1) * tkv, :],
                           lambda j=j: vt_ref[0, j, v_rows, :]))
    ckv = vtc_ref.shape[3]
    for c in range(vtc_ref.shape[1]):
        blocks.append((lambda c=c: kc_ref[0, head, c * ckv:(c + 1) * ckv, :],
                       lambda c=c: vtc_ref[0, c, v_rows, :]))
    state = None
    st = scores(blocks[0][0]())
    for j, (_, load_vt) in enumerate(blocks):
        st_next = scores(blocks[j + 1][0]()) if j + 1 < len(blocks) else None
        state = update(state, st, load_vt())
        st = st_next
    _, l_fin, acc = state
    return l_fin, acc


def _cast_views(stacks):
    return [w.reshape(-1, w.shape[-1]) for w in stacks]


def _cast_row_block(n_rows, n_steps):
    for n_blocks in range(n_steps, 0, -1):
        if n_steps % n_blocks == 0 and n_rows % (BF16_SUBLANES * n_blocks) == 0:
            return n_rows // n_blocks, n_steps // n_blocks
    raise ValueError("rows do not split into bf16 sublane tiles")


def _attn_a_call(lam_vec, gain, q, k, vt, kc, vtc, *cast_views, cast_rows=(), tq, heads_per_step,
                 lam_init, stabilize, name):
    b, h, lq, _ = q.shape
    n_lat = 0 if k is None else vt.shape[1]
    nq = lq // tq
    hps = heads_per_step
    nh = h // hps
    n_tiles = b * nh * nq

    def at_cur(f):
        return lambda g: f(g // (nh * nq), (g // nq) % nh, g % nq)

    in_specs = [
        pl.BlockSpec(lam_vec.shape, lambda g: (0, 0)),
        pl.BlockSpec(gain.shape, lambda g: (0, 0)),
        pl.BlockSpec((1, hps, tq, LANES), at_cur(lambda b_, h_, i: (b_, h_, i, 0))),
    ]
    args = [lam_vec, gain, q]
    if n_lat:
        in_specs += [
            pl.BlockSpec((1, hps, k.shape[2], LANES), at_cur(lambda b_, h_, i: (b_, h_, 0, 0))),
            pl.BlockSpec((1, vt.shape[1], hps * LANES, vt.shape[3]),
                         at_cur(lambda b_, h_, i: (b_, 0, h_, 0))),
        ]
        args += [k, vt]
    in_specs += [
        pl.BlockSpec((1, hps, kc.shape[2], LANES), at_cur(lambda b_, h_, i: (b_, h_, 0, 0))),
        pl.BlockSpec((1, vtc.shape[1], hps * LANES, vtc.shape[3]),
                     at_cur(lambda b_, h_, i: (b_, 0, h_, 0))),
    ]
    args += [kc, vtc]
    cast_in, cast_out, cast_shapes = [], [], []
    for w, (row0, n_rows) in zip(cast_views, cast_rows):
        rb, reps = _cast_row_block(n_rows, n_tiles)
        assert row0 % rb == 0

        def block(g, first=row0 // rb, reps=reps):
            return (first + g // reps, 0)

        cast_in.append(pl.BlockSpec((rb, w.shape[1]), block))
        cast_out.append(pl.BlockSpec((rb, w.shape[1]), functools.partial(block, first=0)))
        cast_shapes.append(jax.ShapeDtypeStruct((n_rows, w.shape[1]), BF16))
    outs = pl.pallas_call(
        functools.partial(_attn_a_kernel, n_lat=n_lat, n_cast=len(cast_views),
                          lam_init=lam_init, stabilize=stabilize),
        grid=(n_tiles,),
        in_specs=in_specs + cast_in,
        out_specs=[pl.BlockSpec((1, tq, hps * LANES),
                                at_cur(lambda b_, h_, i: (b_, i, h_)))] + cast_out,
        out_shape=[jax.ShapeDtypeStruct((b, lq, h * LANES), BF16)] + cast_shapes,
        compiler_params=pltpu.CompilerParams(
            dimension_semantics=("arbitrary",),
            vmem_limit_bytes=_vmem_limit(tq * hps * ATTN_A_QUERY_VMEM[stabilize])),
        name=name + ("_shifted" if stabilize else ""),
    )(*args, *cast_views)
    return outs if cast_views else outs[0]


def _attn_b_kernel(sink_ref, q_ref, kp_ref, km_ref, kn_ref, kc_ref,
                   vp_ref, vm_ref, vn_ref, vc_ref, o_ref, *, stabilize):
    n = pl.program_id(1)
    n_steps = pl.num_programs(1)
    bb = WINDOW_BLOCK
    n_pair = kp_ref.shape[1]
    qb = vm_ref.shape[1]
    ncol = 2 * B_GROUP * bb
    half = ncol // 2

    key_j = lax.broadcasted_iota(jnp.int32, (bb, ncol), 0)
    qry_i = lax.broadcasted_iota(jnp.int32, (bb, ncol), 1) & (bb - 1)
    tri_prev = key_j >= qry_i
    tri_next = key_j <= qry_i
    edge_prev = key_j >= qry_i + jnp.where(n > 0, 0, bb)
    edge_next = key_j <= qry_i - jnp.where(n < n_steps - 1, 0, bb)

    def chains(p):
        rows = slice(p * LANES, (p + 1) * LANES)
        k_blk = lambda j: km_ref[0, p, j * bb:(j + 1) * bb, :]
        v_blk = lambda j: vm_ref[0, j, rows, :]
        for j in range(qb):
            first, final = j == 0, j == qb - 1
            yield (slice(j * bb, (j + 1) * bb),
                   kp_ref[0, p] if first else k_blk(j - 1), k_blk(j),
                   kn_ref[0, p] if final else k_blk(j + 1),
                   vp_ref[0, 0, rows, :] if first else v_blk(j - 1), v_blk(j),
                   vn_ref[0, 0, rows, :] if final else v_blk(j + 1),
                   edge_prev if first else tri_prev, edge_next if final else tri_next)

    def scores(p, chain):
        q_rows, k_prev, k_own, k_next = chain[:4]
        slabs = [q_ref[0, p * B_GROUP + g, q_rows, :].astype(F32) for g in range(B_GROUP)]
        lane = lax.broadcasted_iota(jnp.int32, slabs[0].shape, 1)
        lo = [jnp.where(lane < HEAD_DIM, s, 0.0).astype(BF16) for s in slabs]
        hi = [jnp.where(lane >= HEAD_DIM, s, 0.0).astype(BF16) for s in slabs]
        qq = jnp.concatenate(lo + hi, axis=0)
        kcat = jnp.concatenate([k_prev, k_own, k_next, kc_ref[0, p]], axis=0)
        return lax.dot_general(kcat, qq, (((1,), (1,)), ((), ())), preferred_element_type=F32)

    work = [(p, chain) for p in range(n_pair) for chain in chains(p)]
    st_next = scores(*work[0])
    for idx, (p, chain) in enumerate(work):
        st = st_next
        st_next = scores(*work[idx + 1]) if idx + 1 < len(work) else None
        q_rows, _, _, _, v_prev, v_own, v_next, in_prev, in_next = chain
        sink = sink_ref[p]
        if stabilize:
            s_prev = jnp.where(in_prev, st[0:bb], NEG_INF)
            s_own = st[bb:2 * bb]
            s_next = jnp.where(in_next, st[2 * bb:3 * bb], NEG_INF)
            s_ctx = st[3 * bb:]
            m = jnp.maximum(jnp.maximum(jnp.max(s_prev, axis=0, keepdims=True),
                                        jnp.max(s_own, axis=0, keepdims=True)),
                            jnp.maximum(jnp.max(s_next, axis=0, keepdims=True),
                                        jnp.max(s_ctx, axis=0, keepdims=True)))
            m = jnp.maximum(m, sink)
            parts = [jnp.exp2(s - m) for s in (s_prev, s_own, s_next, s_ctx)]
            l = jnp.exp2(sink - m)
        else:
            parts = [jnp.where(in_prev, jnp.exp2(st[0:bb]), 0.0),
                     jnp.exp2(st[bb:2 * bb]),
                     jnp.where(in_next, jnp.exp2(st[2 * bb:3 * bb]), 0.0),
                     jnp.exp2(st[3 * bb:])]
            l = jnp.exp2(sink)
        for part in parts:
            l = l + jnp.sum(part, axis=0, keepdims=True)
        pt = jnp.concatenate([part.astype(BF16) for part in parts], axis=0)
        rows = slice(p * LANES, (p + 1) * LANES)
        vcat = jnp.concatenate([v_prev, v_own, v_next]
                               + [vc_ref[0, c, rows, :] for c in range(vc_ref.shape[1])], axis=1)
        inv = 1.0 / l
        for e in range(2):
            ot = jnp.dot(vcat[e * HEAD_DIM:(e + 1) * HEAD_DIM], pt[:, e * half:(e + 1) * half],
                         preferred_element_type=F32) * inv[:, e * half:(e + 1) * half]
            for gp in range(B_GROUP // 2):
                blk = jnp.concatenate([ot[:, (2 * gp) * bb:(2 * gp + 1) * bb],
                                       ot[:, (2 * gp + 1) * bb:(2 * gp + 2) * bb]], axis=0)
                c0 = ((p * 2 + e) * (B_GROUP // 2) + gp) * LANES
                o_ref[0, q_rows, c0:c0 + LANES] = blk.T.astype(BF16)


def _attn_b_call(sink_rows, q, k, vt, kc, vtc, *, stabilize, name):
    b, n_slab, l, _ = q.shape
    n_pair = k.shape[1]
    bb = WINDOW_BLOCK
    nb = l // bb
    qb = B_QUERY_BLOCKS
    assert nb % qb == 0 and vt.shape[3] == bb
    prev = lambda n: jnp.maximum(qb * n - 1, 0)
    nxt = lambda n: jnp.minimum(qb * n + qb, nb - 1)
    kspec = lambda f: pl.BlockSpec((1, n_pair, bb, LANES), lambda b_, n: (b_, 0, f(n), 0))
    vspec = lambda f: pl.BlockSpec((1, 1, vt.shape[2], bb), lambda b_, n: (b_, f(n), 0, 0))
    whole = lambda a: pl.BlockSpec((1,) + a.shape[1:], lambda b_, n: (b_,) + (0,) * (a.ndim - 1))
    return pl.pallas_call(
        functools.partial(_attn_b_kernel, stabilize=stabilize),
        grid=(b, nb // qb),
        in_specs=[
            pl.BlockSpec(sink_rows.shape, lambda b_, n: (0, 0, 0)),
            pl.BlockSpec((1, n_slab, qb * bb, LANES), lambda b_, n: (b_, 0, n, 0)),
            kspec(prev),
            pl.BlockSpec((1, n_pair, qb * bb, LANES), lambda b_, n: (b_, 0, n, 0)),
            kspec(nxt), whole(kc),
            vspec(prev),
            pl.BlockSpec((1, qb, vt.shape[2], bb), lambda b_, n: (b_, n, 0, 0)),
            vspec(nxt), whole(vtc),
        ],
        out_specs=pl.BlockSpec((1, qb * bb, n_slab * LANES), lambda b_, n: (b_, n, 0)),
        out_shape=jax.ShapeDtypeStruct((b, l, n_slab * LANES), BF16),
        compiler_params=pltpu.CompilerParams(
            dimension_semantics=("arbitrary", "arbitrary"),
            vmem_limit_bytes=_vmem_limit(qb * ATTN_B_BLOCK_VMEM)),
        name=name + ("_shifted" if stabilize else ""),
    )(sink_rows, q, k, k, k, kc, vt, vt, vt, vtc)


def _rope_tables(n_tok):
    rows = n_tok // GRID_W
    nf = HEAD_DIM // 4
    inv = (np.float32(ROPE_BASE) ** (-np.arange(nf, dtype=np.float32) / np.float32(nf)))
    row = np.repeat(np.arange(rows, dtype=np.float32), GRID_W)
    col = np.tile(np.arange(GRID_W, dtype=np.float32), rows)
    ang = np.stack([row[:, None] * inv, col[:, None] * inv], axis=1)
    ang = np.stack([ang, ang], axis=2).reshape(n_tok, HEAD_DIM).astype(np.float32)
    cos = np.tile(np.cos(ang), (1, LANES // HEAD_DIM)).astype(np.float32)
    sin = np.tile(np.sin(ang), (1, LANES // HEAD_DIM)).astype(np.float32)
    first_half = (np.arange(LANES) % (2 * nf)) < nf
    zero = np.float32(0.0)
    return (jnp.asarray(cos), jnp.asarray(np.where(first_half, -sin, zero)),
            jnp.asarray(np.where(first_half, zero, sin)))


def _group_sum_matrix():
    idx = np.arange(MXU_COLS) // HEAD_DIM
    return jnp.asarray(idx[:, None] == idx[None, :], dtype=BF16)


def _score_bound(gq_row, gk_row):
    return 1.05 * HEAD_DIM * jnp.max(jnp.abs(gq_row)) * jnp.max(jnp.abs(gk_row))


def _pair_q_heads(w_qkv):
    order = []
    for p in range(B_KV_HEADS // 2):
        for g in range(B_GROUP):
            order += [(2 * p) * B_GROUP + g, (2 * p + 1) * B_GROUP + g]
    d = w_qkv.shape[0]
    wq = w_qkv[:, :B_Q_HEADS * HEAD_DIM].reshape(d, B_Q_HEADS, HEAD_DIM)
    wq = wq[:, jnp.array(order)].reshape(d, B_Q_HEADS * HEAD_DIM)
    return jnp.concatenate([wq, w_qkv[:, B_Q_HEADS * HEAD_DIM:]], axis=1)


def kernel(x, c, ctx, c_ctx, ada_w, ada_b, ffn_pre_wi, ffn_pre_wo, ffn_post_wi, ffn_post_wo,
           a_w_qkv, a_w_o, a_q_gain, a_k_gain, a_lambda, a_subln_gain,
           b_w_qkv, b_w_o, b_q_gain, b_k_gain, b_sink):
    bsz, seq, d = x.shape
    n_ctx = ctx.shape[1]
    depth = ada_w.shape[0]
    assert depth == 2 and seq % WINDOW_BLOCK == 0

    rows = 8
    cs = jnp.concatenate([c, c_ctx[None], jnp.zeros((rows - bsz - 1, d), F32)], axis=0)
    mods = _ada(cs, ada_w, ada_b).reshape(depth, rows, N_MOD, d)
    lat, ctx_row = None, bsz

    rope_tabs = _rope_tables(seq)
    bd = _group_sum_matrix()
    qk_scale = HEAD_DIM ** -0.5 * LOG2E
    tile2 = lambda g: jnp.tile(g.astype(F32), LANES // HEAD_DIM).reshape(1, LANES)

    pre_wi = _to_bf16(ffn_pre_wi, layer=0, name="cast_pre_wi")
    pre_wo = _to_bf16(ffn_pre_wo, layer=0, name="cast_pre_wo")

    tm = 512
    tm_post = 512
    tkv = 512
    tq = 2048

    lam_init = 0.8 - 0.6 * math.exp(-0.3 * 0)
    a_cfg = dict(n_q=A_HEADS, n_k=A_HEADS, v_cols=A_HEADS * LANES)
    wqkv_a = _to_bf16(a_w_qkv, layer=0, name="cast_wqkv_a")[0]
    gq_a, gk_a = tile2(a_q_gain[0]) * qk_scale, tile2(a_k_gain[0])
    x1, q, k, vt = _pre_call(x, mods, pre_wi, pre_wo, wqkv_a, bd, gq_a, gk_a, rope_tabs,
                             layer=0, w_layer=0, mod_row=lat, tm=tm, vt_chunk=tkv,
                             name="pre0_lat", **a_cfg)
    xc1, qc, kc, vtc = _pre_call(ctx, mods, pre_wi, pre_wo, wqkv_a, bd, gq_a, gk_a, None,
                                 layer=0, w_layer=0, mod_row=ctx_row, tm=n_ctx, vt_chunk=n_ctx,
                                 name="pre0_ctx", **a_cfg)
    sub_gain = a_subln_gain[0].astype(F32).reshape(1, LANES)
    lam_vec = a_lambda[0].astype(F32)
    unshifted_ok = _score_bound(gq_a, gk_a) <= UNSHIFTED_SCORE_LIMIT

    views = _cast_views([ffn_post_wi, ffn_post_wo, ffn_pre_wi, ffn_pre_wo, a_w_o, b_w_o])
    wi_rows, wo_rows = ffn_pre_wi.shape[1], ffn_pre_wo.shape[1]
    cast_rows = [(0, depth * wi_rows), (0, depth * wo_rows), (wi_rows, wi_rows), (wo_rows, wo_rows),
                 (0, a_w_o.shape[1]), (0, b_w_o.shape[1])]

    def attn_a(stabilize):
        def run(q, k, vt, qc, kc, vtc, *views):
            o, *cast = _attn_a_call(lam_vec, sub_gain, q, k, vt, kc, vtc, *views,
                                    cast_rows=cast_rows, tq=tq, heads_per_step=1,
                                    lam_init=lam_init, stabilize=stabilize, name="attn_a_lat")
            oc = _attn_a_call(lam_vec, sub_gain, qc, None, None, kc, vtc, tq=n_ctx,
                              heads_per_step=A_HEADS, lam_init=lam_init, stabilize=stabilize,
                              name="attn_a_ctx")
            return (o, oc, *cast)
        return run

    o, oc, post_wi, post_wo, pre_wi1, pre_wo1, wout_a, wout_b = lax.cond(
        unshifted_ok, attn_a(False), attn_a(True), q, k, vt, qc, kc, vtc, *views)
    post_wi = post_wi.reshape(ffn_post_wi.shape)
    post_wo = post_wo.reshape(ffn_post_wo.shape)
    pre_wi1 = pre_wi1.reshape((1,) + ffn_pre_wi.shape[1:])
    pre_wo1 = pre_wo1.reshape((1,) + ffn_pre_wo.shape[1:])
    x2 = _post_call(x1, o, mods, wout_a, post_wi, post_wo, layer=0, mod_row=lat, tm=tm_post,
                    name="post0_lat")
    xc2 = _post_call(xc1, oc, mods, wout_a, post_wi, post_wo, layer=0, mod_row=ctx_row,
                     tm=n_ctx, name="post0_ctx")

    b_cfg = dict(n_q=B_Q_HEADS // 2, n_k=B_KV_HEADS // 2, v_cols=B_KV_HEADS * HEAD_DIM)
    wqkv_b = _pair_q_heads(b_w_qkv[0]).astype(BF16)
    gq_b, gk_b = tile2(b_q_gain[0]) * qk_scale, tile2(b_k_gain[0])
    x3, q, k, vt = _pre_call(x2, mods, pre_wi1, pre_wo1, wqkv_b, bd, gq_b, gk_b, rope_tabs,
                             layer=1, w_layer=0, mod_row=lat, tm=tm, vt_chunk=WINDOW_BLOCK,
                             name="pre1_lat", **b_cfg)
    _, _, kc, vtc = _pre_call(xc2, mods, pre_wi1, pre_wo1, wqkv_b, bd, gq_b, gk_b, None,
                              layer=1, w_layer=0, mod_row=ctx_row, tm=n_ctx,
                              vt_chunk=WINDOW_BLOCK, name="pre1_ctx", **b_cfg)
    sink = (b_sink[0].astype(F32) * LOG2E).reshape(B_KV_HEADS // 2, 2 * B_GROUP, 1)
    sink_rows = jnp.broadcast_to(sink, (B_KV_HEADS // 2, 2 * B_GROUP, WINDOW_BLOCK))
    sink_rows = sink_rows.reshape(B_KV_HEADS // 2, 1, 2 * B_GROUP * WINDOW_BLOCK)
    logit_bound = jnp.maximum(_score_bound(gq_b, gk_b), jnp.max(jnp.abs(sink_rows)))
    o = lax.cond(
        logit_bound <= UNSHIFTED_SCORE_LIMIT,
        functools.partial(_attn_b_call, stabilize=False, name="attn_b"),
        functools.partial(_attn_b_call, stabilize=True, name="attn_b"),
        sink_rows, q, k, vt, kc, vtc)
    x4 = _post_call(x3, o, mods, wout_b, post_wi, post_wo, layer=1,
                    mod_row=lat, tm=tm_post, name="post1_lat")
    return x4
```

```python
import functools
import math

import jax
import jax.numpy as jnp
import numpy as np
from jax import lax
from jax.experimental import pallas as pl
from jax.experimental.pallas import tpu as pltpu

F32 = jnp.float32
BF16 = jnp.bfloat16

LANES = 128
BF16_SUBLANES = 16
MXU_COLS = 256
HEAD_DIM = 64
N_MOD = 9
EPS = 1e-6
ROPE_BASE = 10000.0
GRID_W = 64
WINDOW_BLOCK = 128
NEG_INF = -1e30
LOG2E = math.log2(math.e)
UNSHIFTED_SCORE_LIMIT = 96.0
MIB = 1024 * 1024
VMEM_CAP = 56 * MIB
CAST_BLOCK_BYTES = 6 * MIB
PRE_ROW_VMEM = 66 * 1024
POST_ROW_VMEM = 40 * 1024
ATTN_A_QUERY_VMEM = {False: 20 * 1024, True: 28 * 1024}
ATTN_B_BLOCK_VMEM = 3 * MIB


def _vmem_limit(estimate_bytes):
    return int(min(VMEM_CAP, max(estimate_bytes, 16 * MIB)))


def _nbytes(*arrays):
    return sum(a.size * a.dtype.itemsize for a in arrays)

A_HEADS = 8
B_Q_HEADS = 16
B_KV_HEADS = 4
B_GROUP = B_Q_HEADS // B_KV_HEADS
B_QUERY_BLOCKS = 16


def _resident(shape):
    nd = len(shape)
    return pl.BlockSpec(shape, lambda *_: (0,) * nd, pipeline_mode=pl.Buffered(1))


def _resident_layer(stack, layer):
    _, rows, cols = stack.shape
    return pl.BlockSpec((None, rows, cols), lambda *_: (layer, 0, 0),
                        pipeline_mode=pl.Buffered(1))


def _cast_kernel(w_ref, o_ref):
    o_ref[...] = w_ref[...].astype(BF16)


def _to_bf16(w, *, layer, name):
    _, rows, cols = w.shape
    rb = max(r for r in range(8, rows + 1, 8)
             if rows % r == 0 and r * cols * 4 <= CAST_BLOCK_BYTES)
    return pl.pallas_call(
        _cast_kernel, grid=(rows // rb,),
        in_specs=[pl.BlockSpec((1, rb, cols), lambda r: (layer, r, 0))],
        out_specs=pl.BlockSpec((1, rb, cols), lambda r: (0, r, 0)),
        out_shape=jax.ShapeDtypeStruct((1, rows, cols), BF16),
        compiler_params=pltpu.CompilerParams(
            dimension_semantics=("arbitrary",),
            vmem_limit_bytes=_vmem_limit(2 * (rb * cols * 4 + rb * cols * 2) + 2 * MIB)),
        name=name,
    )(w)


def _ada_kernel(cs_ref, w_ref, b_ref, out_ref):
    s = cs_ref[...]
    s = s * jax.nn.sigmoid(s)
    out_ref[0] = jnp.dot(s.astype(BF16), w_ref[0].astype(BF16),
                         preferred_element_type=F32) + b_ref[0]


def _ada(cs, ada_w, ada_b):
    depth, d, n = ada_w.shape
    rows = cs.shape[0]
    tn = n // 4
    return pl.pallas_call(
        _ada_kernel,
        grid=(depth, n // tn),
        in_specs=[
            pl.BlockSpec((rows, d), lambda i, j: (0, 0)),
            pl.BlockSpec((1, d, tn), lambda i, j: (i, 0, j)),
            pl.BlockSpec((1, 1, tn), lambda i, j: (i, 0, j)),
        ],
        out_specs=pl.BlockSpec((1, rows, tn), lambda i, j: (i, 0, j)),
        out_shape=jax.ShapeDtypeStruct((depth, rows, n), F32),
        compiler_params=pltpu.CompilerParams(
            dimension_semantics=("arbitrary", "arbitrary"),
            vmem_limit_bytes=_vmem_limit(d * tn * (2 * 4 + 2) + 2 * MIB)),
        name="ada_mod",
    )(cs, ada_w, ada_b.reshape(depth, 1, n))


def _rms(x):
    return x * lax.rsqrt(jnp.mean(x * x, axis=-1, keepdims=True) + EPS)


def _modulated(x, shift, scale):
    return (_rms(x) * (1.0 + scale) + shift).astype(BF16)


def _ffn_half_step(x, shift, scale, gate, wi_ref, wo_ref):
    d_ff = wo_ref.shape[0]
    h = _modulated(x, shift, scale)
    gu = jnp.dot(h, wi_ref[...], preferred_element_type=F32)
    g = gu[:, :d_ff]
    u = gu[:, d_ff:]
    act = (g * jax.nn.sigmoid(g) * u).astype(BF16)
    ff = jnp.dot(act, wo_ref[...], preferred_element_type=F32)
    return x + (0.5 * gate) * ff


def _pre_kernel(*refs, n_q, n_k, v_cols, rope):
    if rope:
        (x_ref, mod_ref, wi_ref, wo_ref, wqkv_ref, bd_ref, gq_ref, gk_ref,
         cos_ref, sa_ref, sb_ref, x_out, q_out, k_out, vt_out, qkv_sc) = refs
    else:
        (x_ref, mod_ref, wi_ref, wo_ref, wqkv_ref, bd_ref, gq_ref, gk_ref,
         x_out, q_out, k_out, vt_out, qkv_sc) = refs

    step = pl.program_id(0)
    last = pl.num_programs(0) - 1

    @pl.when(step == 0)
    def _():
        qkv_sc[...] = jnp.zeros(qkv_sc.shape, F32)

    def finish_projection():
        bd = bd_ref[...]
        if rope:
            cos, sa, sb = cos_ref[...], sa_ref[...], sb_ref[...]

        def norm_rope_store(col0, n_slabs, gain, out):
            for s in range(0, n_slabs, 2):
                y = qkv_sc[:, col0 + s * LANES: col0 + (s + 2) * LANES]
                ss = jnp.dot((y * y).astype(BF16), bd, preferred_element_type=F32)
                y = y * lax.rsqrt(ss * (1.0 / HEAD_DIM) + EPS)
                for t in range(2):
                    z = y[:, t * LANES:(t + 1) * LANES] * gain
                    if rope:
                        z = (z * cos + pltpu.roll(z, LANES - 16, 1) * sa
                             + pltpu.roll(z, 16, 1) * sb)
                    out[0, s + t] = z.astype(BF16)

        norm_rope_store(0, n_q, gq_ref[...], q_out)
        norm_rope_store(n_q * LANES, n_k, gk_ref[...], k_out)

        v0 = (n_q + n_k) * LANES
        vt = qkv_sc[:, v0:v0 + v_cols].T.astype(BF16)
        chunk = vt_out.shape[3]
        for c in range(vt_out.shape[1]):
            vt_out[0, c] = vt[:, c * chunk:(c + 1) * chunk]

    def tile_matmuls():
        mod = mod_ref[0]
        x1 = _ffn_half_step(x_ref[0], mod[0:1], mod[1:2], mod[2:3], wi_ref, wo_ref)
        x_out[0] = x1
        h = _modulated(x1, mod[3:4], mod[4:5])
        qkv_sc[...] = jnp.dot(h, wqkv_ref[...], preferred_element_type=F32)

    @pl.when(step < last)
    def _():
        finish_projection()
        tile_matmuls()

    @pl.when(step == last)
    def _():
        finish_projection()


def _post_kernel(x_ref, o_ref, mod_ref, wout_ref, wi_ref, wo_ref, x_out):
    mod = mod_ref[0]
    attn = jnp.dot(o_ref[0], wout_ref[...], preferred_element_type=F32)
    x2 = x_ref[0] + mod[5:6] * attn
    x_out[0] = _ffn_half_step(x2, mod[6:7], mod[7:8], mod[8:9], wi_ref, wo_ref)


def _mod_spec(mods, layer, mod_row, wrap):
    _, _, n_mod, d = mods.shape
    return pl.BlockSpec((None, 1, n_mod, d),
                        wrap(lambda b, i: (layer, b if mod_row is None else mod_row, 0, 0)))


def _pre_call(x, mods, wi, wo, wqkv, bd, gq, gk, rope_tabs, *, layer, w_layer, mod_row,
              n_q, n_k, v_cols, tm, vt_chunk, name):
    bx, lx, d = x.shape
    rope = rope_tabs is not None
    nt = lx // tm
    n_tiles = bx * nt
    grid = (n_tiles + 1,)

    def cur(g):
        t = jnp.minimum(g, n_tiles - 1)
        return t // nt, t % nt

    def lag(g):
        t = jnp.maximum(g - 1, 0)
        return t // nt, t % nt

    def at_cur(f):
        return lambda g: f(*cur(g))

    def at_lag(f):
        return lambda g: f(*lag(g))

    in_specs = [
        pl.BlockSpec((1, tm, d), at_cur(lambda b, i: (b, i, 0))),
        _mod_spec(mods, layer, mod_row, at_cur),
        _resident_layer(wi, w_layer), _resident_layer(wo, w_layer), _resident(wqkv.shape),
        _resident(bd.shape), _resident(gq.shape), _resident(gk.shape),
    ]
    args = [x, mods, wi, wo, wqkv, bd, gq, gk]
    if rope:
        in_specs += [pl.BlockSpec((tm, LANES), at_lag(lambda b, i: (i, 0)))] * 3
        args += list(rope_tabs)
    if tm >= vt_chunk:
        vt_block = (1, tm // vt_chunk, v_cols, vt_chunk)
        vt_map = at_lag(lambda b, i: (b, i, 0, 0))
    else:
        per = vt_chunk // tm
        vt_block = (1, 1, v_cols, tm)
        vt_map = at_lag(lambda b, i: (b, i // per, 0, i % per))
    out_specs = [
        pl.BlockSpec((1, tm, d), at_cur(lambda b, i: (b, i, 0))),
        pl.BlockSpec((1, n_q, tm, LANES), at_lag(lambda b, i: (b, 0, i, 0))),
        pl.BlockSpec((1, n_k, tm, LANES), at_lag(lambda b, i: (b, 0, i, 0))),
        pl.BlockSpec(vt_block, vt_map),
    ]
    out_shape = [
        jax.ShapeDtypeStruct((bx, lx, d), F32),
        jax.ShapeDtypeStruct((bx, n_q, lx, LANES), BF16),
        jax.ShapeDtypeStruct((bx, n_k, lx, LANES), BF16),
        jax.ShapeDtypeStruct((bx, lx // vt_chunk, v_cols, vt_chunk), BF16),
    ]
    return pl.pallas_call(
        functools.partial(_pre_kernel, n_q=n_q, n_k=n_k, v_cols=v_cols, rope=rope),
        grid=grid, in_specs=in_specs, out_specs=out_specs, out_shape=out_shape,
        scratch_shapes=[pltpu.VMEM((tm, wqkv.shape[1]), F32)],
        compiler_params=pltpu.CompilerParams(
            dimension_semantics=("arbitrary",),
            vmem_limit_bytes=_vmem_limit(_nbytes(wi[0], wo[0], wqkv) + tm * PRE_ROW_VMEM)),
        name=name,
    )(*args)


def _post_call(x, o, mods, wout, wi, wo, *, layer, mod_row, tm, name):
    bx, lx, d = x.shape
    return pl.pallas_call(
        _post_kernel,
        grid=(bx, lx // tm),
        in_specs=[
            pl.BlockSpec((1, tm, d), lambda b, i: (b, i, 0)),
            pl.BlockSpec((1, tm, o.shape[2]), lambda b, i: (b, i, 0)),
            _mod_spec(mods, layer, mod_row, lambda f: f),
            _resident(wout.shape), _resident_layer(wi, layer), _resident_layer(wo, layer),
        ],
        out_specs=pl.BlockSpec((1, tm, d), lambda b, i: (b, i, 0)),
        out_shape=jax.ShapeDtypeStruct((bx, lx, d), F32),
        compiler_params=pltpu.CompilerParams(
            dimension_semantics=("arbitrary", "arbitrary"),
            vmem_limit_bytes=_vmem_limit(_nbytes(wout, wi[0], wo[0]) + tm * POST_ROW_VMEM)),
        name=name,
    )(x, o, mods, wout, wi, wo)


def _split_components(q_tile):
    qf = q_tile.astype(F32)
    lane = lax.broadcasted_iota(jnp.int32, qf.shape, 1)
    lo = jnp.where(lane < HEAD_DIM, qf, 0.0).astype(BF16)
    hi = jnp.where(lane >= HEAD_DIM, qf, 0.0).astype(BF16)
    return jnp.concatenate([lo, hi], axis=0)


def _attn_a_kernel(*refs, n_lat, n_cast, lam_init, stabilize):
    n_in = len(refs) - 1 - n_cast
    cast_in, cast_out = refs[n_in - n_cast:n_in], refs[n_in + 1:]
    refs = tuple(refs[:n_in - n_cast]) + (refs[n_in],)
    if n_lat:
        lam_ref, gain_ref, q_ref, k_ref, vt_ref, kc_ref, vtc_ref, o_ref = refs
    else:
        lam_ref, gain_ref, q_ref, kc_ref, vtc_ref, o_ref = refs
    tq = q_ref.shape[2]

    def finish(head, l_fin, acc):
        inv = 1.0 / l_fin
        lv = lam_ref[...]
        lam = (jnp.exp(jnp.sum(lv[0:1] * lv[1:2], axis=-1, keepdims=True))
               - jnp.exp(jnp.sum(lv[2:3] * lv[3:4], axis=-1, keepdims=True)) + lam_init)
        ot = acc[:, :tq] * inv[:, :tq] - lam * (acc[:, tq:] * inv[:, tq:])
        on = ot * lax.rsqrt(jnp.mean(ot * ot, axis=0, keepdims=True) + EPS)
        o = (on.T * (gain_ref[...] * (1.0 - lam_init))).astype(BF16)
        o_ref[0, :, head * LANES:(head + 1) * LANES] = o

    for w_in, w_out in zip(cast_in, cast_out):
        w_out[...] = w_in[...].astype(BF16)
    for head in range(q_ref.shape[1]):
        finish(head, *_attn_a_sums(head, q_ref, k_ref if n_lat else None,
                                   vt_ref if n_lat else None, kc_ref, vtc_ref, stabilize))


def _attn_a_sums(head, q_ref, k_ref, vt_ref, kc_ref, vtc_ref, stabilize):
    qq = _split_components(q_ref[0, head])
    v_rows = slice(head * LANES, (head + 1) * LANES)

    def scores(kb):
        return lax.dot_general(kb, qq, (((1,), (1,)), ((), ())), preferred_element_type=F32)

    def update_unshifted(state, st, vtb):
        p = jnp.exp2(st)
        l_new = jnp.sum(p, axis=0, keepdims=True)
        acc_new = jnp.dot(vtb, p.astype(BF16), preferred_element_type=F32)
        if state is not None:
            l_new = state[1] + l_new
            acc_new = state[2] + acc_new
        return None, l_new, acc_new

    def update(state, st, vtb):
        if not stabilize:
            return update_unshifted(state, st, vtb)
        m_blk = jnp.max(st, axis=0, keepdims=True)
        if state is None:
            m_new = m_blk
        else:
            m_old, l_old, acc_old = state
            m_new = jnp.maximum(m_old, m_blk)
            alpha = jnp.exp2(m_old - m_new)
        p = jnp.exp2(st - m_new)
        l_new = jnp.sum(p, axis=0, keepdims=True)
        acc_new = jnp.dot(vtb, p.astype(BF16), preferred_element_type=F32)
        if state is not None:
            l_new = alpha * l_old + l_new
            acc_new = alpha * acc_old + acc_new
        return m_new, l_new, acc_new

    blocks = []
    if k_ref is not None:
        tkv = vt_ref.shape[3]
        for j in range(vt_ref.shape[1]):
            blocks.append((lambda j=j: k_ref[0, head, j * tkv:(j + 1) * tkv, :],
                           lambda j=j: vt_ref[0, j, v_rows, :]))
    ckv = vtc_ref.shape[3]
    for c in range(vtc_ref.shape[1]):
        blocks.append((lambda c=c: kc_ref[0, head, c * ckv:(c + 1) * ckv, :],
                       lambda c=c: vtc_ref[0, c, v_rows, :]))
    state = None
    st = scores(blocks[0][0]())
    for j, (_, load_vt) in enumerate(blocks):
        st_next = scores(blocks[j + 1][0]()) if j + 1 < len(blocks) else None
        state = update(state, st, load_vt())
        st = st_next
    _, l_fin, acc = state
    return l_fin, acc


def _cast_views(stacks):
    return [w.reshape(-1, w.shape[-1]) for w in stacks]


def _cast_row_block(n_rows, n_steps):
    for n_blocks in range(n_steps, 0, -1):
        if n_steps % n_blocks == 0 and n_rows % (BF16_SUBLANES * n_blocks) == 0:
            return n_rows // n_blocks, n_steps // n_blocks
    raise ValueError("rows do not split into bf16 sublane tiles")


def _attn_a_call(lam_vec, gain, q, k, vt, kc, vtc, *cast_views, cast_rows=(), tq, heads_per_step,
                 lam_init, stabilize, name):
    b, h, lq, _ = q.shape
    n_lat = 0 if k is None else vt.shape[1]
    nq = lq // tq
    hps = heads_per_step
    nh = h // hps
    n_tiles = b * nh * nq

    def at_cur(f):
        return lambda g: f(g // (nh * nq), (g // nq) % nh, g % nq)

    in_specs = [
        pl.BlockSpec(lam_vec.shape, lambda g: (0, 0)),
        pl.BlockSpec(gain.shape, lambda g: (0, 0)),
        pl.BlockSpec((1, hps, tq, LANES), at_cur(lambda b_, h_, i: (b_, h_, i, 0))),
    ]
    args = [lam_vec, gain, q]
    if n_lat:
        in_specs += [
            pl.BlockSpec((1, hps, k.shape[2], LANES), at_cur(lambda b_, h_, i: (b_, h_, 0, 0))),
            pl.BlockSpec((1, vt.shape[1], hps * LANES, vt.shape[3]),
                         at_cur(lambda b_, h_, i: (b_, 0, h_, 0))),
        ]
        args += [k, vt]
    in_specs += [
        pl.BlockSpec((1, hps, kc.shape[2], LANES), at_cur(lambda b_, h_, i: (b_, h_, 0, 0))),
        pl.BlockSpec((1, vtc.shape[1], hps * LANES, vtc.shape[3]),
                     at_cur(lambda b_, h_, i: (b_, 0, h_, 0))),
    ]
    args += [kc, vtc]
    cast_in, cast_out, cast_shapes = [], [], []
    for w, (row0, n_rows) in zip(cast_views, cast_rows):
        rb, reps = _cast_row_block(n_rows, n_tiles)
        assert row0 % rb == 0

        def block(g, first=row0 // rb, reps=reps):
            return (first + g // reps, 0)

        cast_in.append(pl.BlockSpec((rb, w.shape[1]), block))
        cast_out.append(pl.BlockSpec((rb, w.shape[1]), functools.partial(block, first=0)))
        cast_shapes.append(jax.ShapeDtypeStruct((n_rows, w.shape[1]), BF16))
    outs = pl.pallas_call(
        functools.partial(_attn_a_kernel, n_lat=n_lat, n_cast=len(cast_views),
                          lam_init=lam_init, stabilize=stabilize),
        grid=(n_tiles,),
        in_specs=in_specs + cast_in,
        out_specs=[pl.BlockSpec((1, tq, hps * LANES),
                                at_cur(lambda b_, h_, i: (b_, i, h_)))] + cast_out,
        out_shape=[jax.ShapeDtypeStruct((b, lq, h * LANES), BF16)] + cast_shapes,
        compiler_params=pltpu.CompilerParams(
            dimension_semantics=("arbitrary",),
            vmem_limit_bytes=_vmem_limit(tq * hps * ATTN_A_QUERY_VMEM[stabilize])),
        name=name + ("_shifted" if stabilize else ""),
    )(*args, *cast_views)
    return outs if cast_views else outs[0]


def _attn_b_kernel(sink_ref, q_ref, kp_ref, km_ref, kn_ref, kc_ref,
                   vp_ref, vm_ref, vn_ref, vc_ref, o_ref, *, stabilize):
    n = pl.program_id(1)
    n_steps = pl.num_programs(1)
    bb = WINDOW_BLOCK
    n_pair = kp_ref.shape[1]
    qb = vm_ref.shape[1]
    ncol = 2 * B_GROUP * bb
    half = ncol // 2

    key_j = lax.broadcasted_iota(jnp.int32, (bb, ncol), 0)
    qry_i = lax.broadcasted_iota(jnp.int32, (bb, ncol), 1) & (bb - 1)
    tri_prev = key_j >= qry_i
    tri_next = key_j <= qry_i
    edge_prev = key_j >= qry_i + jnp.where(n > 0, 0, bb)
    edge_next = key_j <= qry_i - jnp.where(n < n_steps - 1, 0, bb)

    def chains(p):
        rows = slice(p * LANES, (p + 1) * LANES)
        k_blk = lambda j: km_ref[0, p, j * bb:(j + 1) * bb, :]
        v_blk = lambda j: vm_ref[0, j, rows, :]
        for j in range(qb):
            first, final = j == 0, j == qb - 1
            yield (slice(j * bb, (j + 1) * bb),
                   kp_ref[0, p] if first else k_blk(j - 1), k_blk(j),
                   kn_ref[0, p] if final else k_blk(j + 1),
                   vp_ref[0, 0, rows, :] if first else v_blk(j - 1), v_blk(j),
                   vn_ref[0, 0, rows, :] if final else v_blk(j + 1),
                   edge_prev if first else tri_prev, edge_next if final else tri_next)

    def scores(p, chain):
        q_rows, k_prev, k_own, k_next = chain[:4]
        slabs = [q_ref[0, p * B_GROUP + g, q_rows, :].astype(F32) for g in range(B_GROUP)]
        lane = lax.broadcasted_iota(jnp.int32, slabs[0].shape, 1)
        lo = [jnp.where(lane < HEAD_DIM, s, 0.0).astype(BF16) for s in slabs]
        hi = [jnp.where(lane >= HEAD_DIM, s, 0.0).astype(BF16) for s in slabs]
        qq = jnp.concatenate(lo + hi, axis=0)
        kcat = jnp.concatenate([k_prev, k_own, k_next, kc_ref[0, p]], axis=0)
        return lax.dot_general(kcat, qq, (((1,), (1,)), ((), ())), preferred_element_type=F32)

    work = [(p, chain) for p in range(n_pair) for chain in chains(p)]
    st_next = scores(*work[0])
    for idx, (p, chain) in enumerate(work):
        st = st_next
        st_next = scores(*work[idx + 1]) if idx + 1 < len(work) else None
        q_rows, _, _, _, v_prev, v_own, v_next, in_prev, in_next = chain
        sink = sink_ref[p]
        if stabilize:
            s_prev = jnp.where(in_prev, st[0:bb], NEG_INF)
            s_own = st[bb:2 * bb]
            s_next = jnp.where(in_next, st[2 * bb:3 * bb], NEG_INF)
            s_ctx = st[3 * bb:]
            m = jnp.maximum(jnp.maximum(jnp.max(s_prev, axis=0, keepdims=True),
                                        jnp.max(s_own, axis=0, keepdims=True)),
                            jnp.maximum(jnp.max(s_next, axis=0, keepdims=True),
                                        jnp.max(s_ctx, axis=0, keepdims=True)))
            m = jnp.maximum(m, sink)
            parts = [jnp.exp2(s - m) for s in (s_prev, s_own, s_next, s_ctx)]
            l = jnp.exp2(sink - m)
        else:
            parts = [jnp.where(in_prev, jnp.exp2(st[0:bb]), 0.0),
                     jnp.exp2(st[bb:2 * bb]),
                     jnp.where(in_next, jnp.exp2(st[2 * bb:3 * bb]), 0.0),
                     jnp.exp2(st[3 * bb:])]
            l = jnp.exp2(sink)
        for part in parts:
            l = l + jnp.sum(part, axis=0, keepdims=True)
        pt = jnp.concatenate([part.astype(BF16) for part in parts], axis=0)
        rows = slice(p * LANES, (p + 1) * LANES)
        vcat = jnp.concatenate([v_prev, v_own, v_next]
                               + [vc_ref[0, c, rows, :] for c in range(vc_ref.shape[1])], axis=1)
        inv = 1.0 / l
        for e in range(2):
            ot = jnp.dot(vcat[e * HEAD_DIM:(e + 1) * HEAD_DIM], pt[:, e * half:(e + 1) * half],
                         preferred_element_type=F32) * inv[:, e * half:(e + 1) * half]
            for gp in range(B_GROUP // 2):
                blk = jnp.concatenate([ot[:, (2 * gp) * bb:(2 * gp + 1) * bb],
                                       ot[:, (2 * gp + 1) * bb:(2 * gp + 2) * bb]], axis=0)
                c0 = ((p * 2 + e) * (B_GROUP // 2) + gp) * LANES
                o_ref[0, q_rows, c0:c0 + LANES] = blk.T.astype(BF16)


def _attn_b_call(sink_rows, q, k, vt, kc, vtc, *, stabilize, name):
    b, n_slab, l, _ = q.shape
    n_pair = k.shape[1]
    bb = WINDOW_BLOCK
    nb = l // bb
    qb = B_QUERY_BLOCKS
    assert nb % qb == 0 and vt.shape[3] == bb
    prev = lambda n: jnp.maximum(qb * n - 1, 0)
    nxt = lambda n: jnp.minimum(qb * n + qb, nb - 1)
    kspec = lambda f: pl.BlockSpec((1, n_pair, bb, LANES), lambda b_, n: (b_, 0, f(n), 0))
    vspec = lambda f: pl.BlockSpec((1, 1, vt.shape[2], bb), lambda b_, n: (b_, f(n), 0, 0))
    whole = lambda a: pl.BlockSpec((1,) + a.shape[1:], lambda b_, n: (b_,) + (0,) * (a.ndim - 1))
    return pl.pallas_call(
        functools.partial(_attn_b_kernel, stabilize=stabilize),
        grid=(b, nb // qb),
        in_specs=[
            pl.BlockSpec(sink_rows.shape, lambda b_, n: (0, 0, 0)),
            pl.BlockSpec((1, n_slab, qb * bb, LANES), lambda b_, n: (b_, 0, n, 0)),
            kspec(prev),
            pl.BlockSpec((1, n_pair, qb * bb, LANES), lambda b_, n: (b_, 0, n, 0)),
            kspec(nxt), whole(kc),
            vspec(prev),
            pl.BlockSpec((1, qb, vt.shape[2], bb), lambda b_, n: (b_, n, 0, 0)),
            vspec(nxt), whole(vtc),
        ],
        out_specs=pl.BlockSpec((1, qb * bb, n_slab * LANES), lambda b_, n: (b_, n, 0)),
        out_shape=jax.ShapeDtypeStruct((b, l, n_slab * LANES), BF16),
        compiler_params=pltpu.CompilerParams(
            dimension_semantics=("arbitrary", "arbitrary"),
            vmem_limit_bytes=_vmem_limit(qb * ATTN_B_BLOCK_VMEM)),
        name=name + ("_shifted" if stabilize else ""),
    )(sink_rows, q, k, k, k, kc, vt, vt, vt, vtc)


def _rope_tables(n_tok):
    rows = n_tok // GRID_W
    nf = HEAD_DIM // 4
    inv = (np.float32(ROPE_BASE) ** (-np.arange(nf, dtype=np.float32) / np.float32(nf)))
    row = np.repeat(np.arange(rows, dtype=np.float32), GRID_W)
    col = np.tile(np.arange(GRID_W, dtype=np.float32), rows)
    ang = np.stack([row[:, None] * inv, col[:, None] * inv], axis=1)
    ang = np.stack([ang, ang], axis=2).reshape(n_tok, HEAD_DIM).astype(np.float32)
    cos = np.tile(np.cos(ang), (1, LANES // HEAD_DIM)).astype(np.float32)
    sin = np.tile(np.sin(ang), (1, LANES // HEAD_DIM)).astype(np.float32)
    first_half = (np.arange(LANES) % (2 * nf)) < nf
    zero = np.float32(0.0)
    return (jnp.asarray(cos), jnp.asarray(np.where(first_half, -sin, zero)),
            jnp.asarray(np.where(first_half, zero, sin)))


def _group_sum_matrix():
    idx = np.arange(MXU_COLS) // HEAD_DIM
    return jnp.asarray(idx[:, None] == idx[None, :], dtype=BF16)


def _score_bound(gq_row, gk_row):
    return 1.05 * HEAD_DIM * jnp.max(jnp.abs(gq_row)) * jnp.max(jnp.abs(gk_row))


def _pair_q_heads(w_qkv):
    order = []
    for p in range(B_KV_HEADS // 2):
        for g in range(B_GROUP):
            order += [(2 * p) * B_GROUP + g, (2 * p + 1) * B_GROUP + g]
    d = w_qkv.shape[0]
    wq = w_qkv[:, :B_Q_HEADS * HEAD_DIM].reshape(d, B_Q_HEADS, HEAD_DIM)
    wq = wq[:, jnp.array(order)].reshape(d, B_Q_HEADS * HEAD_DIM)
    return jnp.concatenate([wq, w_qkv[:, B_Q_HEADS * HEAD_DIM:]], axis=1)


def kernel(x, c, ctx, c_ctx, ada_w, ada_b, ffn_pre_wi, ffn_pre_wo, ffn_post_wi, ffn_post_wo,
           a_w_qkv, a_w_o, a_q_gain, a_k_gain, a_lambda, a_subln_gain,
           b_w_qkv, b_w_o, b_q_gain, b_k_gain, b_sink):
    bsz, seq, d = x.shape
    n_ctx = ctx.shape[1]
    depth = ada_w.shape[0]
    assert depth == 2 and seq % WINDOW_BLOCK == 0

    rows = 8
    cs = jnp.concatenate([c, c_ctx[None], jnp.zeros((rows - bsz - 1, d), F32)], axis=0)
    mods = _ada(cs, ada_w, ada_b).reshape(depth, rows, N_MOD, d)
    lat, ctx_row = None, bsz

    rope_tabs = _rope_tables(seq)
    bd = _group_sum_matrix()
    qk_scale = HEAD_DIM ** -0.5 * LOG2E
    tile2 = lambda g: jnp.tile(g.astype(F32), LANES // HEAD_DIM).reshape(1, LANES)

    pre_wi = _to_bf16(ffn_pre_wi, layer=0, name="cast_pre_wi")
    pre_wo = _to_bf16(ffn_pre_wo, layer=0, name="cast_pre_wo")

    tm = 512
    tm_post = 512
    tkv = 512
    tq = 2048

    lam_init = 0.8 - 0.6 * math.exp(-0.3 * 0)
    a_cfg = dict(n_q=A_HEADS, n_k=A_HEADS, v_cols=A_HEADS * LANES)
    wqkv_a = _to_bf16(a_w_qkv, layer=0, name="cast_wqkv_a")[0]
    gq_a, gk_a = tile2(a_q_gain[0]) * qk_scale, tile2(a_k_gain[0])
    x1, q, k, vt = _pre_call(x, mods, pre_wi, pre_wo, wqkv_a, bd, gq_a, gk_a, rope_tabs,
                             layer=0, w_layer=0, mod_row=lat, tm=tm, vt_chunk=tkv,
                             name="pre0_lat", **a_cfg)
    xc1, qc, kc, vtc = _pre_call(ctx, mods, pre_wi, pre_wo, wqkv_a, bd, gq_a, gk_a, None,
                                 layer=0, w_layer=0, mod_row=ctx_row, tm=n_ctx, vt_chunk=n_ctx,
                                 name="pre0_ctx", **a_cfg)
    sub_gain = a_subln_gain[0].astype(F32).reshape(1, LANES)
    lam_vec = a_lambda[0].astype(F32)
    unshifted_ok = _score_bound(gq_a, gk_a) <= UNSHIFTED_SCORE_LIMIT

    views = _cast_views([ffn_post_wi, ffn_post_wo, ffn_pre_wi, ffn_pre_wo, a_w_o, b_w_o])
    wi_rows, wo_rows = ffn_pre_wi.shape[1], ffn_pre_wo.shape[1]
    cast_rows = [(0, depth * wi_rows), (0, depth * wo_rows), (wi_rows, wi_rows), (wo_rows, wo_rows),
                 (0, a_w_o.shape[1]), (0, b_w_o.shape[1])]

    def attn_a(stabilize):
        def run(q, k, vt, qc, kc, vtc, *views):
            o, *cast = _attn_a_call(lam_vec, sub_gain, q, k, vt, kc, vtc, *views,
                                    cast_rows=cast_rows, tq=tq, heads_per_step=1,
                                    lam_init=lam_init, stabilize=stabilize, name="attn_a_lat")
            oc = _attn_a_call(lam_vec, sub_gain, qc, None, None, kc, vtc, tq=n_ctx,
                              heads_per_step=A_HEADS, lam_init=lam_init, stabilize=stabilize,
                              name="attn_a_ctx")
            return (o, oc, *cast)
        return run

    o, oc, post_wi, post_wo, pre_wi1, pre_wo1, wout_a, wout_b = lax.cond(
        unshifted_ok, attn_a(False), attn_a(True), q, k, vt, qc, kc, vtc, *views)
    post_wi = post_wi.reshape(ffn_post_wi.shape)
    post_wo = post_wo.reshape(ffn_post_wo.shape)
    pre_wi1 = pre_wi1.reshape((1,) + ffn_pre_wi.shape[1:])
    pre_wo1 = pre_wo1.reshape((1,) + ffn_pre_wo.shape[1:])
    x2 = _post_call(x1, o, mods, wout_a, post_wi, post_wo, layer=0, mod_row=lat, tm=tm_post,
                    name="post0_lat")
    xc2 = _post_call(xc1, oc, mods, wout_a, post_wi, post_wo, layer=0, mod_row=ctx_row,
                     tm=n_ctx, name="post0_ctx")

    b_cfg = dict(n_q=B_Q_HEADS // 2, n_k=B_KV_HEADS // 2, v_cols=B_KV_HEADS * HEAD_DIM)
    wqkv_b = _pair_q_heads(b_w_qkv[0]).astype(BF16)
    gq_b, gk_b = tile2(b_q_gain[0]) * qk_scale, tile2(b_k_gain[0])
    x3, q, k, vt = _pre_call(x2, mods, pre_wi1, pre_wo1, wqkv_b, bd, gq_b, gk_b, rope_tabs,
                             layer=1, w_layer=0, mod_row=lat, tm=tm, vt_chunk=WINDOW_BLOCK,
                             name="pre1_lat", **b_cfg)
    _, _, kc, vtc = _pre_call(xc2, mods, pre_wi1, pre_wo1, wqkv_b, bd, gq_b, gk_b, None,
                              layer=1, w_layer=0, mod_row=ctx_row, tm=n_ctx,
                              vt_chunk=WINDOW_BLOCK, name="pre1_ctx", **b_cfg)
    sink = (b_sink[0].astype(F32) * LOG2E).reshape(B_KV_HEADS // 2, 2 * B_GROUP, 1)
    sink_rows = jnp.broadcast_to(sink, (B_KV_HEADS // 2, 2 * B_GROUP, WINDOW_BLOCK))
    sink_rows = sink_rows.reshape(B_KV_HEADS // 2, 1, 2 * B_GROUP * WINDOW_BLOCK)
    logit_bound = jnp.maximum(_score_bound(gq_b, gk_b), jnp.max(jnp.abs(sink_rows)))
    o = lax.cond(
        logit_bound <= UNSHIFTED_SCORE_LIMIT,
        functools.partial(_attn_b_call, stabilize=False, name="attn_b"),
        functools.partial(_attn_b_call, stabilize=True, name="attn_b"),
        sink_rows, q, k, vt, kc, vtc)
    x4 = _post_call(x3, o, mods, wout_b, post_wi, post_wo, layer=1,
                    mod_row=lat, tm=tm_post, name="post1_lat")
    return x4
```
